```python
import math
import jax, jax.numpy as jnp
from jax import lax
import numpy as np

D_MODEL = 2048
BATCH = 2
SEQ = 4096
DEPTH = 4

CTX_LEN = 256
GRID_W = 64
HEAD_DIM = 128

A_WIDTH = D_MODEL // 4
A_HEADS = A_WIDTH // HEAD_DIM
A_CHUNK = 128

B_WIDTH = D_MODEL // 2
B_QK_DIM = HEAD_DIM
B_V_DIM = 2 * HEAD_DIM
B_HEADS = B_WIDTH // B_V_DIM
B_QK_WIDTH = B_HEADS * 2 * B_QK_DIM
Q_BLOCK = 128
ROPE_THETA = 10000.0

C_WIDTH = D_MODEL // 4
C_VAL_DIM = HEAD_DIM
C_KEY_DIM = HEAD_DIM
C_HEADS = C_WIDTH // C_VAL_DIM
C_KEY_WIDTH = C_HEADS * C_KEY_DIM
SCAN_CHUNK = 64

IN_SPLITS = (A_WIDTH, A_WIDTH,
             B_QK_WIDTH, B_QK_WIDTH, B_WIDTH,
             C_KEY_WIDTH, C_KEY_WIDTH, C_KEY_WIDTH,
             C_WIDTH, C_WIDTH)
IN_WIDTH = sum(IN_SPLITS)
MIX_WIDTH = A_WIDTH + B_WIDTH + C_WIDTH

N_EXPERTS = 64
TOP_K = 8
EXPERT_DIM = 384
ROUTED_SCALE = 2.5
MOE_BLOCK = 128

DEEPNORM_ALPHA = (2.0 * DEPTH) ** 0.25
DEEPNORM_BETA = (8.0 * DEPTH) ** -0.25
LN_EPS = 1e-5
NORM_EPS = 1e-5
F32 = jnp.float32

kernel_name = 'hybrid_gmlp_diffattn_hgrn2_moe_dit'


def layer_norm(x, g, b):
    xf = x.astype(F32)
    mu = jnp.mean(xf, axis=-1, keepdims=True)
    var = jnp.mean(jnp.square(xf - mu), axis=-1, keepdims=True)
    return ((xf - mu) * lax.rsqrt(var + LN_EPS)).astype(x.dtype) * g + b


def rms_norm(x, g):
    xf = x.astype(F32)
    return (xf * lax.rsqrt(jnp.mean(jnp.square(xf), axis=-1, keepdims=True) + NORM_EPS)).astype(x.dtype) * g


def modulate(h, shift, scale):
    return h * (1 + scale) + shift


def axial_rope_tables(n):
    t = jnp.arange(n)
    row = (t // GRID_W).astype(F32)
    col = (t % GRID_W).astype(F32)
    axis_dim = HEAD_DIM // 2
    inv = ROPE_THETA ** (-jnp.arange(0, axis_dim, 2, dtype=F32) / axis_dim)
    ar = row[:, None] * inv
    ac = col[:, None] * inv
    cos = jnp.concatenate([jnp.cos(ar), jnp.cos(ar), jnp.cos(ac), jnp.cos(ac)], axis=-1)
    sin = jnp.concatenate([jnp.sin(ar), jnp.sin(ar), jnp.sin(ac), jnp.sin(ac)], axis=-1)
    return cos, sin


def apply_axial_rope(x, cos, sin):
    x1, x2, x3, x4 = jnp.split(x, 4, axis=-1)
    rot = jnp.concatenate([-x2, x1, -x4, x3], axis=-1)
    c = cos[:, None, None, :]
    s = sin[:, None, None, :]
    return (x.astype(F32) * c + rot.astype(F32) * s).astype(x.dtype)


def chunk_gmlp(u, v, ln_g, ln_b, w_s, b_s):
    bsz, n, _ = u.shape
    u = jax.nn.gelu(u)
    v = jax.nn.gelu(v).reshape(bsz, n // A_CHUNK, A_CHUNK, A_HEADS, HEAD_DIM)
    v = layer_norm(v, ln_g.reshape(A_HEADS, HEAD_DIM), ln_b.reshape(A_HEADS, HEAD_DIM))
    s = jnp.einsum('hpq,bcqhd->bcphd', w_s, v) + b_s.T[:, :, None]
    return u * s.reshape(bsz, n, A_WIDTH)


def diff_attend(q, k, v, lam):
    s = jnp.einsum('bqhmd,bshmd->bhmqs', q, k).astype(F32) * (B_QK_DIM ** -0.5)
    p = jax.nn.softmax(s, axis=-1)
    a = p[:, :, 0] - lam * p[:, :, 1]
    return jnp.einsum('bhqs,bshe->bqhe', a.astype(v.dtype), v)


def diff_attention(q_lat, k_lat, v_lat, q_ctx, k_ctx, v_ctx, lam, lam_init, subln_g, need_ctx):
    bsz, n = q_lat.shape[0], q_lat.shape[1]
    cos, sin = axial_rope_tables(n)
    q_lat = apply_axial_rope(q_lat, cos, sin)
    k_lat = apply_axial_rope(k_lat, cos, sin)
    k_all = jnp.concatenate([k_ctx, k_lat], axis=1)
    v_all = jnp.concatenate([v_ctx, v_lat], axis=1)
    nb = n // Q_BLOCK
    qb = q_lat.reshape(bsz, nb, Q_BLOCK, B_HEADS, 2, B_QK_DIM).swapaxes(0, 1)
    o = lax.map(lambda blk: diff_attend(blk, k_all, v_all, lam), qb)
    o = o.swapaxes(0, 1).reshape(bsz, n, B_HEADS, B_V_DIM)

    def post(t):
        return (rms_norm(t, subln_g) * (1.0 - lam_init)).reshape(t.shape[0], t.shape[1], B_WIDTH)

    out_lat = post(o)
    out_ctx = post(diff_attend(q_ctx, k_ctx, v_ctx, lam)) if need_ctx else None
    return out_lat, out_ctx


def gla_chunk_scan(q, k, v, logf, s0):
    bsz, h, n, _ = q.shape
    dv = v.shape[-1]
    nc = n // SCAN_CHUNK

    def to_chunks(t):
        return jnp.moveaxis(t.astype(F32).reshape(bsz, h, nc, SCAN_CHUNK, t.shape[-1]), 2, 0)

    mask = jnp.tril(jnp.ones((SCAN_CHUNK, SCAN_CHUNK), dtype=bool))[:, :, None]

    def step(S, inp):
        qc, kc, vc, gc = inp
        G = jnp.cumsum(gc, axis=-2)
        diff = G[..., :, None, :] - G[..., None, :, :]
        decay = jnp.exp(jnp.where(mask, diff, -jnp.inf))
        A = jnp.einsum('bhtd,bhsd,bhtsd->bhts', qc, kc, decay)
        o = jnp.einsum('bhts,bhse->bhte', A, vc) + jnp.einsum('bhtd,bhde->bhte', qc * jnp.exp(G), S)
        G_end = G[..., -1:, :]
        S = jnp.exp(G_end)[..., 0, :, None] * S + jnp.einsum('bhsd,bhse->bhde', kc * jnp.exp(G_end - G), vc)
        return S, o

    S, o = lax.scan(step, s0.astype(F32), (to_chunks(q), to_chunks(k), to_chunks(v), to_chunks(logf)))
    o = jnp.moveaxis(o, 0, 2).reshape(bsz, h, n, dv).astype(q.dtype)
    return o, S


def gla_final_state(k, v, logf, s0):
    G = jnp.cumsum(logf.astype(F32), axis=2)
    G_end = G[:, :, -1:, :]
    return (jnp.exp(G_end)[:, :, 0, :, None] * s0
            + jnp.einsum('bhsd,bhse->bhde', k.astype(F32) * jnp.exp(G_end - G), v.astype(F32)))


def _heads(t, h, d):
    bsz, n, _ = t.shape
    return t.reshape(bsz, n, h, d).transpose(0, 2, 1, 3)


def hgrn_gates(f_raw, lb):
    lb = lb.astype(F32).reshape(C_HEADS, 1, C_KEY_DIM)
    f = lb + (1.0 - lb) * jax.nn.sigmoid(_heads(f_raw, C_HEADS, C_KEY_DIM).astype(F32))
    return jnp.log(f), 1.0 - f


def hgrn2_mixer(lat, ctxp, lb_f, lb_b, norm_g, need_ctx):
    q_l, ff_l, fb_l, i_l, g_l = lat
    q_c, ff_c, fb_c, i_c, g_c = ctxp
    bsz = q_l.shape[0]
    s0 = jnp.zeros((bsz, C_HEADS, C_KEY_DIM, C_VAL_DIM), F32)
    flip = lambda t: jnp.flip(t, axis=2)

    def readout(o, gate):
        n = o.shape[2]
        o = rms_norm(o.transpose(0, 2, 1, 3), norm_g)
        return (o * jax.nn.silu(gate.reshape(bsz, n, C_HEADS, C_VAL_DIM))).reshape(bsz, n, C_WIDTH)

    ci = _heads(i_c, C_HEADS, C_VAL_DIM)
    clf, ckf = hgrn_gates(ff_c, lb_f)
    clb, ckb = hgrn_gates(fb_c, lb_b)
    if need_ctx:
        cq = _heads(jax.nn.silu(q_c), C_HEADS, C_KEY_DIM)
        o_cf, S_f = gla_chunk_scan(cq, ckf, ci, clf, s0)
        o_cb, S_b = gla_chunk_scan(flip(cq), flip(ckb), flip(ci), flip(clb), s0)
        out_ctx = readout(o_cf + flip(o_cb), g_c)
    else:
        S_f = gla_final_state(ckf, ci, clf, s0)
        S_b = gla_final_state(flip(ckb), flip(ci), flip(clb), s0)
        out_ctx = None

    lq = _heads(jax.nn.silu(q_l), C_HEADS, C_KEY_DIM)
    li = _heads(i_l, C_HEADS, C_VAL_DIM)
    llf, lkf = hgrn_gates(ff_l, lb_f)
    llb, lkb = hgrn_gates(fb_l, lb_b)
    o_f, _ = gla_chunk_scan(lq, lkf, li, llf, S_f)
    o_b, _ = gla_chunk_scan(flip(lq), flip(lkb), flip(li), flip(llb), S_b)
    out_lat = readout(o_f + flip(o_b), g_l)
    return out_lat, out_ctx


def moe_ffn(h, router_w, router_b, w_gate, w_up, w_down, s_gate, s_up, s_down):
    T = h.shape[0]
    scores = jax.nn.sigmoid((h @ router_w).astype(F32))
    _, idx = lax.top_k(scores + router_b.astype(F32), TOP_K)
    w = jnp.take_along_axis(scores, idx, axis=-1)
    w = w / jnp.sum(w, axis=-1, keepdims=True) * ROUTED_SCALE
    combine = jnp.sum(jax.nn.one_hot(idx, N_EXPERTS, dtype=F32) * w[..., None], axis=1).astype(h.dtype)
    nblk = T // MOE_BLOCK

    def block(args):
        hb, cb = args
        a = jax.nn.silu(jnp.einsum('td,edf->tef', hb, w_gate)) * jnp.einsum('td,edf->tef', hb, w_up)
        return jnp.einsum('tef,efd->td', a * cb[..., None], w_down)

    routed = lax.map(block, (h.reshape(nblk, MOE_BLOCK, -1),
                             combine.reshape(nblk, MOE_BLOCK, N_EXPERTS))).reshape(T, -1)
    shared = (jax.nn.silu(h @ s_gate) * (h @ s_up)) @ s_down
    return routed + shared


def setup_inputs(seed: int = 0) -> dict:
    key = jax.random.key(seed)
    ks = iter(jax.random.split(key, 32))
    D, E, F = D_MODEL, N_EXPERTS, EXPERT_DIM

    def nrm(shape, scale):
        return jax.random.normal(next(ks), shape, jnp.float32) * scale

    return {
        'x': nrm((BATCH, SEQ, D), 1.0),
        'c': nrm((BATCH, D), 1.0),
        'ctx': nrm((BATCH, CTX_LEN, D), 1.0),
        'c_ctx': nrm((D,), 1.0),
        'w_mod': nrm((DEPTH, D, 6 * D), 0.5 * D ** -0.5),
        'b_mod': nrm((DEPTH, 6 * D), 0.01),
        'w_in': nrm((DEPTH, D, IN_WIDTH), D ** -0.5),
        'w_out': nrm((DEPTH, MIX_WIDTH, D), DEEPNORM_BETA * MIX_WIDTH ** -0.5),
        'gmlp_ln_g': 1.0 + nrm((DEPTH, A_WIDTH), 0.01),
        'gmlp_ln_b': nrm((DEPTH, A_WIDTH), 0.01),
        'gmlp_ws': nrm((DEPTH, A_HEADS, A_CHUNK, A_CHUNK), A_CHUNK ** -0.5),
        'gmlp_bs': 1.0 + nrm((DEPTH, A_HEADS, A_CHUNK), 0.01),
        'diff_lam': nrm((DEPTH, 4, B_QK_DIM), 0.1),
        'diff_subln_g': 1.0 + nrm((DEPTH, B_V_DIM), 0.01),
        'hgrn_lb': nrm((DEPTH, 2, C_KEY_WIDTH), 0.5),
        'hgrn_norm_g': 1.0 + nrm((DEPTH, C_VAL_DIM), 0.01),
        'ln1_g': 1.0 + nrm((DEPTH, D), 0.01),
        'ln1_b': nrm((DEPTH, D), 0.01),
        'ln2_g': 1.0 + nrm((DEPTH, D), 0.01),
        'ln2_b': nrm((DEPTH, D), 0.01),
        'router_w': nrm((DEPTH, D, E), D ** -0.5),
        'router_b': nrm((DEPTH, E), 0.01),
        'exp_w_gate': nrm((DEPTH, E, D, F), D ** -0.5),
        'exp_w_up': nrm((DEPTH, E, D, F), D ** -0.5),
        'exp_w_down': nrm((DEPTH, E, F, D), DEEPNORM_BETA * F ** -0.5),
        'sh_w_gate': nrm((DEPTH, D, F), D ** -0.5),
        'sh_w_up': nrm((DEPTH, D, F), D ** -0.5),
        'sh_w_down': nrm((DEPTH, F, D), DEEPNORM_BETA * F ** -0.5),
    }


def reference(x, c, ctx, c_ctx, w_mod, b_mod, w_in, w_out, gmlp_ln_g, gmlp_ln_b, gmlp_ws, gmlp_bs,
              diff_lam, diff_subln_g, hgrn_lb, hgrn_norm_g, ln1_g, ln1_b, ln2_g, ln2_b,
              router_w, router_b, exp_w_gate, exp_w_up, exp_w_down, sh_w_gate, sh_w_up, sh_w_down):
    bsz, n, D = x.shape
    L = ctx.shape[1]
    split_idx = [int(v) for v in np.cumsum(IN_SPLITS)[:-1]]
    sm = jax.nn.softmax(hgrn_lb.astype(F32), axis=0)
    lb_all = jnp.cumsum(sm, axis=0) - sm[0]
    silu_c = jax.nn.silu(c)
    silu_cc = jax.nn.silu(c_ctx)
    cx = ctx
    for l in range(DEPTH):
        need_ctx = l < DEPTH - 1
        mod = silu_c @ w_mod[l] + b_mod[l]
        modc = silu_cc @ w_mod[l] + b_mod[l]
        sh1, sc1, g1, sh2, sc2, g2 = jnp.split(mod[:, None, :], 6, axis=-1)
        sh1c, sc1c, g1c, sh2c, sc2c, g2c = jnp.split(modc, 6)

        px = jnp.split(modulate(x, sh1, sc1) @ w_in[l], split_idx, axis=-1)
        pc = jnp.split(modulate(cx, sh1c, sc1c) @ w_in[l], split_idx, axis=-1)
        au, av, bq, bk, bv, cq, cff, cfb, ci, cg = px
        au_c, av_c, bq_c, bk_c, bv_c, cq_c, cff_c, cfb_c, ci_c, cg_c = pc

        a_x = chunk_gmlp(au, av, gmlp_ln_g[l], gmlp_ln_b[l], gmlp_ws[l], gmlp_bs[l])

        lam_init = 0.8 - 0.6 * math.exp(-0.3 * l)
        dl = diff_lam[l].astype(F32)
        lam = jnp.exp(jnp.sum(dl[0] * dl[1])) - jnp.exp(jnp.sum(dl[2] * dl[3])) + lam_init
        qk_shape = (B_HEADS, 2, B_QK_DIM)
        b_x, b_c = diff_attention(
            bq.reshape(bsz, n, *qk_shape), bk.reshape(bsz, n, *qk_shape), bv.reshape(bsz, n, B_HEADS, B_V_DIM),
            bq_c.reshape(bsz, L, *qk_shape), bk_c.reshape(bsz, L, *qk_shape), bv_c.reshape(bsz, L, B_HEADS, B_V_DIM),
            lam, lam_init, diff_subln_g[l], need_ctx)

        c_x, c_c = hgrn2_mixer((cq, cff, cfb, ci, cg), (cq_c, cff_c, cfb_c, ci_c, cg_c),
                               lb_all[l, 0], lb_all[l, 1], hgrn_norm_g[l], need_ctx)

        mix_x = jnp.concatenate([a_x, b_x, c_x], axis=-1) @ w_out[l]
        x = layer_norm(DEEPNORM_ALPHA * x + g1 * mix_x, ln1_g[l], ln1_b[l])

        hx = modulate(x, sh2, sc2)
        moe_args = (router_w[l], router_b[l], exp_w_gate[l], exp_w_up[l], exp_w_down[l],
                    sh_w_gate[l], sh_w_up[l], sh_w_down[l])
        if need_ctx:
            a_c = chunk_gmlp(au_c, av_c, gmlp_ln_g[l], gmlp_ln_b[l], gmlp_ws[l], gmlp_bs[l])
            mix_c = jnp.concatenate([a_c, b_c, c_c], axis=-1) @ w_out[l]
            cx = layer_norm(DEEPNORM_ALPHA * cx + g1c * mix_c, ln1_g[l], ln1_b[l])
            hc = modulate(cx, sh2c, sc2c)
            y = moe_ffn(jnp.concatenate([hc, hx], axis=1).reshape(-1, D), *moe_args).reshape(bsz, L + n, D)
            y_c, y_x = y[:, :L], y[:, L:]
            cx = layer_norm(DEEPNORM_ALPHA * cx + g2c * y_c, ln2_g[l], ln2_b[l])
        else:
            y_x = moe_ffn(hx.reshape(-1, D), *moe_args).reshape(bsz, n, D)
        x = layer_norm(DEEPNORM_ALPHA * x + g2 * y_x, ln2_g[l], ln2_b[l])
    return x
```

```python
import functools
import math

import numpy as np
import jax
import jax.numpy as jnp
from jax import lax
from jax.experimental import pallas as pl
from jax.experimental.pallas import tpu as pltpu

F32 = jnp.float32
BF16 = jnp.bfloat16

D_MODEL = 2048
BATCH = 2
SEQ = 4096
DEPTH = 4
CTX_LEN = 256
GRID_W = 64
HEAD_DIM = 128
A_WIDTH = 512
A_HEADS = 4
A_CHUNK = 128
B_WIDTH = 1024
B_HEADS = 4
B_QK_WIDTH = 1024
B_V_DIM = 256
ROPE_THETA = 10000.0
C_WIDTH = 512
C_HEADS = 4
IN_WIDTH = 6656
N_EXPERTS = 64
TOP_K = 8
EXPERT_DIM = 384
ROUTED_SCALE = 2.5
DEEPNORM_ALPHA = (2.0 * DEPTH) ** 0.25
LN_EPS = 1e-5
NORM_EPS = 1e-5

N_LAT = BATCH * SEQ
N_CTX = BATCH * CTX_LEN
T_ROWS = N_LAT + N_CTX
LOW_WIDTH = 4096
HI_WIDTH = IN_WIDTH - LOW_WIDTH
QK_SCALE = HEAD_DIM ** -0.5

LANE = 128
PROJ_TM = 512
PROJ_TN = 512
ATT_TQ = 256
ATT_TK = 512
SCAN_C = 128
SCAN_LEVELS = 7
OUT_TM = 256
MOE_TM = 256
COMBINE_TM = 128
MOE_ROWS = ((T_ROWS * TOP_K + N_EXPERTS * (MOE_TM - 1)) // MOE_TM) * MOE_TM
MOE_TILES = MOE_ROWS // MOE_TM
VMEM_LIMIT = 48 * 1024 * 1024


def _cparams(sem, vmem=VMEM_LIMIT):
    return pltpu.CompilerParams(dimension_semantics=sem, vmem_limit_bytes=vmem)


def _sigmoid(x):
    return 1.0 / (1.0 + jnp.exp(-x))


def _gelu_tanh(x):
    return 0.5 * x * (1.0 + jnp.tanh(math.sqrt(2.0 / math.pi) * (x + 0.044715 * (x * x * x))))


def _dot_nt(a, b):
    return lax.dot_general(a, b, (((1,), (1,)), ((), ())), preferred_element_type=F32)


def _dot_tn(a, b):
    return lax.dot_general(a, b, (((0,), (0,)), ((), ())), preferred_element_type=F32)


def _mod_kernel(c_ref, w_ref, b_ref, o_ref):
    c = c_ref[...]
    s = (c * _sigmoid(c)).astype(BF16)
    o_ref[0] = jnp.dot(s, w_ref[0].astype(BF16), preferred_element_type=F32) + b_ref[0]


def _modulation(cond, w_mod, b_mod):
    tn = 1024
    return pl.pallas_call(
        _mod_kernel,
        grid=(DEPTH, 6 * D_MODEL // tn),
        in_specs=[pl.BlockSpec((8, D_MODEL), lambda l, n: (0, 0)),
                  pl.BlockSpec((1, D_MODEL, tn), lambda l, n: (l, 0, n)),
                  pl.BlockSpec((1, 1, tn), lambda l, n: (l, 0, n))],
        out_specs=pl.BlockSpec((1, 8, tn), lambda l, n: (l, 0, n)),
        out_shape=jax.ShapeDtypeStruct((DEPTH, 8, 6 * D_MODEL), F32),
        compiler_params=_cparams(("parallel", "parallel")),
        name="modulation",
    )(cond, w_mod, b_mod.reshape(DEPTH, 1, 6 * D_MODEL))


def _mod_row(m, tm):
    return jnp.minimum(m // (SEQ // tm), BATCH)


def _inproj_kernel(x_ref, mod_ref, w_ref, cos_ref, sa_ref, sb_ref, o_ref, h_scr, *, rope):
    n = pl.program_id(1)

    @pl.when(n == 0)
    def _():
        sh = mod_ref[0, 0:1, :]
        sc = mod_ref[0, 1:2, :]
        h_scr[...] = (x_ref[...] * (1.0 + sc) + sh).astype(BF16)

    acc = jnp.dot(h_scr[...], w_ref[...], preferred_element_type=F32)
    if not rope:
        o_ref[...] = acc.astype(o_ref.dtype)
        return

    is_qk = (n >= 2) & (n < 6)

    @pl.when(is_qk)
    def _():
        scale = jnp.where(n < 4, QK_SCALE, 1.0).astype(F32)
        cos = cos_ref[...]
        sa = sa_ref[...]
        sb = sb_ref[...]
        for g in range(PROJ_TN // LANE):
            blk = acc[:, g * LANE:(g + 1) * LANE]
            r = blk * cos + pltpu.roll(blk, 32, 1) * sa + pltpu.roll(blk, 96, 1) * sb
            o_ref[:, g * LANE:(g + 1) * LANE] = (r * scale).astype(o_ref.dtype)

    @pl.when(jnp.logical_not(is_qk))
    def _():
        o_ref[...] = acc.astype(o_ref.dtype)


def _inproj(x, mod3, w, tables, out_dtype, rope):
    width = w.shape[1]
    cos, sa, sb = tables
    tab_spec = pl.BlockSpec((PROJ_TM, LANE), lambda m, n: (m, 0))
    return pl.pallas_call(
        functools.partial(_inproj_kernel, rope=rope),
        grid=(T_ROWS // PROJ_TM, width // PROJ_TN),
        in_specs=[pl.BlockSpec((PROJ_TM, D_MODEL), lambda m, n: (m, 0)),
                  pl.BlockSpec((1, 6, D_MODEL), lambda m, n: (_mod_row(m, PROJ_TM), 0, 0)),
                  pl.BlockSpec((D_MODEL, PROJ_TN), lambda m, n: (0, n)),
                  tab_spec, tab_spec, tab_spec],
        out_specs=pl.BlockSpec((PROJ_TM, PROJ_TN), lambda m, n: (m, n)),
        out_shape=jax.ShapeDtypeStruct((T_ROWS, width), out_dtype),
        scratch_shapes=[pltpu.VMEM((PROJ_TM, D_MODEL), BF16)],
        compiler_params=_cparams(("parallel", "arbitrary")),
        name="inproj_rope" if rope else "inproj",
    )(x, mod3, w, cos, sa, sb)


def _gmlp_kernel(u_ref, v_ref, g_ref, b_ref, ws_ref, bs_ref, o_ref):
    for h in range(A_HEADS):
        sl = slice(h * HEAD_DIM, (h + 1) * HEAD_DIM)
        v = _gelu_tanh(v_ref[:, sl].astype(F32))
        mu = jnp.mean(v, axis=-1, keepdims=True)
        var = jnp.mean(jnp.square(v - mu), axis=-1, keepdims=True)
        vn = (v - mu) * lax.rsqrt(var + LN_EPS) * g_ref[:, sl] + b_ref[:, sl]
        s = jnp.dot(ws_ref[h], vn.astype(BF16), preferred_element_type=F32) + bs_ref[:, h:h + 1]
        u = _gelu_tanh(u_ref[:, sl].astype(F32))
        o_ref[:, sl] = (u * s).astype(o_ref.dtype)


def _gmlp(p_lo, ln_g, ln_b, ws, bs_t):
    return pl.pallas_call(
        _gmlp_kernel,
        grid=(T_ROWS // A_CHUNK,),
        in_specs=[pl.BlockSpec((A_CHUNK, A_WIDTH), lambda i: (i, 0)),
                  pl.BlockSpec((A_CHUNK, A_WIDTH), lambda i: (i, 1)),
                  pl.BlockSpec((1, A_WIDTH), lambda i: (0, 0)),
                  pl.BlockSpec((1, A_WIDTH), lambda i: (0, 0)),
                  pl.BlockSpec((A_HEADS, A_CHUNK, A_CHUNK), lambda i: (0, 0, 0)),
                  pl.BlockSpec((A_CHUNK, A_HEADS), lambda i: (0, 0))],
        out_specs=pl.BlockSpec((A_CHUNK, A_WIDTH), lambda i: (i, 0)),
        out_shape=jax.ShapeDtypeStruct((T_ROWS, A_WIDTH), BF16),
        compiler_params=_cparams(("parallel",)),
        name="gmlp",
    )(p_lo, p_lo, ln_g, ln_b, ws, bs_t)


NQ_LAT = SEQ // ATT_TQ


def _attn_kernel(lam_ref, q_ref, kl_ref, vl_ref, kc_ref, vc_ref, g_ref, o_ref, m_scr, l_scr, acc_scr):
    qi = pl.program_id(2)
    m_scr[...] = jnp.full(m_scr.shape, -1e30, F32)
    l_scr[...] = jnp.zeros(l_scr.shape, F32)
    acc_scr[...] = jnp.zeros(acc_scr.shape, F32)

    def process(k, v):
        for mp in range(2):
            sl = slice(mp * HEAD_DIM, (mp + 1) * HEAD_DIM)
            s = _dot_nt(q_ref[:, sl], k[:, sl])
            m_old = m_scr[mp]
            m_new = jnp.maximum(m_old, jnp.max(s, axis=-1, keepdims=True))
            alpha = jnp.exp(m_old - m_new)
            p = jnp.exp(s - m_new)
            l_scr[mp] = alpha * l_scr[mp] + jnp.sum(p, axis=-1, keepdims=True)
            acc_scr[mp] = alpha * acc_scr[mp] + jnp.dot(p.astype(BF16), v, preferred_element_type=F32)
            m_scr[mp] = m_new

    n_lat_chunks = jnp.where(qi < NQ_LAT, SEQ // ATT_TK, 0)

    def body(j, carry):
        rows = pl.ds(pl.multiple_of(j * ATT_TK, ATT_TK), ATT_TK)
        process(kl_ref[rows, :], vl_ref[rows, :])
        return carry

    lax.fori_loop(0, n_lat_chunks, body, 0)
    process(kc_ref[...], vc_ref[...])

    lam = lam_ref[0]
    post = lam_ref[1]
    o = acc_scr[0] / l_scr[0] - lam * (acc_scr[1] / l_scr[1])
    r = o * lax.rsqrt(jnp.mean(jnp.square(o), axis=-1, keepdims=True) + NORM_EPS)
    o_ref[...] = (r * g_ref[...] * post).astype(o_ref.dtype)


def _diff_attention(p_lo, lam2, subln_g):
    def q_rows(b, h, qi):
        return jnp.where(qi < NQ_LAT, b * NQ_LAT + qi, BATCH * NQ_LAT + b)

    return pl.pallas_call(
        _attn_kernel,
        grid=(BATCH, B_HEADS, NQ_LAT + 1),
        in_specs=[pl.BlockSpec(memory_space=pltpu.SMEM),
                  pl.BlockSpec((ATT_TQ, B_V_DIM), lambda b, h, qi: (q_rows(b, h, qi), 4 + h)),
                  pl.BlockSpec((SEQ, B_V_DIM), lambda b, h, qi: (b, 8 + h)),
                  pl.BlockSpec((SEQ, B_V_DIM), lambda b, h, qi: (b, 12 + h)),
                  pl.BlockSpec((CTX_LEN, B_V_DIM), lambda b, h, qi: (N_LAT // CTX_LEN + b, 8 + h)),
                  pl.BlockSpec((CTX_LEN, B_V_DIM), lambda b, h, qi: (N_LAT // CTX_LEN + b, 12 + h)),
                  pl.BlockSpec((1, B_V_DIM), lambda b, h, qi: (0, 0))],
        out_specs=pl.BlockSpec((ATT_TQ, B_V_DIM), lambda b, h, qi: (q_rows(b, h, qi), h)),
        scratch_shapes=[pltpu.VMEM((2, ATT_TQ, 1), F32),
                        pltpu.VMEM((2, ATT_TQ, 1), F32),
                        pltpu.VMEM((2, ATT_TQ, B_V_DIM), F32)],
        out_shape=jax.ShapeDtypeStruct((T_ROWS, B_WIDTH), BF16),
        compiler_params=_cparams(("parallel", "parallel", "arbitrary")),
        name="diff_attention",
    )(lam2, p_lo, p_lo, p_lo, p_lo, p_lo, subln_g)


def _scan_structure():
    c = SCAN_C
    x = np.zeros((2, (2 + SCAN_LEVELS) * c, c), np.float32)
    msk = np.zeros((2, SCAN_LEVELS + 1, c, c), np.float32)
    r = np.arange(c)
    j = np.arange(c)[None, :]
    xf = np.zeros(((2 + SCAN_LEVELS) * c, c), np.float32)
    mf = np.zeros((SCAN_LEVELS + 1, c, c), np.float32)
    xf[0:c] = (j <= r[:, None])
    xf[c:2 * c] = (j > r[:, None])
    for lev in range(SCAN_LEVELS):
        half = (c // 2) >> lev
        start = (r // (2 * half)) * (2 * half)
        mid = start + half - 1
        later = r > mid
        rows = np.where(later[:, None], (j > mid[:, None]) & (j <= r[:, None]),
                        (j > r[:, None]) & (j <= mid[:, None]))
        xf[(2 + lev) * c:(3 + lev) * c] = rows
        same = start[:, None] == start[None, :]
        mf[lev] = same & later[:, None] & (~later)[None, :]
    mf[SCAN_LEVELS] = np.eye(c)
    x[0] = xf
    msk[0] = mf
    x[1] = xf.reshape(2 + SCAN_LEVELS, c, c)[:, ::-1, ::-1].reshape(-1, c)
    msk[1] = mf[:, ::-1, ::-1]
    return x, msk


def _hgrn_kernel(qf_ref, ff_ref, if_ref, qb_ref, fb_ref, ib_ref, lb_ref, x_ref, msk_ref,
                 of_ref, ob_ref, st_scr):
    j = pl.program_id(1)

    @pl.when(j == 0)
    def _():
        st_scr[...] = jnp.zeros(st_scr.shape, F32)

    c = SCAN_C
    dirs = ((qf_ref, ff_ref, if_ref, of_ref, c - 1), (qb_ref, fb_ref, ib_ref, ob_ref, 0))
    for d, (q_ref, f_ref, i_ref, o_ref, end_row) in enumerate(dirs):
        for h in range(C_HEADS):
            sl = slice(h * HEAD_DIM, (h + 1) * HEAD_DIM)
            qraw = q_ref[:, sl]
            q = qraw * _sigmoid(qraw)
            lb = lb_ref[d:d + 1, sl]
            f = lb + (1.0 - lb) * _sigmoid(f_ref[:, sl])
            logf = jnp.log(f)
            k = 1.0 - f
            v = i_ref[:, sl].astype(BF16)
            hi = logf.astype(BF16)
            r1 = logf - hi.astype(F32)
            mid = r1.astype(BF16)
            lo = (r1 - mid.astype(F32)).astype(BF16)
            e3 = jnp.dot(x_ref[d], jnp.concatenate([hi, mid, lo], axis=1), preferred_element_type=F32)
            w = jnp.exp(e3[:, 0:c] + e3[:, c:2 * c] + e3[:, 2 * c:3 * c])
            a = msk_ref[d, SCAN_LEVELS] * _dot_nt(q.astype(BF16), k.astype(BF16))
            for lev in range(SCAN_LEVELS):
                wl = w[(2 + lev) * c:(3 + lev) * c]
                a = a + msk_ref[d, lev] * _dot_nt((q * wl).astype(BF16), (k * wl).astype(BF16))
            st = st_scr[d, h]
            o = jnp.dot(a.astype(BF16), v, preferred_element_type=F32)
            o = o + _dot_nt((q * w[0:c]).astype(BF16), st.astype(BF16))
            o_ref[:, sl] = o
            g_end = w[end_row:end_row + 1, :]
            st_scr[d, h] = st * g_end + _dot_tn(v, (k * w[c:2 * c]).astype(BF16))


def _hgrn_scan(p_hi, lb2, xmat, masks):
    lat_chunks = SEQ // SCAN_C
    ctx_chunks = CTX_LEN // SCAN_C
    steps = ctx_chunks + lat_chunks

    def fwd_rows(b, j):
        return jnp.where(j < ctx_chunks, N_LAT // SCAN_C + b * ctx_chunks + j,
                         b * lat_chunks + j - ctx_chunks)

    def bwd_rows(b, j):
        return jnp.where(j < ctx_chunks, N_LAT // SCAN_C + b * ctx_chunks + (ctx_chunks - 1 - j),
                         b * lat_chunks + (steps - 1 - j))

    def spec(rows, col):
        return pl.BlockSpec((SCAN_C, C_WIDTH), lambda b, j: (rows(b, j), col))

    nx = (2 + SCAN_LEVELS) * SCAN_C
    return pl.pallas_call(
        _hgrn_kernel,
        grid=(BATCH, steps),
        in_specs=[spec(fwd_rows, 0), spec(fwd_rows, 1), spec(fwd_rows, 3),
                  spec(bwd_rows, 0), spec(bwd_rows, 2), spec(bwd_rows, 3),
                  pl.BlockSpec((2, C_WIDTH), lambda b, j: (0, 0)),
                  pl.BlockSpec((2, nx, SCAN_C), lambda b, j: (0, 0, 0)),
                  pl.BlockSpec((2, SCAN_LEVELS + 1, SCAN_C, SCAN_C), lambda b, j: (0, 0, 0, 0))],
        out_specs=[spec(fwd_rows, 0), spec(bwd_rows, 0)],
        out_shape=[jax.ShapeDtypeStruct((T_ROWS, C_WIDTH), F32)] * 2,
        scratch_shapes=[pltpu.VMEM((2, C_HEADS, HEAD_DIM, HEAD_DIM), F32)],
        compiler_params=_cparams(("parallel", "arbitrary")),
        name="hgrn2_scan",
    )(p_hi, p_hi, p_hi, p_hi, p_hi, p_hi, lb2, xmat, masks)


def _layer_norm_rows(z, g, b):
    mu = jnp.mean(z, axis=-1, keepdims=True)
    var = jnp.mean(jnp.square(z - mu), axis=-1, keepdims=True)
    return (z - mu) * lax.rsqrt(var + LN_EPS) * g + b


def _outproj_kernel(a_ref, b_ref, of_ref, ob_ref, cg_ref, hg_ref, x_ref, mod_ref, w_ref,
                    l1g_ref, l1b_ref, rw_ref, rb_ref,
                    x1_ref, hx_ref, idx_ref, wt_ref):
    parts = []
    for h in range(C_HEADS):
        sl = slice(h * HEAD_DIM, (h + 1) * HEAD_DIM)
        o = of_ref[:, sl] + ob_ref[:, sl]
        on = o * lax.rsqrt(jnp.mean(jnp.square(o), axis=-1, keepdims=True) + NORM_EPS) * hg_ref[...]
        g = cg_ref[:, sl]
        parts.append((on * (g * _sigmoid(g))).astype(BF16))
    c_x = jnp.concatenate(parts, axis=1)
    mix = jnp.dot(a_ref[...], w_ref[0:A_WIDTH, :], preferred_element_type=F32)
    mix = mix + jnp.dot(b_ref[...], w_ref[A_WIDTH:A_WIDTH + B_WIDTH, :], preferred_element_type=F32)
    mix = mix + jnp.dot(c_x, w_ref[A_WIDTH + B_WIDTH:, :], preferred_element_type=F32)
    g1 = mod_ref[0, 2:3, :]
    sh2 = mod_ref[0, 3:4, :]
    sc2 = mod_ref[0, 4:5, :]
    x1 = _layer_norm_rows(DEEPNORM_ALPHA * x_ref[...] + g1 * mix, l1g_ref[...], l1b_ref[...])
    x1_ref[...] = x1
    hx = x1 * (1.0 + sc2) + sh2
    hx_ref[...] = hx.astype(BF16)

    logits = jnp.dot(hx, rw_ref[...], preferred_element_type=F32, precision=lax.Precision.HIGHEST)
    scores = _sigmoid(logits)
    sel = scores + rb_ref[...]
    lane = lax.broadcasted_iota(jnp.int32, sel.shape, 1).astype(F32)
    slot = lax.broadcasted_iota(jnp.int32, (sel.shape[0], LANE), 1)
    idx_acc = jnp.zeros((sel.shape[0], LANE), F32)
    wt_acc = jnp.zeros((sel.shape[0], LANE), F32)
    for k in range(TOP_K):
        mx = jnp.max(sel, axis=-1, keepdims=True)
        idx = jnp.min(jnp.where(sel == mx, lane, float(N_EXPERTS)), axis=-1, keepdims=True)
        hit = lane == idx
        w_k = jnp.sum(jnp.where(hit, scores, 0.0), axis=-1, keepdims=True)
        idx_acc = jnp.where(slot == k, idx, idx_acc)
        wt_acc = jnp.where(slot == k, w_k, wt_acc)
        sel = jnp.where(hit, -jnp.inf, sel)
    idx_ref[...] = idx_acc.astype(jnp.int32)
    wt_ref[...] = wt_acc / jnp.sum(wt_acc, axis=-1, keepdims=True) * ROUTED_SCALE


def _outproj(a_x, b_x, o_f, o_b, p_hi, hnorm_g, x, mod3, w_out, ln_g, ln_b, router_w, router_b):
    row = lambda width: pl.BlockSpec((OUT_TM, width), lambda i: (i, 0))
    const = lambda shape: pl.BlockSpec(shape, lambda i: tuple(0 for _ in shape))
    return pl.pallas_call(
        _outproj_kernel,
        grid=(T_ROWS // OUT_TM,),
        in_specs=[row(A_WIDTH), row(B_WIDTH), row(C_WIDTH), row(C_WIDTH),
                  pl.BlockSpec((OUT_TM, C_WIDTH), lambda i: (i, 4)),
                  const((1, HEAD_DIM)),
                  row(D_MODEL),
                  pl.BlockSpec((1, 6, D_MODEL), lambda i: (_mod_row(i, OUT_TM), 0, 0)),
                  const((D_MODEL, D_MODEL)),
                  const((1, D_MODEL)), const((1, D_MODEL)),
                  const((D_MODEL, N_EXPERTS)), const((1, N_EXPERTS))],
        out_specs=[row(D_MODEL), row(D_MODEL), row(LANE), row(LANE)],
        out_shape=[jax.ShapeDtypeStruct((T_ROWS, D_MODEL), F32),
                   jax.ShapeDtypeStruct((T_ROWS, D_MODEL), BF16),
                   jax.ShapeDtypeStruct((T_ROWS, LANE), jnp.int32),
                   jax.ShapeDtypeStruct((T_ROWS, LANE), F32)],
        compiler_params=_cparams(("parallel",)),
        name="outproj_ln_router",
    )(a_x, b_x, o_f, o_b, p_hi, hnorm_g, x, mod3, w_out, ln_g, ln_b, router_w, router_b)


def _experts_kernel(te_ref, nt_ref, xs_ref, wgu_ref, wd_ref, ys_ref):
    i = pl.program_id(0)

    @pl.when(i < nt_ref[0])
    def _():
        h = jnp.dot(xs_ref[...], wgu_ref[0], preferred_element_type=F32)
        g = h[:, :EXPERT_DIM]
        u = h[:, EXPERT_DIM:]
        a = (g * _sigmoid(g) * u).astype(BF16)
        ys_ref[...] = jnp.dot(a, wd_ref[0], preferred_element_type=F32).astype(ys_ref.dtype)


def _experts(tile_expert, n_tiles, xs, w_gu, w_d):
    tiles = xs.shape[0] // MOE_TM
    last = lambda i, nt: jnp.minimum(i, nt[0] - 1)
    grid_spec = pltpu.PrefetchScalarGridSpec(
        num_scalar_prefetch=2,
        grid=(tiles,),
        in_specs=[pl.BlockSpec((MOE_TM, D_MODEL), lambda i, te, nt: (last(i, nt), 0)),
                  pl.BlockSpec((1, D_MODEL, 2 * EXPERT_DIM), lambda i, te, nt: (te[i], 0, 0)),
                  pl.BlockSpec((1, EXPERT_DIM, D_MODEL), lambda i, te, nt: (te[i], 0, 0))],
        out_specs=pl.BlockSpec((MOE_TM, D_MODEL), lambda i, te, nt: (last(i, nt), 0)),
    )
    return pl.pallas_call(
        _experts_kernel,
        grid_spec=grid_spec,
        out_shape=jax.ShapeDtypeStruct((xs.shape[0], D_MODEL), F32),
        compiler_params=_cparams(("arbitrary",)),
        name="experts",
    )(tile_expert, n_tiles, xs, w_gu, w_d)


def _combine_kernel(yg_ref, wt_ref, ysh_ref, x1_ref, mod_ref, g_ref, b_ref, o_ref):
    y = ysh_ref[...]
    for k in range(TOP_K):
        y = y + wt_ref[:, k:k + 1] * yg_ref[k]
    g2 = mod_ref[0, 5:6, :]
    o_ref[...] = _layer_norm_rows(DEEPNORM_ALPHA * x1_ref[...] + g2 * y, g_ref[...], b_ref[...])


def _combine(yg, wt, ysh, x1, mod3, ln_g, ln_b):
    tm = COMBINE_TM
    row = lambda width: pl.BlockSpec((tm, width), lambda i: (i, 0))
    return pl.pallas_call(
        _combine_kernel,
        grid=(T_ROWS // tm,),
        in_specs=[pl.BlockSpec((TOP_K, tm, D_MODEL), lambda i: (0, i, 0)),
                  row(LANE), row(D_MODEL), row(D_MODEL),
                  pl.BlockSpec((1, 6, D_MODEL), lambda i: (_mod_row(i, tm), 0, 0)),
                  pl.BlockSpec((1, D_MODEL), lambda i: (0, 0)),
                  pl.BlockSpec((1, D_MODEL), lambda i: (0, 0))],
        out_specs=row(D_MODEL),
        out_shape=jax.ShapeDtypeStruct((T_ROWS, D_MODEL), F32),
        compiler_params=_cparams(("parallel",)),
        name="moe_combine_ln",
    )(yg, wt, ysh, x1, mod3, ln_g, ln_b)


def _dispatch(idx):
    onehot = (idx[:, :, None] == jnp.arange(N_EXPERTS, dtype=jnp.int32)[None, None, :])
    sel = jnp.sum(onehot.astype(jnp.int32), axis=1)
    rank = jnp.cumsum(sel, axis=0) - sel
    counts = jnp.sum(sel, axis=0)
    tiles_per = (counts + MOE_TM - 1) // MOE_TM
    tile_end = jnp.cumsum(tiles_per)
    start = (tile_end - tiles_per) * MOE_TM
    pos = jnp.take_along_axis(start[None, :] + rank, idx, axis=1)
    tok = jnp.broadcast_to(jnp.arange(T_ROWS, dtype=jnp.int32)[:, None], idx.shape)
    tok_of_row = jnp.zeros((MOE_ROWS,), jnp.int32).at[pos.reshape(-1)].set(tok.reshape(-1))
    n_tiles = tile_end[-1:].astype(jnp.int32)
    tile_expert = jnp.searchsorted(tile_end, jnp.arange(MOE_TILES, dtype=jnp.int32), side="right")
    tile_expert = jnp.minimum(tile_expert, N_EXPERTS - 1).astype(jnp.int32)
    return tok_of_row, pos, tile_expert, n_tiles


def _rope_tables():
    t = jnp.arange(SEQ)
    row = (t // GRID_W).astype(F32)
    col = (t % GRID_W).astype(F32)
    axis_dim = HEAD_DIM // 2
    inv = ROPE_THETA ** (-jnp.arange(0, axis_dim, 2, dtype=F32) / axis_dim)
    ar = row[:, None] * inv
    ac = col[:, None] * inv
    cos = jnp.concatenate([jnp.cos(ar), jnp.cos(ar), jnp.cos(ac), jnp.cos(ac)], axis=-1)
    sin = jnp.concatenate([jnp.sin(ar), jnp.sin(ar), jnp.sin(ac), jnp.sin(ac)], axis=-1)
    odd = ((jnp.arange(HEAD_DIM) // 32) % 2 == 1)[None, :]
    sa = jnp.where(odd, sin, 0.0)
    sb = jnp.where(odd, 0.0, -sin)
    lat = lambda z: jnp.tile(z, (BATCH, 1))
    cos_all = jnp.concatenate([lat(cos), jnp.ones((N_CTX, HEAD_DIM), F32)], axis=0)
    sa_all = jnp.concatenate([lat(sa), jnp.zeros((N_CTX, HEAD_DIM), F32)], axis=0)
    sb_all = jnp.concatenate([lat(sb), jnp.zeros((N_CTX, HEAD_DIM), F32)], axis=0)
    return cos_all, sa_all, sb_all


def kernel(x, c, ctx, c_ctx, w_mod, b_mod, w_in, w_out, gmlp_ln_g, gmlp_ln_b, gmlp_ws, gmlp_bs,
           diff_lam, diff_subln_g, hgrn_lb, hgrn_norm_g, ln1_g, ln1_b, ln2_g, ln2_b,
           router_w, router_b, exp_w_gate, exp_w_up, exp_w_down, sh_w_gate, sh_w_up, sh_w_down):
    assert x.shape == (BATCH, SEQ, D_MODEL) and ctx.shape == (BATCH, CTX_LEN, D_MODEL)
    xs_all = jnp.concatenate([x.reshape(N_LAT, D_MODEL), ctx.reshape(N_CTX, D_MODEL)], axis=0)

    cond = jnp.zeros((8, D_MODEL), F32).at[0:BATCH].set(c).at[BATCH].set(c_ctx)
    mod_all = _modulation(cond, w_mod, b_mod)[:, :BATCH + 1].reshape(DEPTH, BATCH + 1, 6, D_MODEL)

    sm = jax.nn.softmax(hgrn_lb.astype(F32), axis=0)
    lb_all = jnp.cumsum(sm, axis=0) - sm[0]
    tables = _rope_tables()
    xmat_np, masks_np = _scan_structure()
    xmat = jnp.asarray(xmat_np, BF16)
    masks = jnp.asarray(masks_np, F32)

    for l in range(DEPTH):
        mod3 = mod_all[l]
        w_in_l = w_in[l].astype(BF16)
        p_lo = _inproj(xs_all, mod3, w_in_l[:, :LOW_WIDTH], tables, BF16, True)
        p_hi = _inproj(xs_all, mod3, w_in_l[:, LOW_WIDTH:], tables, F32, False)

        a_x = _gmlp(p_lo, gmlp_ln_g[l][None, :], gmlp_ln_b[l][None, :],
                    gmlp_ws[l].astype(BF16), gmlp_bs[l].T)

        lam_init = 0.8 - 0.6 * math.exp(-0.3 * l)
        dl = diff_lam[l].astype(F32)
        lam = jnp.exp(jnp.sum(dl[0] * dl[1])) - jnp.exp(jnp.sum(dl[2] * dl[3])) + lam_init
        lam2 = jnp.stack([lam, jnp.asarray(1.0 - lam_init, F32)]).astype(F32)
        b_x = _diff_attention(p_lo, lam2, diff_subln_g[l][None, :])

        o_f, o_b = _hgrn_scan(p_hi, lb_all[l], xmat, masks)

        x1, hx, idx, wt = _outproj(a_x, b_x, o_f, o_b, p_hi, hgrn_norm_g[l][None, :], xs_all, mod3,
                                   w_out[l].astype(BF16), ln1_g[l][None, :], ln1_b[l][None, :],
                                   router_w[l], router_b[l][None, :])

        idx = idx[:, :TOP_K]
        tok_of_row, pos, tile_expert, n_tiles = _dispatch(idx)
        xs_grouped = jnp.take(hx, tok_of_row, axis=0)
        w_gu = jnp.concatenate([exp_w_gate[l], exp_w_up[l]], axis=-1).astype(BF16)
        ys = _experts(tile_expert, n_tiles, xs_grouped, w_gu, exp_w_down[l].astype(BF16))
        yg = jnp.take(ys, pos.T.reshape(-1), axis=0).reshape(TOP_K, T_ROWS, D_MODEL)

        s_gu = jnp.concatenate([sh_w_gate[l], sh_w_up[l]], axis=-1).astype(BF16)[None]
        ysh = _experts(jnp.zeros((T_ROWS // MOE_TM,), jnp.int32),
                       jnp.full((1,), T_ROWS // MOE_TM, jnp.int32),
                       hx, s_gu, sh_w_down[l].astype(BF16)[None])

        xs_all = _combine(yg, wt, ysh, x1, mod3, ln2_g[l][None, :], ln2_b[l][None, :])

    return xs_all[:N_LAT].reshape(BATCH, SEQ, D_MODEL)
```

```python
import functools
import math

import numpy as np
import jax
import jax.numpy as jnp
from jax import lax
from jax.experimental import pallas as pl
from jax.experimental.pallas import tpu as pltpu

F32 = jnp.float32
BF16 = jnp.bfloat16

D_MODEL = 2048
BATCH = 2
SEQ = 4096
DEPTH = 4
CTX_LEN = 256
GRID_W = 64
HEAD_DIM = 128
A_WIDTH = 512
A_HEADS = 4
A_CHUNK = 128
B_WIDTH = 1024
B_HEADS = 4
B_QK_WIDTH = 1024
B_V_DIM = 256
ROPE_THETA = 10000.0
C_WIDTH = 512
C_HEADS = 4
IN_WIDTH = 6656
N_EXPERTS = 64
TOP_K = 8
EXPERT_DIM = 384
ROUTED_SCALE = 2.5
DEEPNORM_ALPHA = (2.0 * DEPTH) ** 0.25
LN_EPS = 1e-5
NORM_EPS = 1e-5

N_LAT = BATCH * SEQ
N_CTX = BATCH * CTX_LEN
T_ROWS = N_LAT + N_CTX
LOW_WIDTH = 4096
HI_WIDTH = IN_WIDTH - LOW_WIDTH
QK_SCALE = HEAD_DIM ** -0.5 * math.log2(math.e)

LANE = 128
PROJ_TM = 512
PROJ_TN = 512
ATT_TQ = 256
ATT_TK = 1024
SCAN_C = 128
SCAN_LEVELS = 7
OUT_TM = 256
MOE_TM = 256
COMBINE_TM = 128
MOE_ROWS = ((T_ROWS * TOP_K + N_EXPERTS * (MOE_TM - 1)) // MOE_TM) * MOE_TM
MOE_TILES = MOE_ROWS // MOE_TM
VMEM_LIMIT = 48 * 1024 * 1024


def _cparams(sem, vmem=VMEM_LIMIT):
    return pltpu.CompilerParams(dimension_semantics=sem, vmem_limit_bytes=vmem)


def _sigmoid(x):
    return 1.0 / (1.0 + jnp.exp(-x))


def _gelu_tanh(x):
    return 0.5 * x * (1.0 + jnp.tanh(math.sqrt(2.0 / math.pi) * (x + 0.044715 * (x * x * x))))


def _dot_nt(a, b):
    return lax.dot_general(a, b, (((1,), (1,)), ((), ())), preferred_element_type=F32)


def _dot_tn(a, b):
    return lax.dot_general(a, b, (((0,), (0,)), ((), ())), preferred_element_type=F32)


def _mod_kernel(c_ref, w_ref, b_ref, o_ref):
    c = c_ref[...]
    s = (c * _sigmoid(c)).astype(BF16)
    o_ref[0] = jnp.dot(s, w_ref[0].astype(BF16), preferred_element_type=F32) + b_ref[0]


def _modulation(cond, w_mod, b_mod):
    tn = 1024
    return pl.pallas_call(
        _mod_kernel,
        grid=(DEPTH, 6 * D_MODEL // tn),
        in_specs=[pl.BlockSpec((8, D_MODEL), lambda l, n: (0, 0)),
                  pl.BlockSpec((1, D_MODEL, tn), lambda l, n: (l, 0, n)),
                  pl.BlockSpec((1, 1, tn), lambda l, n: (l, 0, n))],
        out_specs=pl.BlockSpec((1, 8, tn), lambda l, n: (l, 0, n)),
        out_shape=jax.ShapeDtypeStruct((DEPTH, 8, 6 * D_MODEL), F32),
        compiler_params=_cparams(("parallel", "parallel")),
        name="modulation",
    )(cond, w_mod, b_mod.reshape(DEPTH, 1, 6 * D_MODEL))


def _mod_row(m, tm):
    return jnp.minimum(m // (SEQ // tm), BATCH)


def _inproj_kernel(x_ref, mod_ref, w_ref, cos_ref, sa_ref, sb_ref, o_ref, h_scr, *, rope):
    n = pl.program_id(1)

    @pl.when(n == 0)
    def _():
        sh = mod_ref[0, 0:1, :]
        sc = mod_ref[0, 1:2, :]
        h_scr[...] = (x_ref[...] * (1.0 + sc) + sh).astype(BF16)

    acc = jnp.dot(h_scr[...], w_ref[...], preferred_element_type=F32)
    if not rope:
        o_ref[...] = acc.astype(o_ref.dtype)
        return

    is_qk = (n >= 2) & (n < 6)

    @pl.when(is_qk)
    def _():
        scale = jnp.where(n < 4, QK_SCALE, 1.0).astype(F32)
        cos = cos_ref[...]
        sa = sa_ref[...]
        sb = sb_ref[...]
        for g in range(PROJ_TN // LANE):
            blk = acc[:, g * LANE:(g + 1) * LANE]
            r = blk * cos + pltpu.roll(blk, 32, 1) * sa + pltpu.roll(blk, 96, 1) * sb
            o_ref[:, g * LANE:(g + 1) * LANE] = (r * scale).astype(o_ref.dtype)

    @pl.when(jnp.logical_not(is_qk))
    def _():
        o_ref[...] = acc.astype(o_ref.dtype)


def _inproj(x, mod3, w, tables, out_dtype, rope):
    width = w.shape[1]
    cos, sa, sb = tables
    tab_spec = pl.BlockSpec((PROJ_TM, LANE), lambda m, n: (m, 0))
    return pl.pallas_call(
        functools.partial(_inproj_kernel, rope=rope),
        grid=(T_ROWS // PROJ_TM, width // PROJ_TN),
        in_specs=[pl.BlockSpec((PROJ_TM, D_MODEL), lambda m, n: (m, 0)),
                  pl.BlockSpec((1, 6, D_MODEL), lambda m, n: (_mod_row(m, PROJ_TM), 0, 0)),
                  pl.BlockSpec((D_MODEL, PROJ_TN), lambda m, n: (0, n)),
                  tab_spec, tab_spec, tab_spec],
        out_specs=pl.BlockSpec((PROJ_TM, PROJ_TN), lambda m, n: (m, n)),
        out_shape=jax.ShapeDtypeStruct((T_ROWS, width), out_dtype),
        scratch_shapes=[pltpu.VMEM((PROJ_TM, D_MODEL), BF16)],
        compiler_params=_cparams(("parallel", "arbitrary")),
        name="inproj_rope" if rope else "inproj",
    )(x, mod3, w, cos, sa, sb)


def _gmlp_kernel(u_ref, v_ref, g_ref, b_ref, ws_ref, bs_ref, o_ref):
    for h in range(A_HEADS):
        sl = slice(h * HEAD_DIM, (h + 1) * HEAD_DIM)
        v = _gelu_tanh(v_ref[:, sl].astype(F32))
        mu = jnp.mean(v, axis=-1, keepdims=True)
        var = jnp.mean(jnp.square(v - mu), axis=-1, keepdims=True)
        vn = (v - mu) * lax.rsqrt(var + LN_EPS) * g_ref[:, sl] + b_ref[:, sl]
        s = jnp.dot(ws_ref[h], vn.astype(BF16), preferred_element_type=F32) + bs_ref[:, h:h + 1]
        u = _gelu_tanh(u_ref[:, sl].astype(F32))
        o_ref[:, sl] = (u * s).astype(o_ref.dtype)


def _gmlp(p_lo, ln_g, ln_b, ws, bs_t):
    return pl.pallas_call(
        _gmlp_kernel,
        grid=(T_ROWS // A_CHUNK,),
        in_specs=[pl.BlockSpec((A_CHUNK, A_WIDTH), lambda i: (i, 0)),
                  pl.BlockSpec((A_CHUNK, A_WIDTH), lambda i: (i, 1)),
                  pl.BlockSpec((1, A_WIDTH), lambda i: (0, 0)),
                  pl.BlockSpec((1, A_WIDTH), lambda i: (0, 0)),
                  pl.BlockSpec((A_HEADS, A_CHUNK, A_CHUNK), lambda i: (0, 0, 0)),
                  pl.BlockSpec((A_CHUNK, A_HEADS), lambda i: (0, 0))],
        out_specs=pl.BlockSpec((A_CHUNK, A_WIDTH), lambda i: (i, 0)),
        out_shape=jax.ShapeDtypeStruct((T_ROWS, A_WIDTH), BF16),
        compiler_params=_cparams(("parallel",)),
        name="gmlp",
    )(p_lo, p_lo, ln_g, ln_b, ws, bs_t)


NQ_LAT = SEQ // ATT_TQ


def _attn_kernel(lam_ref, q_ref, kl_ref, vl_ref, kc_ref, vc_ref, g_ref, o_ref, m_scr, l_scr, acc_scr):
    qi = pl.program_id(2)
    m_scr[...] = jnp.full(m_scr.shape, -1e30, F32)
    l_scr[...] = jnp.zeros(l_scr.shape, F32)
    acc_scr[...] = jnp.zeros(acc_scr.shape, F32)

    def process(k, v):
        for mp in range(2):
            sl = slice(mp * HEAD_DIM, (mp + 1) * HEAD_DIM)
            s = _dot_nt(q_ref[:, sl], k[:, sl])
            cols = [s[:, c * LANE:(c + 1) * LANE] for c in range(k.shape[0] // LANE)]
            m_part = cols[0]
            for col in cols[1:]:
                m_part = jnp.maximum(m_part, col)
            m_old = m_scr[mp]
            m_new = jnp.maximum(m_old, jnp.max(m_part, axis=-1, keepdims=True))
            alpha = jnp.exp2(m_old - m_new)
            l_part = alpha * l_scr[mp]
            ps = []
            for col in cols:
                p = jnp.exp2(col - m_new)
                l_part = l_part + p
                ps.append(p.astype(BF16))
            pv = jnp.dot(jnp.concatenate(ps, axis=1), v, preferred_element_type=F32)
            for half in range(B_V_DIM // LANE):
                hs = slice(half * LANE, (half + 1) * LANE)
                acc_scr[mp, :, hs] = alpha * acc_scr[mp, :, hs] + pv[:, hs]
            l_scr[mp] = l_part
            m_scr[mp] = m_new

    n_lat_chunks = jnp.where(qi < NQ_LAT, SEQ // ATT_TK, 0)

    def body(j, carry):
        rows = pl.ds(pl.multiple_of(j * ATT_TK, ATT_TK), ATT_TK)
        process(kl_ref[rows, :], vl_ref[rows, :])
        return carry

    lax.fori_loop(0, n_lat_chunks, body, 0)
    process(kc_ref[...], vc_ref[...])

    lam = lam_ref[0]
    post = lam_ref[1]
    l0 = jnp.sum(l_scr[0], axis=-1, keepdims=True)
    l1 = jnp.sum(l_scr[1], axis=-1, keepdims=True)
    o = acc_scr[0] / l0 - lam * (acc_scr[1] / l1)
    r = o * lax.rsqrt(jnp.mean(jnp.square(o), axis=-1, keepdims=True) + NORM_EPS)
    o_ref[...] = (r * g_ref[...] * post).astype(o_ref.dtype)


def _diff_attention(p_lo, lam2, subln_g):
    def q_rows(b, h, qi):
        return jnp.where(qi < NQ_LAT, b * NQ_LAT + qi, BATCH * NQ_LAT + b)

    return pl.pallas_call(
        _attn_kernel,
        grid=(BATCH, B_HEADS, NQ_LAT + 1),
        in_specs=[pl.BlockSpec(memory_space=pltpu.SMEM),
                  pl.BlockSpec((ATT_TQ, B_V_DIM), lambda b, h, qi: (q_rows(b, h, qi), 4 + h)),
                  pl.BlockSpec((SEQ, B_V_DIM), lambda b, h, qi: (b, 8 + h)),
                  pl.BlockSpec((SEQ, B_V_DIM), lambda b, h, qi: (b, 12 + h)),
                  pl.BlockSpec((CTX_LEN, B_V_DIM), lambda b, h, qi: (N_LAT // CTX_LEN + b, 8 + h)),
                  pl.BlockSpec((CTX_LEN, B_V_DIM), lambda b, h, qi: (N_LAT // CTX_LEN + b, 12 + h)),
                  pl.BlockSpec((1, B_V_DIM), lambda b, h, qi: (0, 0))],
        out_specs=pl.BlockSpec((ATT_TQ, B_V_DIM), lambda b, h, qi: (q_rows(b, h, qi), h)),
        scratch_shapes=[pltpu.VMEM((2, ATT_TQ, LANE), F32),
                        pltpu.VMEM((2, ATT_TQ, LANE), F32),
                        pltpu.VMEM((2, ATT_TQ, B_V_DIM), F32)],
        out_shape=jax.ShapeDtypeStruct((T_ROWS, B_WIDTH), BF16),
        compiler_params=_cparams(("parallel", "parallel", "arbitrary")),
        name="diff_attention",
    )(lam2, p_lo, p_lo, p_lo, p_lo, p_lo, subln_g)


def _scan_structure():
    c = SCAN_C
    x = np.zeros((2, (2 + SCAN_LEVELS) * c, c), np.float32)
    msk = np.zeros((2, SCAN_LEVELS + 1, c, c), np.float32)
    r = np.arange(c)
    j = np.arange(c)[None, :]
    xf = np.zeros(((2 + SCAN_LEVELS) * c, c), np.float32)
    mf = np.zeros((SCAN_LEVELS + 1, c, c), np.float32)
    xf[0:c] = (j <= r[:, None])
    xf[c:2 * c] = (j > r[:, None])
    for lev in range(SCAN_LEVELS):
        half = (c // 2) >> lev
        start = (r // (2 * half)) * (2 * half)
        mid = start + half - 1
        later = r > mid
        rows = np.where(later[:, None], (j > mid[:, None]) & (j <= r[:, None]),
                        (j > r[:, None]) & (j <= mid[:, None]))
        xf[(2 + lev) * c:(3 + lev) * c] = rows
        same = start[:, None] == start[None, :]
        mf[lev] = same & later[:, None] & (~later)[None, :]
    mf[SCAN_LEVELS] = np.eye(c)
    x[0] = xf
    msk[0] = mf
    x[1] = xf.reshape(2 + SCAN_LEVELS, c, c)[:, ::-1, ::-1].reshape(-1, c)
    msk[1] = mf[:, ::-1, ::-1]
    return x, msk


def _hgrn_kernel(qf_ref, ff_ref, if_ref, qb_ref, fb_ref, ib_ref, lb_ref, x_ref, msk_ref,
                 of_ref, ob_ref, st_scr):
    j = pl.program_id(1)

    @pl.when(j == 0)
    def _():
        st_scr[...] = jnp.zeros(st_scr.shape, F32)

    c = SCAN_C
    dirs = ((qf_ref, ff_ref, if_ref, of_ref, c - 1), (qb_ref, fb_ref, ib_ref, ob_ref, 0))
    for d, (q_ref, f_ref, i_ref, o_ref, end_row) in enumerate(dirs):
        for h in range(C_HEADS):
            sl = slice(h * HEAD_DIM, (h + 1) * HEAD_DIM)
            qraw = q_ref[:, sl]
            q = qraw * _sigmoid(qraw)
            lb = lb_ref[d:d + 1, sl]
            f = lb + (1.0 - lb) * _sigmoid(f_ref[:, sl])
            logf = jnp.log(f)
            k = 1.0 - f
            v = i_ref[:, sl].astype(BF16)
            hi = logf.astype(BF16)
            r1 = logf - hi.astype(F32)
            mid = r1.astype(BF16)
            lo = (r1 - mid.astype(F32)).astype(BF16)
            e3 = jnp.dot(x_ref[d], jnp.concatenate([hi, mid, lo], axis=1), preferred_element_type=F32)
            w = jnp.exp(e3[:, 0:c] + e3[:, c:2 * c] + e3[:, 2 * c:3 * c])
            a = msk_ref[d, SCAN_LEVELS] * _dot_nt(q.astype(BF16), k.astype(BF16))
            for lev in range(SCAN_LEVELS):
                wl = w[(2 + lev) * c:(3 + lev) * c]
                a = a + msk_ref[d, lev] * _dot_nt((q * wl).astype(BF16), (k * wl).astype(BF16))
            st = st_scr[d, h]
            o = jnp.dot(a.astype(BF16), v, preferred_element_type=F32)
            o = o + _dot_nt((q * w[0:c]).astype(BF16), st.astype(BF16))
            o_ref[:, sl] = o
            g_end = w[end_row:end_row + 1, :]
            st_scr[d, h] = st * g_end + _dot_tn(v, (k * w[c:2 * c]).astype(BF16))


def _hgrn_scan(p_hi, lb2, xmat, masks):
    lat_chunks = SEQ // SCAN_C
    ctx_chunks = CTX_LEN // SCAN_C
    steps = ctx_chunks + lat_chunks

    def fwd_rows(b, j):
        return jnp.where(j < ctx_chunks, N_LAT // SCAN_C + b * ctx_chunks + j,
                         b * lat_chunks + j - ctx_chunks)

    def bwd_rows(b, j):
        return jnp.where(j < ctx_chunks, N_LAT // SCAN_C + b * ctx_chunks + (ctx_chunks - 1 - j),
                         b * lat_chunks + (steps - 1 - j))

    def spec(rows, col):
        return pl.BlockSpec((SCAN_C, C_WIDTH), lambda b, j: (rows(b, j), col))

    nx = (2 + SCAN_LEVELS) * SCAN_C
    return pl.pallas_call(
        _hgrn_kernel,
        grid=(BATCH, steps),
        in_specs=[spec(fwd_rows, 0), spec(fwd_rows, 1), spec(fwd_rows, 3),
                  spec(bwd_rows, 0), spec(bwd_rows, 2), spec(bwd_rows, 3),
                  pl.BlockSpec((2, C_WIDTH), lambda b, j: (0, 0)),
                  pl.BlockSpec((2, nx, SCAN_C), lambda b, j: (0, 0, 0)),
                  pl.BlockSpec((2, SCAN_LEVELS + 1, SCAN_C, SCAN_C), lambda b, j: (0, 0, 0, 0))],
        out_specs=[spec(fwd_rows, 0), spec(bwd_rows, 0)],
        out_shape=[jax.ShapeDtypeStruct((T_ROWS, C_WIDTH), F32)] * 2,
        scratch_shapes=[pltpu.VMEM((2, C_HEADS, HEAD_DIM, HEAD_DIM), F32)],
        compiler_params=_cparams(("parallel", "arbitrary")),
        name="hgrn2_scan",
    )(p_hi, p_hi, p_hi, p_hi, p_hi, p_hi, lb2, xmat, masks)


def _layer_norm_rows(z, g, b):
    mu = jnp.mean(z, axis=-1, keepdims=True)
    var = jnp.mean(jnp.square(z - mu), axis=-1, keepdims=True)
    return (z - mu) * lax.rsqrt(var + LN_EPS) * g + b


def _outproj_kernel(a_ref, b_ref, of_ref, ob_ref, cg_ref, hg_ref, x_ref, mod_ref, w_ref,
                    l1g_ref, l1b_ref, rw_ref, rb_ref,
                    x1_ref, hx_ref, idx_ref, wt_ref, rank_ref, cnt_ref, cnt_scr):
    parts = []
    for h in range(C_HEADS):
        sl = slice(h * HEAD_DIM, (h + 1) * HEAD_DIM)
        o = of_ref[:, sl] + ob_ref[:, sl]
        on = o * lax.rsqrt(jnp.mean(jnp.square(o), axis=-1, keepdims=True) + NORM_EPS) * hg_ref[...]
        g = cg_ref[:, sl]
        parts.append((on * (g * _sigmoid(g))).astype(BF16))
    c_x = jnp.concatenate(parts, axis=1)
    mix = jnp.dot(a_ref[...], w_ref[0:A_WIDTH, :], preferred_element_type=F32)
    mix = mix + jnp.dot(b_ref[...], w_ref[A_WIDTH:A_WIDTH + B_WIDTH, :], preferred_element_type=F32)
    mix = mix + jnp.dot(c_x, w_ref[A_WIDTH + B_WIDTH:, :], preferred_element_type=F32)
    g1 = mod_ref[0, 2:3, :]
    sh2 = mod_ref[0, 3:4, :]
    sc2 = mod_ref[0, 4:5, :]
    x1 = _layer_norm_rows(DEEPNORM_ALPHA * x_ref[...] + g1 * mix, l1g_ref[...], l1b_ref[...])
    x1_ref[...] = x1
    hx = x1 * (1.0 + sc2) + sh2
    hx_ref[...] = hx.astype(BF16)

    logits = jnp.dot(hx, rw_ref[...], preferred_element_type=F32, precision=lax.Precision.HIGHEST)
    scores = _sigmoid(logits)
    sel = scores + rb_ref[...]
    tm = sel.shape[0]
    lane = lax.broadcasted_iota(jnp.int32, sel.shape, 1).astype(F32)
    slot = lax.broadcasted_iota(jnp.int32, (tm, LANE), 1)
    idx_acc = jnp.zeros((tm, LANE), F32)
    wt_acc = jnp.zeros((tm, LANE), F32)
    chosen = jnp.zeros(sel.shape, F32)
    hits = []
    for k in range(TOP_K):
        mx = jnp.max(sel, axis=-1, keepdims=True)
        idx = jnp.min(jnp.where(sel == mx, lane, float(N_EXPERTS)), axis=-1, keepdims=True)
        hit = lane == idx
        hits.append(hit)
        chosen = jnp.where(hit, 1.0, chosen)
        w_k = jnp.sum(jnp.where(hit, scores, 0.0), axis=-1, keepdims=True)
        idx_acc = jnp.where(slot == k, idx, idx_acc)
        wt_acc = jnp.where(slot == k, w_k, wt_acc)
        sel = jnp.where(hit, -jnp.inf, sel)
    idx_ref[...] = idx_acc.astype(jnp.int32)
    wt_ref[...] = wt_acc / jnp.sum(wt_acc, axis=-1, keepdims=True) * ROUTED_SCALE

    @pl.when(pl.program_id(0) == 0)
    def _():
        cnt_scr[...] = jnp.zeros(cnt_scr.shape, F32)

    below = (lax.broadcasted_iota(jnp.int32, (tm, tm), 0) > lax.broadcasted_iota(jnp.int32, (tm, tm), 1))
    before = jnp.dot(below.astype(BF16), chosen.astype(BF16), preferred_element_type=F32) + cnt_scr[0:1, :]
    rank_acc = jnp.zeros((tm, LANE), F32)
    for k in range(TOP_K):
        r_k = jnp.sum(jnp.where(hits[k], before, 0.0), axis=-1, keepdims=True)
        rank_acc = jnp.where(slot == k, r_k, rank_acc)
    rank_ref[...] = rank_acc.astype(jnp.int32)
    cnt = cnt_scr[...] + jnp.sum(chosen, axis=0, keepdims=True)
    cnt_scr[...] = cnt
    cnt_ref[...] = cnt


def _outproj(a_x, b_x, o_f, o_b, p_hi, hnorm_g, x, mod3, w_out, ln_g, ln_b, router_w, router_b):
    row = lambda width: pl.BlockSpec((OUT_TM, width), lambda i: (i, 0))
    const = lambda shape: pl.BlockSpec(shape, lambda i: tuple(0 for _ in shape))
    return pl.pallas_call(
        _outproj_kernel,
        grid=(T_ROWS // OUT_TM,),
        in_specs=[row(A_WIDTH), row(B_WIDTH), row(C_WIDTH), row(C_WIDTH),
                  pl.BlockSpec((OUT_TM, C_WIDTH), lambda i: (i, 4)),
                  const((1, HEAD_DIM)),
                  row(D_MODEL),
                  pl.BlockSpec((1, 6, D_MODEL), lambda i: (_mod_row(i, OUT_TM), 0, 0)),
                  const((D_MODEL, D_MODEL)),
                  const((1, D_MODEL)), const((1, D_MODEL)),
                  const((D_MODEL, N_EXPERTS)), const((1, N_EXPERTS))],
        out_specs=[row(D_MODEL), row(D_MODEL), row(LANE), row(LANE), row(LANE), const((8, N_EXPERTS))],
        out_shape=[jax.ShapeDtypeStruct((T_ROWS, D_MODEL), F32),
                   jax.ShapeDtypeStruct((T_ROWS, D_MODEL), BF16),
                   jax.ShapeDtypeStruct((T_ROWS, LANE), jnp.int32),
                   jax.ShapeDtypeStruct((T_ROWS, LANE), F32),
                   jax.ShapeDtypeStruct((T_ROWS, LANE), jnp.int32),
                   jax.ShapeDtypeStruct((8, N_EXPERTS), F32)],
        scratch_shapes=[pltpu.VMEM((8, N_EXPERTS), F32)],
        compiler_params=_cparams(("arbitrary",)),
        name="outproj_ln_router",
    )(a_x, b_x, o_f, o_b, p_hi, hnorm_g, x, mod3, w_out, ln_g, ln_b, router_w, router_b)


def _experts_kernel(te_ref, nt_ref, xs_ref, wg_ref, wu_ref, wd_ref, ys_ref, wgu_scr, wd_scr):
    i = pl.program_id(0)
    live = i < nt_ref[0]
    new_expert = jnp.logical_or(i == 0, te_ref[i] != te_ref[jnp.maximum(i - 1, 0)])

    @pl.when(jnp.logical_and(live, new_expert))
    def _():
        wgu_scr[:, :EXPERT_DIM] = wg_ref[0, 0].astype(BF16)
        wgu_scr[:, EXPERT_DIM:] = wu_ref[0, 0].astype(BF16)
        wd_scr[...] = wd_ref[0, 0].astype(BF16)

    @pl.when(live)
    def _():
        h = jnp.dot(xs_ref[...], wgu_scr[...], preferred_element_type=F32)
        g = h[:, :EXPERT_DIM]
        u = h[:, EXPERT_DIM:]
        a = (g * _sigmoid(g) * u).astype(BF16)
        ys_ref[...] = jnp.dot(a, wd_scr[...], preferred_element_type=F32).astype(ys_ref.dtype)


def _experts(tile_expert, n_tiles, xs, w_gate, w_up, w_down, layer):
    tiles = xs.shape[0] // MOE_TM
    last = lambda i, nt: jnp.minimum(i, nt[0] - 1)
    grid_spec = pltpu.PrefetchScalarGridSpec(
        num_scalar_prefetch=2,
        grid=(tiles,),
        in_specs=[pl.BlockSpec((MOE_TM, D_MODEL), lambda i, te, nt: (last(i, nt), 0)),
                  pl.BlockSpec((1, 1, D_MODEL, EXPERT_DIM), lambda i, te, nt: (layer, te[i], 0, 0)),
                  pl.BlockSpec((1, 1, D_MODEL, EXPERT_DIM), lambda i, te, nt: (layer, te[i], 0, 0)),
                  pl.BlockSpec((1, 1, EXPERT_DIM, D_MODEL), lambda i, te, nt: (layer, te[i], 0, 0))],
        out_specs=pl.BlockSpec((MOE_TM, D_MODEL), lambda i, te, nt: (last(i, nt), 0)),
        scratch_shapes=[pltpu.VMEM((D_MODEL, 2 * EXPERT_DIM), BF16),
                        pltpu.VMEM((EXPERT_DIM, D_MODEL), BF16)],
    )
    return pl.pallas_call(
        _experts_kernel,
        grid_spec=grid_spec,
        out_shape=jax.ShapeDtypeStruct((xs.shape[0], D_MODEL), BF16),
        compiler_params=_cparams(("arbitrary",)),
        name="experts",
    )(tile_expert, n_tiles, xs, w_gate, w_up, w_down)


def _combine_kernel(yg_ref, wt_ref, ysh_ref, x1_ref, mod_ref, g_ref, b_ref, o_ref):
    y = ysh_ref[...].astype(F32)
    for k in range(TOP_K):
        y = y + wt_ref[:, k:k + 1] * yg_ref[k].astype(F32)
    g2 = mod_ref[0, 5:6, :]
    o_ref[...] = _layer_norm_rows(DEEPNORM_ALPHA * x1_ref[...] + g2 * y, g_ref[...], b_ref[...])


def _combine(yg, wt, ysh, x1, mod3, ln_g, ln_b):
    tm = COMBINE_TM
    row = lambda width: pl.BlockSpec((tm, width), lambda i: (i, 0))
    return pl.pallas_call(
        _combine_kernel,
        grid=(T_ROWS // tm,),
        in_specs=[pl.BlockSpec((TOP_K, tm, D_MODEL), lambda i: (0, i, 0)),
                  row(LANE), row(D_MODEL), row(D_MODEL),
                  pl.BlockSpec((1, 6, D_MODEL), lambda i: (_mod_row(i, tm), 0, 0)),
                  pl.BlockSpec((1, D_MODEL), lambda i: (0, 0)),
                  pl.BlockSpec((1, D_MODEL), lambda i: (0, 0))],
        out_specs=row(D_MODEL),
        out_shape=jax.ShapeDtypeStruct((T_ROWS, D_MODEL), F32),
        compiler_params=_cparams(("parallel",)),
        name="moe_combine_ln",
    )(yg, wt, ysh, x1, mod3, ln_g, ln_b)


def _dispatch(idx, rank, counts):
    tiles_per = (counts + MOE_TM - 1) // MOE_TM
    tile_end = jnp.cumsum(tiles_per)
    start = (tile_end - tiles_per) * MOE_TM
    experts = jnp.arange(N_EXPERTS, dtype=jnp.int32)
    pos = rank + jnp.sum(jnp.where(idx[:, :, None] == experts, start, 0), axis=-1)
    tok = jnp.broadcast_to(jnp.arange(T_ROWS, dtype=jnp.int32)[:, None], idx.shape)
    tok_of_row = jnp.zeros((MOE_ROWS,), jnp.int32).at[pos.reshape(-1)].set(
        tok.reshape(-1), unique_indices=True)
    n_tiles = tile_end[-1:].astype(jnp.int32)
    tile = jnp.arange(MOE_TILES, dtype=jnp.int32)
    tile_expert = jnp.sum((tile_end[None, :] <= tile[:, None]).astype(jnp.int32), axis=-1)
    tile_expert = jnp.minimum(tile_expert, N_EXPERTS - 1)
    return tok_of_row, pos, tile_expert, n_tiles


def _rope_tables():
    t = jnp.arange(SEQ)
    row = (t // GRID_W).astype(F32)
    col = (t % GRID_W).astype(F32)
    axis_dim = HEAD_DIM // 2
    inv = ROPE_THETA ** (-jnp.arange(0, axis_dim, 2, dtype=F32) / axis_dim)
    ar = row[:, None] * inv
    ac = col[:, None] * inv
    cos = jnp.concatenate([jnp.cos(ar), jnp.cos(ar), jnp.cos(ac), jnp.cos(ac)], axis=-1)
    sin = jnp.concatenate([jnp.sin(ar), jnp.sin(ar), jnp.sin(ac), jnp.sin(ac)], axis=-1)
    odd = ((jnp.arange(HEAD_DIM) // 32) % 2 == 1)[None, :]
    sa = jnp.where(odd, sin, 0.0)
    sb = jnp.where(odd, 0.0, -sin)
    lat = lambda z: jnp.tile(z, (BATCH, 1))
    cos_all = jnp.concatenate([lat(cos), jnp.ones((N_CTX, HEAD_DIM), F32)], axis=0)
    sa_all = jnp.concatenate([lat(sa), jnp.zeros((N_CTX, HEAD_DIM), F32)], axis=0)
    sb_all = jnp.concatenate([lat(sb), jnp.zeros((N_CTX, HEAD_DIM), F32)], axis=0)
    return cos_all, sa_all, sb_all


def kernel(x, c, ctx, c_ctx, w_mod, b_mod, w_in, w_out, gmlp_ln_g, gmlp_ln_b, gmlp_ws, gmlp_bs,
           diff_lam, diff_subln_g, hgrn_lb, hgrn_norm_g, ln1_g, ln1_b, ln2_g, ln2_b,
           router_w, router_b, exp_w_gate, exp_w_up, exp_w_down, sh_w_gate, sh_w_up, sh_w_down):
    assert x.shape == (BATCH, SEQ, D_MODEL) and ctx.shape == (BATCH, CTX_LEN, D_MODEL)
    xs_all = jnp.concatenate([x.reshape(N_LAT, D_MODEL), ctx.reshape(N_CTX, D_MODEL)], axis=0)

    cond = jnp.zeros((8, D_MODEL), F32).at[0:BATCH].set(c).at[BATCH].set(c_ctx)
    mod_all = _modulation(cond, w_mod, b_mod)[:, :BATCH + 1].reshape(DEPTH, BATCH + 1, 6, D_MODEL)

    sm = jax.nn.softmax(hgrn_lb.astype(F32), axis=0)
    lb_all = jnp.cumsum(sm, axis=0) - sm[0]
    tables = _rope_tables()
    xmat_np, masks_np = _scan_structure()
    xmat = jnp.asarray(xmat_np, BF16)
    masks = jnp.asarray(masks_np, F32)

    for l in range(DEPTH):
        mod3 = mod_all[l]
        w_in_l = w_in[l].astype(BF16)
        p_lo = _inproj(xs_all, mod3, w_in_l[:, :LOW_WIDTH], tables, BF16, True)
        p_hi = _inproj(xs_all, mod3, w_in_l[:, LOW_WIDTH:], tables, F32, False)

        a_x = _gmlp(p_lo, gmlp_ln_g[l][None, :], gmlp_ln_b[l][None, :],
                    gmlp_ws[l].astype(BF16), gmlp_bs[l].T)

        lam_init = 0.8 - 0.6 * math.exp(-0.3 * l)
        dl = diff_lam[l].astype(F32)
        lam = jnp.exp(jnp.sum(dl[0] * dl[1])) - jnp.exp(jnp.sum(dl[2] * dl[3])) + lam_init
        lam2 = jnp.stack([lam, jnp.asarray(1.0 - lam_init, F32)]).astype(F32)
        b_x = _diff_attention(p_lo, lam2, diff_subln_g[l][None, :])

        o_f, o_b = _hgrn_scan(p_hi, lb_all[l], xmat, masks)

        x1, hx, idx, wt, rank, cnt = _outproj(
            a_x, b_x, o_f, o_b, p_hi, hgrn_norm_g[l][None, :], xs_all, mod3,
            w_out[l].astype(BF16), ln1_g[l][None, :], ln1_b[l][None, :],
            router_w[l], router_b[l][None, :])

        tok_of_row, pos, tile_expert, n_tiles = _dispatch(
            idx[:, :TOP_K], rank[:, :TOP_K], cnt[0].astype(jnp.int32))
        xs_grouped = jnp.take(hx, tok_of_row, axis=0)
        ys = _experts(tile_expert, n_tiles, xs_grouped, exp_w_gate, exp_w_up, exp_w_down, l)
        yg = jnp.take(ys, pos.T.reshape(-1), axis=0).reshape(TOP_K, T_ROWS, D_MODEL)

        ysh = _experts(jnp.zeros((T_ROWS // MOE_TM,), jnp.int32),
                       jnp.full((1,), T_ROWS // MOE_TM, jnp.int32),
                       hx, sh_w_gate[:, None], sh_w_up[:, None], sh_w_down[:, None], l)

        xs_all = _combine(yg, wt, ysh, x1, mod3, ln2_g[l][None, :], ln2_b[l][None, :])

    return xs_all[:N_LAT].reshape(BATCH, SEQ, D_MODEL)
```

```python
import functools
import math

import numpy as np
import jax
import jax.numpy as jnp
from jax import lax
from jax.experimental import pallas as pl
from jax.experimental.pallas import tpu as pltpu

F32 = jnp.float32
BF16 = jnp.bfloat16

D_MODEL = 2048
BATCH = 2
SEQ = 4096
DEPTH = 4
CTX_LEN = 256
GRID_W = 64
HEAD_DIM = 128
A_WIDTH = 512
A_HEADS = 4
A_CHUNK = 128
B_WIDTH = 1024
B_HEADS = 4
B_QK_WIDTH = 1024
B_V_DIM = 256
ROPE_THETA = 10000.0
C_WIDTH = 512
C_HEADS = 4
IN_WIDTH = 6656
N_EXPERTS = 64
TOP_K = 8
EXPERT_DIM = 384
ROUTED_SCALE = 2.5
DEEPNORM_ALPHA = (2.0 * DEPTH) ** 0.25
LN_EPS = 1e-5
NORM_EPS = 1e-5

N_LAT = BATCH * SEQ
N_CTX = BATCH * CTX_LEN
T_ROWS = N_LAT + N_CTX
LOW_WIDTH = 4096
HI_WIDTH = IN_WIDTH - LOW_WIDTH
QK_SCALE = HEAD_DIM ** -0.5 * math.log2(math.e)

LANE = 128
PROJ_TM = 512
PROJ_TN = 512
ATT_TQ = 256
ATT_TK = 1024
SCAN_C = 128
SCAN_LEVELS = 7
OUT_TM = 256
MOE_TM = 256
COMBINE_TM = 128
MOE_ROWS = ((T_ROWS * TOP_K + N_EXPERTS * (MOE_TM - 1)) // MOE_TM) * MOE_TM
MOE_TILES = MOE_ROWS // MOE_TM
VMEM_LIMIT = 48 * 1024 * 1024


def _cparams(sem, vmem=VMEM_LIMIT):
    return pltpu.CompilerParams(dimension_semantics=sem, vmem_limit_bytes=vmem)


def _sigmoid(x):
    return 1.0 / (1.0 + jnp.exp(-x))


def _gelu_tanh(x):
    return 0.5 * x * (1.0 + jnp.tanh(math.sqrt(2.0 / math.pi) * (x + 0.044715 * (x * x * x))))


def _dot_nt(a, b):
    return lax.dot_general(a, b, (((1,), (1,)), ((), ())), preferred_element_type=F32)


def _dot_tn(a, b):
    return lax.dot_general(a, b, (((0,), (0,)), ((), ())), preferred_element_type=F32)


def _mod_kernel(c_ref, w_ref, b_ref, o_ref):
    c = c_ref[...]
    s = (c * _sigmoid(c)).astype(BF16)
    o_ref[0] = jnp.dot(s, w_ref[0].astype(BF16), preferred_element_type=F32) + b_ref[0]


def _modulation(cond, w_mod, b_mod):
    tn = 1024
    return pl.pallas_call(
        _mod_kernel,
        grid=(DEPTH, 6 * D_MODEL // tn),
        in_specs=[pl.BlockSpec((8, D_MODEL), lambda l, n: (0, 0)),
                  pl.BlockSpec((1, D_MODEL, tn), lambda l, n: (l, 0, n)),
                  pl.BlockSpec((1, 1, tn), lambda l, n: (l, 0, n))],
        out_specs=pl.BlockSpec((1, 8, tn), lambda l, n: (l, 0, n)),
        out_shape=jax.ShapeDtypeStruct((DEPTH, 8, 6 * D_MODEL), F32),
        compiler_params=_cparams(("parallel", "parallel")),
        name="modulation",
    )(cond, w_mod, b_mod.reshape(DEPTH, 1, 6 * D_MODEL))


def _mod_row(m, tm):
    return jnp.minimum(m // (SEQ // tm), BATCH)


def _inproj_kernel(x_ref, mod_ref, w_ref, cos_ref, sa_ref, sb_ref, o_ref, h_scr, *, rope):
    n = pl.program_id(1)

    @pl.when(n == 0)
    def _():
        sh = mod_ref[0, 0:1, :]
        sc = mod_ref[0, 1:2, :]
        h_scr[...] = (x_ref[...] * (1.0 + sc) + sh).astype(BF16)

    acc = jnp.dot(h_scr[...], w_ref[...], preferred_element_type=F32)
    if not rope:
        o_ref[...] = acc.astype(o_ref.dtype)
        return

    is_qk = (n >= 2) & (n < 6)

    @pl.when(is_qk)
    def _():
        scale = jnp.where(n < 4, QK_SCALE, 1.0).astype(F32)
        cos = cos_ref[...]
        sa = sa_ref[...]
        sb = sb_ref[...]
        for g in range(PROJ_TN // LANE):
            blk = acc[:, g * LANE:(g + 1) * LANE]
            r = blk * cos + pltpu.roll(blk, 32, 1) * sa + pltpu.roll(blk, 96, 1) * sb
            o_ref[:, g * LANE:(g + 1) * LANE] = (r * scale).astype(o_ref.dtype)

    @pl.when(jnp.logical_not(is_qk))
    def _():
        o_ref[...] = acc.astype(o_ref.dtype)


def _inproj(x, mod3, w, tables, out_dtype, rope):
    width = w.shape[1]
    cos, sa, sb = tables
    tab_spec = pl.BlockSpec((PROJ_TM, LANE), lambda m, n: (m, 0))
    return pl.pallas_call(
        functools.partial(_inproj_kernel, rope=rope),
        grid=(T_ROWS // PROJ_TM, width // PROJ_TN),
        in_specs=[pl.BlockSpec((PROJ_TM, D_MODEL), lambda m, n: (m, 0)),
                  pl.BlockSpec((1, 6, D_MODEL), lambda m, n: (_mod_row(m, PROJ_TM), 0, 0)),
                  pl.BlockSpec((D_MODEL, PROJ_TN), lambda m, n: (0, n)),
                  tab_spec, tab_spec, tab_spec],
        out_specs=pl.BlockSpec((PROJ_TM, PROJ_TN), lambda m, n: (m, n)),
        out_shape=jax.ShapeDtypeStruct((T_ROWS, width), out_dtype),
        scratch_shapes=[pltpu.VMEM((PROJ_TM, D_MODEL), BF16)],
        compiler_params=_cparams(("parallel", "arbitrary")),
        name="inproj_rope" if rope else "inproj",
    )(x, mod3, w, cos, sa, sb)


def _gmlp_kernel(u_ref, v_ref, g_ref, b_ref, ws_ref, bs_ref, o_ref):
    for h in range(A_HEADS):
        sl = slice(h * HEAD_DIM, (h + 1) * HEAD_DIM)
        v = _gelu_tanh(v_ref[:, sl].astype(F32))
        mu = jnp.mean(v, axis=-1, keepdims=True)
        var = jnp.mean(jnp.square(v - mu), axis=-1, keepdims=True)
        vn = (v - mu) * lax.rsqrt(var + LN_EPS) * g_ref[:, sl] + b_ref[:, sl]
        s = jnp.dot(ws_ref[h], vn.astype(BF16), preferred_element_type=F32) + bs_ref[:, h:h + 1]
        u = _gelu_tanh(u_ref[:, sl].astype(F32))
        o_ref[:, sl] = (u * s).astype(o_ref.dtype)


def _gmlp(p_lo, ln_g, ln_b, ws, bs_t):
    return pl.pallas_call(
        _gmlp_kernel,
        grid=(T_ROWS // A_CHUNK,),
        in_specs=[pl.BlockSpec((A_CHUNK, A_WIDTH), lambda i: (i, 0)),
                  pl.BlockSpec((A_CHUNK, A_WIDTH), lambda i: (i, 1)),
                  pl.BlockSpec((1, A_WIDTH), lambda i: (0, 0)),
                  pl.BlockSpec((1, A_WIDTH), lambda i: (0, 0)),
                  pl.BlockSpec((A_HEADS, A_CHUNK, A_CHUNK), lambda i: (0, 0, 0)),
                  pl.BlockSpec((A_CHUNK, A_HEADS), lambda i: (0, 0))],
        out_specs=pl.BlockSpec((A_CHUNK, A_WIDTH), lambda i: (i, 0)),
        out_shape=jax.ShapeDtypeStruct((T_ROWS, A_WIDTH), BF16),
        compiler_params=_cparams(("parallel",)),
        name="gmlp",
    )(p_lo, p_lo, ln_g, ln_b, ws, bs_t)


NQ_LAT = SEQ // ATT_TQ


def _attn_kernel(lam_ref, q_ref, kl_ref, vl_ref, kc_ref, vc_ref, g_ref, o_ref, m_scr, l_scr, acc_scr):
    qi = pl.program_id(2)
    m_scr[...] = jnp.full(m_scr.shape, -1e30, F32)
    l_scr[...] = jnp.zeros(l_scr.shape, F32)
    acc_scr[...] = jnp.zeros(acc_scr.shape, F32)

    def process(k, v):
        for mp in range(2):
            sl = slice(mp * HEAD_DIM, (mp + 1) * HEAD_DIM)
            s = _dot_nt(q_ref[:, sl], k[:, sl])
            cols = [s[:, c * LANE:(c + 1) * LANE] for c in range(k.shape[0] // LANE)]
            m_part = cols[0]
            for col in cols[1:]:
                m_part = jnp.maximum(m_part, col)
            m_old = m_scr[mp]
            m_new = jnp.maximum(m_old, jnp.max(m_part, axis=-1, keepdims=True))
            alpha = jnp.exp2(m_old - m_new)
            l_part = alpha * l_scr[mp]
            ps = []
            for col in cols:
                p = jnp.exp2(col - m_new)
                l_part = l_part + p
                ps.append(p.astype(BF16))
            pv = jnp.dot(jnp.concatenate(ps, axis=1), v, preferred_element_type=F32)
            for half in range(B_V_DIM // LANE):
                hs = slice(half * LANE, (half + 1) * LANE)
                acc_scr[mp, :, hs] = alpha * acc_scr[mp, :, hs] + pv[:, hs]
            l_scr[mp] = l_part
            m_scr[mp] = m_new

    n_lat_chunks = jnp.where(qi < NQ_LAT, SEQ // ATT_TK, 0)

    def body(j, carry):
        rows = pl.ds(pl.multiple_of(j * ATT_TK, ATT_TK), ATT_TK)
        process(kl_ref[rows, :], vl_ref[rows, :])
        return carry

    lax.fori_loop(0, n_lat_chunks, body, 0)
    process(kc_ref[...], vc_ref[...])

    lam = lam_ref[0]
    post = lam_ref[1]
    l0 = jnp.sum(l_scr[0], axis=-1, keepdims=True)
    l1 = jnp.sum(l_scr[1], axis=-1, keepdims=True)
    o = acc_scr[0] / l0 - lam * (acc_scr[1] / l1)
    r = o * lax.rsqrt(jnp.mean(jnp.square(o), axis=-1, keepdims=True) + NORM_EPS)
    o_ref[...] = (r * g_ref[...] * post).astype(o_ref.dtype)


def _diff_attention(p_lo, lam2, subln_g):
    def q_rows(b, h, qi):
        return jnp.where(qi < NQ_LAT, b * NQ_LAT + qi, BATCH * NQ_LAT + b)

    return pl.pallas_call(
        _attn_kernel,
        grid=(BATCH, B_HEADS, NQ_LAT + 1),
        in_specs=[pl.BlockSpec(memory_space=pltpu.SMEM),
                  pl.BlockSpec((ATT_TQ, B_V_DIM), lambda b, h, qi: (q_rows(b, h, qi), 4 + h)),
                  pl.BlockSpec((SEQ, B_V_DIM), lambda b, h, qi: (b, 8 + h)),
                  pl.BlockSpec((SEQ, B_V_DIM), lambda b, h, qi: (b, 12 + h)),
                  pl.BlockSpec((CTX_LEN, B_V_DIM), lambda b, h, qi: (N_LAT // CTX_LEN + b, 8 + h)),
                  pl.BlockSpec((CTX_LEN, B_V_DIM), lambda b, h, qi: (N_LAT // CTX_LEN + b, 12 + h)),
                  pl.BlockSpec((1, B_V_DIM), lambda b, h, qi: (0, 0))],
        out_specs=pl.BlockSpec((ATT_TQ, B_V_DIM), lambda b, h, qi: (q_rows(b, h, qi), h)),
        scratch_shapes=[pltpu.VMEM((2, ATT_TQ, LANE), F32),
                        pltpu.VMEM((2, ATT_TQ, LANE), F32),
                        pltpu.VMEM((2, ATT_TQ, B_V_DIM), F32)],
        out_shape=jax.ShapeDtypeStruct((T_ROWS, B_WIDTH), BF16),
        compiler_params=_cparams(("parallel", "parallel", "arbitrary")),
        name="diff_attention",
    )(lam2, p_lo, p_lo, p_lo, p_lo, p_lo, subln_g)


def _scan_structure():
    c = SCAN_C
    x = np.zeros((2, (2 + SCAN_LEVELS) * c, c), np.float32)
    msk = np.zeros((2, SCAN_LEVELS + 1, c, c), np.float32)
    r = np.arange(c)
    j = np.arange(c)[None, :]
    xf = np.zeros(((2 + SCAN_LEVELS) * c, c), np.float32)
    mf = np.zeros((SCAN_LEVELS + 1, c, c), np.float32)
    xf[0:c] = (j <= r[:, None])
    xf[c:2 * c] = (j > r[:, None])
    for lev in range(SCAN_LEVELS):
        half = (c // 2) >> lev
        start = (r // (2 * half)) * (2 * half)
        mid = start + half - 1
        later = r > mid
        rows = np.where(later[:, None], (j > mid[:, None]) & (j <= r[:, None]),
                        (j > r[:, None]) & (j <= mid[:, None]))
        xf[(2 + lev) * c:(3 + lev) * c] = rows
        same = start[:, None] == start[None, :]
        mf[lev] = same & later[:, None] & (~later)[None, :]
    mf[SCAN_LEVELS] = np.eye(c)
    x[0] = xf
    msk[0] = mf
    x[1] = xf.reshape(2 + SCAN_LEVELS, c, c)[:, ::-1, ::-1].reshape(-1, c)
    msk[1] = mf[:, ::-1, ::-1]
    return x, msk


def _hgrn_kernel(qf_ref, ff_ref, if_ref, qb_ref, fb_ref, ib_ref, lb_ref, x_ref, msk_ref,
                 of_ref, ob_ref, st_scr):
    j = pl.program_id(1)

    @pl.when(j == 0)
    def _():
        st_scr[...] = jnp.zeros(st_scr.shape, F32)

    c = SCAN_C
    dirs = ((qf_ref, ff_ref, if_ref, of_ref, c - 1), (qb_ref, fb_ref, ib_ref, ob_ref, 0))
    for d, (q_ref, f_ref, i_ref, o_ref, end_row) in enumerate(dirs):
        for h in range(C_HEADS):
            sl = slice(h * HEAD_DIM, (h + 1) * HEAD_DIM)
            qraw = q_ref[:, sl]
            q = qraw * _sigmoid(qraw)
            lb = lb_ref[d:d + 1, sl]
            f = lb + (1.0 - lb) * _sigmoid(f_ref[:, sl])
            logf = jnp.log(f)
            k = 1.0 - f
            v = i_ref[:, sl].astype(BF16)
            hi = logf.astype(BF16)
            r1 = logf - hi.astype(F32)
            mid = r1.astype(BF16)
            lo = (r1 - mid.astype(F32)).astype(BF16)
            e3 = jnp.dot(x_ref[d], jnp.concatenate([hi, mid, lo], axis=1), preferred_element_type=F32)
            w = jnp.exp(e3[:, 0:c] + e3[:, c:2 * c] + e3[:, 2 * c:3 * c])
            a = msk_ref[d, SCAN_LEVELS] * _dot_nt(q.astype(BF16), k.astype(BF16))
            for lev in range(SCAN_LEVELS):
                wl = w[(2 + lev) * c:(3 + lev) * c]
                a = a + msk_ref[d, lev] * _dot_nt((q * wl).astype(BF16), (k * wl).astype(BF16))
            st = st_scr[d, h]
            o = jnp.dot(a.astype(BF16), v, preferred_element_type=F32)
            o = o + _dot_nt((q * w[0:c]).astype(BF16), st.astype(BF16))
            o_ref[:, sl] = o
            g_end = w[end_row:end_row + 1, :]
            st_scr[d, h] = st * g_end + _dot_tn(v, (k * w[c:2 * c]).astype(BF16))


def _hgrn_scan(p_hi, lb2, xmat, masks):
    lat_chunks = SEQ // SCAN_C
    ctx_chunks = CTX_LEN // SCAN_C
    steps = ctx_chunks + lat_chunks

    def fwd_rows(b, j):
        return jnp.where(j < ctx_chunks, N_LAT // SCAN_C + b * ctx_chunks + j,
                         b * lat_chunks + j - ctx_chunks)

    def bwd_rows(b, j):
        return jnp.where(j < ctx_chunks, N_LAT // SCAN_C + b * ctx_chunks + (ctx_chunks - 1 - j),
                         b * lat_chunks + (steps - 1 - j))

    def spec(rows, col):
        return pl.BlockSpec((SCAN_C, C_WIDTH), lambda b, j: (rows(b, j), col))

    nx = (2 + SCAN_LEVELS) * SCAN_C
    return pl.pallas_call(
        _hgrn_kernel,
        grid=(BATCH, steps),
        in_specs=[spec(fwd_rows, 0), spec(fwd_rows, 1), spec(fwd_rows, 3),
                  spec(bwd_rows, 0), spec(bwd_rows, 2), spec(bwd_rows, 3),
                  pl.BlockSpec((2, C_WIDTH), lambda b, j: (0, 0)),
                  pl.BlockSpec((2, nx, SCAN_C), lambda b, j: (0, 0, 0)),
                  pl.BlockSpec((2, SCAN_LEVELS + 1, SCAN_C, SCAN_C), lambda b, j: (0, 0, 0, 0))],
        out_specs=[spec(fwd_rows, 0), spec(bwd_rows, 0)],
        out_shape=[jax.ShapeDtypeStruct((T_ROWS, C_WIDTH), F32)] * 2,
        scratch_shapes=[pltpu.VMEM((2, C_HEADS, HEAD_DIM, HEAD_DIM), F32)],
        compiler_params=_cparams(("parallel", "arbitrary")),
        name="hgrn2_scan",
    )(p_hi, p_hi, p_hi, p_hi, p_hi, p_hi, lb2, xmat, masks)


def _layer_norm_rows(z, g, b):
    mu = jnp.mean(z, axis=-1, keepdims=True)
    var = jnp.mean(jnp.square(z - mu), axis=-1, keepdims=True)
    return (z - mu) * lax.rsqrt(var + LN_EPS) * g + b


def _outproj_kernel(a_ref, b_ref, of_ref, ob_ref, cg_ref, hg_ref, x_ref, mod_ref, w_ref,
                    l1g_ref, l1b_ref, rw_ref, rb_ref,
                    x1_ref, hx_ref, idx_ref, wt_ref, rank_ref, cnt_ref, cnt_scr):
    parts = []
    for h in range(C_HEADS):
        sl = slice(h * HEAD_DIM, (h + 1) * HEAD_DIM)
        o = of_ref[:, sl] + ob_ref[:, sl]
        on = o * lax.rsqrt(jnp.mean(jnp.square(o), axis=-1, keepdims=True) + NORM_EPS) * hg_ref[...]
        g = cg_ref[:, sl]
        parts.append((on * (g * _sigmoid(g))).astype(BF16))
    c_x = jnp.concatenate(parts, axis=1)
    mix = jnp.dot(a_ref[...], w_ref[0:A_WIDTH, :], preferred_element_type=F32)
    mix = mix + jnp.dot(b_ref[...], w_ref[A_WIDTH:A_WIDTH + B_WIDTH, :], preferred_element_type=F32)
    mix = mix + jnp.dot(c_x, w_ref[A_WIDTH + B_WIDTH:, :], preferred_element_type=F32)
    g1 = mod_ref[0, 2:3, :]
    sh2 = mod_ref[0, 3:4, :]
    sc2 = mod_ref[0, 4:5, :]
    x1 = _layer_norm_rows(DEEPNORM_ALPHA * x_ref[...] + g1 * mix, l1g_ref[...], l1b_ref[...])
    x1_ref[...] = x1
    hx = x1 * (1.0 + sc2) + sh2
    hx_ref[...] = hx.astype(BF16)

    logits = jnp.dot(hx.astype(BF16), rw_ref[...], preferred_element_type=F32)
    scores = _sigmoid(logits)
    sel = scores + rb_ref[...]
    tm = sel.shape[0]
    lane = lax.broadcasted_iota(jnp.int32, sel.shape, 1).astype(F32)
    slot = lax.broadcasted_iota(jnp.int32, (tm, LANE), 1)
    idx_acc = jnp.zeros((tm, LANE), F32)
    wt_acc = jnp.zeros((tm, LANE), F32)
    chosen = jnp.zeros(sel.shape, F32)
    hits = []
    for k in range(TOP_K):
        mx = jnp.max(sel, axis=-1, keepdims=True)
        idx = jnp.min(jnp.where(sel == mx, lane, float(N_EXPERTS)), axis=-1, keepdims=True)
        hit = lane == idx
        hits.append(hit)
        chosen = jnp.where(hit, 1.0, chosen)
        w_k = jnp.sum(jnp.where(hit, scores, 0.0), axis=-1, keepdims=True)
        idx_acc = jnp.where(slot == k, idx, idx_acc)
        wt_acc = jnp.where(slot == k, w_k, wt_acc)
        sel = jnp.where(hit, -jnp.inf, sel)
    idx_ref[...] = idx_acc.astype(jnp.int32)
    wt_ref[...] = wt_acc / jnp.sum(wt_acc, axis=-1, keepdims=True) * ROUTED_SCALE

    @pl.when(pl.program_id(0) == 0)
    def _():
        cnt_scr[...] = jnp.zeros(cnt_scr.shape, F32)

    below = (lax.broadcasted_iota(jnp.int32, (tm, tm), 0) > lax.broadcasted_iota(jnp.int32, (tm, tm), 1))
    before = jnp.dot(below.astype(BF16), chosen.astype(BF16), preferred_element_type=F32) + cnt_scr[0:1, :]
    rank_acc = jnp.zeros((tm, LANE), F32)
    for k in range(TOP_K):
        r_k = jnp.sum(jnp.where(hits[k], before, 0.0), axis=-1, keepdims=True)
        rank_acc = jnp.where(slot == k, r_k, rank_acc)
    rank_ref[...] = rank_acc.astype(jnp.int32)
    cnt = cnt_scr[...] + jnp.sum(chosen, axis=0, keepdims=True)
    cnt_scr[...] = cnt
    cnt_ref[...] = cnt


def _outproj(a_x, b_x, o_f, o_b, p_hi, hnorm_g, x, mod3, w_out, ln_g, ln_b, router_w, router_b):
    row = lambda width: pl.BlockSpec((OUT_TM, width), lambda i: (i, 0))
    const = lambda shape: pl.BlockSpec(shape, lambda i: tuple(0 for _ in shape))
    return pl.pallas_call(
        _outproj_kernel,
        grid=(T_ROWS // OUT_TM,),
        in_specs=[row(A_WIDTH), row(B_WIDTH), row(C_WIDTH), row(C_WIDTH),
                  pl.BlockSpec((OUT_TM, C_WIDTH), lambda i: (i, 4)),
                  const((1, HEAD_DIM)),
                  row(D_MODEL),
                  pl.BlockSpec((1, 6, D_MODEL), lambda i: (_mod_row(i, OUT_TM), 0, 0)),
                  const((D_MODEL, D_MODEL)),
                  const((1, D_MODEL)), const((1, D_MODEL)),
                  const((D_MODEL, N_EXPERTS)), const((1, N_EXPERTS))],
        out_specs=[row(D_MODEL), row(D_MODEL), row(LANE), row(LANE), row(LANE), const((8, N_EXPERTS))],
        out_shape=[jax.ShapeDtypeStruct((T_ROWS, D_MODEL), F32),
                   jax.ShapeDtypeStruct((T_ROWS, D_MODEL), BF16),
                   jax.ShapeDtypeStruct((T_ROWS, LANE), jnp.int32),
                   jax.ShapeDtypeStruct((T_ROWS, LANE), F32),
                   jax.ShapeDtypeStruct((T_ROWS, LANE), jnp.int32),
                   jax.ShapeDtypeStruct((8, N_EXPERTS), F32)],
        scratch_shapes=[pltpu.VMEM((8, N_EXPERTS), F32)],
        compiler_params=_cparams(("arbitrary",)),
        name="outproj_ln_router",
    )(a_x, b_x, o_f, o_b, p_hi, hnorm_g, x, mod3, w_out, ln_g, ln_b, router_w, router_b)


def _experts_kernel(te_ref, nt_ref, xs_ref, wg_ref, wu_ref, wd_ref, ys_ref, wgu_scr, wd_scr):
    i = pl.program_id(0)
    live = i < nt_ref[0]
    new_expert = jnp.logical_or(i == 0, te_ref[i] != te_ref[jnp.maximum(i - 1, 0)])

    @pl.when(jnp.logical_and(live, new_expert))
    def _():
        wgu_scr[:, :EXPERT_DIM] = wg_ref[0, 0].astype(BF16)
        wgu_scr[:, EXPERT_DIM:] = wu_ref[0, 0].astype(BF16)
        wd_scr[...] = wd_ref[0, 0].astype(BF16)

    @pl.when(live)
    def _():
        h = jnp.dot(xs_ref[...], wgu_scr[...], preferred_element_type=F32)
        g = h[:, :EXPERT_DIM]
        u = h[:, EXPERT_DIM:]
        a = (g * _sigmoid(g) * u).astype(BF16)
        ys_ref[...] = jnp.dot(a, wd_scr[...], preferred_element_type=F32).astype(ys_ref.dtype)

    @pl.when(jnp.logical_not(live))
    def _():
        ys_ref[...] = jnp.zeros(ys_ref.shape, ys_ref.dtype)


def _experts(tile_expert, n_tiles, xs, w_gate, w_up, w_down, layer, out_dtype):
    tiles = xs.shape[0] // MOE_TM
    last = lambda i, nt: jnp.minimum(i, nt[0] - 1)
    grid_spec = pltpu.PrefetchScalarGridSpec(
        num_scalar_prefetch=2,
        grid=(tiles,),
        in_specs=[pl.BlockSpec((MOE_TM, D_MODEL), lambda i, te, nt: (last(i, nt), 0)),
                  pl.BlockSpec((1, 1, D_MODEL, EXPERT_DIM), lambda i, te, nt: (layer, te[i], 0, 0)),
                  pl.BlockSpec((1, 1, D_MODEL, EXPERT_DIM), lambda i, te, nt: (layer, te[i], 0, 0)),
                  pl.BlockSpec((1, 1, EXPERT_DIM, D_MODEL), lambda i, te, nt: (layer, te[i], 0, 0))],
        out_specs=pl.BlockSpec((MOE_TM, D_MODEL), lambda i, te, nt: (i, 0)),
        scratch_shapes=[pltpu.VMEM((D_MODEL, 2 * EXPERT_DIM), BF16),
                        pltpu.VMEM((EXPERT_DIM, D_MODEL), BF16)],
    )
    return pl.pallas_call(
        _experts_kernel,
        grid_spec=grid_spec,
        out_shape=jax.ShapeDtypeStruct((xs.shape[0], D_MODEL), out_dtype),
        compiler_params=_cparams(("arbitrary",)),
        name="experts",
    )(tile_expert, n_tiles, xs, w_gate, w_up, w_down)


def _combine_kernel(pos_ref, wt_ref, ysh_ref, x1_ref, mod_ref, g_ref, b_ref, ys_hbm, o_ref, buf, sem):
    tm = COMBINE_TM
    n_rows = TOP_K * tm

    def row_copy(j):
        return pltpu.make_async_copy(ys_hbm.at[pl.ds(pos_ref[j], 1), :], buf.at[pl.ds(j, 1), :], sem)

    def issue(j, carry):
        row_copy(j).start()
        return carry

    lax.fori_loop(0, n_rows, issue, 0, unroll=8)
    pltpu.make_async_copy(ys_hbm.at[pl.ds(0, n_rows), :], buf, sem).wait()

    y = ysh_ref[...].astype(F32)
    for k in range(TOP_K):
        y = y + wt_ref[:, k:k + 1] * buf[k * tm:(k + 1) * tm, :]
    g2 = mod_ref[0, 5:6, :]
    o_ref[...] = _layer_norm_rows(DEEPNORM_ALPHA * x1_ref[...] + g2 * y, g_ref[...], b_ref[...])


def _combine(ys, pos_blocks, wt, ysh, x1, mod3, ln_g, ln_b):
    tm = COMBINE_TM
    row = lambda width: pl.BlockSpec((tm, width), lambda i: (i, 0))
    return pl.pallas_call(
        _combine_kernel,
        grid=(T_ROWS // tm,),
        in_specs=[pl.BlockSpec((TOP_K * tm,), lambda i: (i,), memory_space=pltpu.SMEM),
                  row(LANE), row(D_MODEL), row(D_MODEL),
                  pl.BlockSpec((1, 6, D_MODEL), lambda i: (_mod_row(i, tm), 0, 0)),
                  pl.BlockSpec((1, D_MODEL), lambda i: (0, 0)),
                  pl.BlockSpec((1, D_MODEL), lambda i: (0, 0)),
                  pl.BlockSpec(memory_space=pl.ANY)],
        out_specs=row(D_MODEL),
        out_shape=jax.ShapeDtypeStruct((T_ROWS, D_MODEL), F32),
        scratch_shapes=[pltpu.VMEM((TOP_K * tm, D_MODEL), F32), pltpu.SemaphoreType.DMA(())],
        compiler_params=_cparams(("arbitrary",)),
        name="moe_combine_ln",
    )(pos_blocks, wt, ysh, x1, mod3, ln_g, ln_b, ys)


def _dispatch(idx, rank, counts):
    tiles_per = (counts + MOE_TM - 1) // MOE_TM
    tile_end = jnp.cumsum(tiles_per)
    start = (tile_end - tiles_per) * MOE_TM
    experts = jnp.arange(N_EXPERTS, dtype=jnp.int32)
    pos = rank + jnp.sum(jnp.where(idx[:, :, None] == experts, start, 0), axis=-1)
    tok = jnp.broadcast_to(jnp.arange(T_ROWS, dtype=jnp.int32)[:, None], idx.shape)
    tok_of_row = jnp.zeros((MOE_ROWS,), jnp.int32).at[pos.reshape(-1)].set(
        tok.reshape(-1), unique_indices=True)
    n_tiles = tile_end[-1:].astype(jnp.int32)
    tile = jnp.arange(MOE_TILES, dtype=jnp.int32)
    tile_expert = jnp.sum((tile_end[None, :] <= tile[:, None]).astype(jnp.int32), axis=-1)
    tile_expert = jnp.minimum(tile_expert, N_EXPERTS - 1)
    return tok_of_row, pos, tile_expert, n_tiles


def _rope_tables():
    t = jnp.arange(SEQ)
    row = (t // GRID_W).astype(F32)
    col = (t % GRID_W).astype(F32)
    axis_dim = HEAD_DIM // 2
    inv = ROPE_THETA ** (-jnp.arange(0, axis_dim, 2, dtype=F32) / axis_dim)
    ar = row[:, None] * inv
    ac = col[:, None] * inv
    cos = jnp.concatenate([jnp.cos(ar), jnp.cos(ar), jnp.cos(ac), jnp.cos(ac)], axis=-1)
    sin = jnp.concatenate([jnp.sin(ar), jnp.sin(ar), jnp.sin(ac), jnp.sin(ac)], axis=-1)
    odd = ((jnp.arange(HEAD_DIM) // 32) % 2 == 1)[None, :]
    sa = jnp.where(odd, sin, 0.0)
    sb = jnp.where(odd, 0.0, -sin)
    lat = lambda z: jnp.tile(z, (BATCH, 1))
    cos_all = jnp.concatenate([lat(cos), jnp.ones((N_CTX, HEAD_DIM), F32)], axis=0)
    sa_all = jnp.concatenate([lat(sa), jnp.zeros((N_CTX, HEAD_DIM), F32)], axis=0)
    sb_all = jnp.concatenate([lat(sb), jnp.zeros((N_CTX, HEAD_DIM), F32)], axis=0)
    return cos_all, sa_all, sb_all


def kernel(x, c, ctx, c_ctx, w_mod, b_mod, w_in, w_out, gmlp_ln_g, gmlp_ln_b, gmlp_ws, gmlp_bs,
           diff_lam, diff_subln_g, hgrn_lb, hgrn_norm_g, ln1_g, ln1_b, ln2_g, ln2_b,
           router_w, router_b, exp_w_gate, exp_w_up, exp_w_down, sh_w_gate, sh_w_up, sh_w_down):
    assert x.shape == (BATCH, SEQ, D_MODEL) and ctx.shape == (BATCH, CTX_LEN, D_MODEL)
    xs_all = jnp.concatenate([x.reshape(N_LAT, D_MODEL), ctx.reshape(N_CTX, D_MODEL)], axis=0)

    cond = jnp.zeros((8, D_MODEL), F32).at[0:BATCH].set(c).at[BATCH].set(c_ctx)
    mod_all = _modulation(cond, w_mod, b_mod)[:, :BATCH + 1].reshape(DEPTH, BATCH + 1, 6, D_MODEL)

    sm = jax.nn.softmax(hgrn_lb.astype(F32), axis=0)
    lb_all = jnp.cumsum(sm, axis=0) - sm[0]
    tables = _rope_tables()
    xmat_np, masks_np = _scan_structure()
    xmat = jnp.asarray(xmat_np, BF16)
    masks = jnp.asarray(masks_np, F32)

    for l in range(DEPTH):
        mod3 = mod_all[l]
        w_in_l = w_in[l].astype(BF16)
        p_lo = _inproj(xs_all, mod3, w_in_l[:, :LOW_WIDTH], tables, BF16, True)
        p_hi = _inproj(xs_all, mod3, w_in_l[:, LOW_WIDTH:], tables, F32, False)

        a_x = _gmlp(p_lo, gmlp_ln_g[l][None, :], gmlp_ln_b[l][None, :],
                    gmlp_ws[l].astype(BF16), gmlp_bs[l].T)

        lam_init = 0.8 - 0.6 * math.exp(-0.3 * l)
        dl = diff_lam[l].astype(F32)
        lam = jnp.exp(jnp.sum(dl[0] * dl[1])) - jnp.exp(jnp.sum(dl[2] * dl[3])) + lam_init
        lam2 = jnp.stack([lam, jnp.asarray(1.0 - lam_init, F32)]).astype(F32)
        b_x = _diff_attention(p_lo, lam2, diff_subln_g[l][None, :])

        o_f, o_b = _hgrn_scan(p_hi, lb_all[l], xmat, masks)

        x1, hx, idx, wt, rank, cnt = _outproj(
            a_x, b_x, o_f, o_b, p_hi, hgrn_norm_g[l][None, :], xs_all, mod3,
            w_out[l].astype(BF16), ln1_g[l][None, :], ln1_b[l][None, :],
            router_w[l].astype(BF16), router_b[l][None, :])

        tok_of_row, pos, tile_expert, n_tiles = _dispatch(
            idx[:, :TOP_K], rank[:, :TOP_K], cnt[0].astype(jnp.int32))
        xs_grouped = hx.at[tok_of_row].get(mode="promise_in_bounds")
        ys = _experts(tile_expert, n_tiles, xs_grouped, exp_w_gate, exp_w_up, exp_w_down, l, F32)
        pos_blocks = pos.reshape(T_ROWS // COMBINE_TM, COMBINE_TM, TOP_K).transpose(0, 2, 1).reshape(-1)

        ysh = _experts(jnp.zeros((T_ROWS // MOE_TM,), jnp.int32),
                       jnp.full((1,), T_ROWS // MOE_TM, jnp.int32),
                       hx, sh_w_gate[:, None], sh_w_up[:, None], sh_w_down[:, None], l, BF16)

        xs_all = _combine(ys, pos_blocks, wt, ysh, x1, mod3, ln2_g[l][None, :], ln2_b[l][None, :])

    return xs_all[:N_LAT].reshape(BATCH, SEQ, D_MODEL)
```

```python
import functools
import math

import numpy as np
import jax
import jax.numpy as jnp
from jax import lax
from jax.experimental import pallas as pl
from jax.experimental.pallas import tpu as pltpu

F32 = jnp.float32
BF16 = jnp.bfloat16

D_MODEL = 2048
BATCH = 2
SEQ = 4096
DEPTH = 4
CTX_LEN = 256
GRID_W = 64
HEAD_DIM = 128
A_WIDTH = 512
A_HEADS = 4
A_CHUNK = 128
B_WIDTH = 1024
B_HEADS = 4
B_QK_WIDTH = 1024
B_V_DIM = 256
ROPE_THETA = 10000.0
C_WIDTH = 512
C_HEADS = 4
IN_WIDTH = 6656
N_EXPERTS = 64
TOP_K = 8
EXPERT_DIM = 384
ROUTED_SCALE = 2.5
DEEPNORM_ALPHA = (2.0 * DEPTH) ** 0.25
LN_EPS = 1e-5
NORM_EPS = 1e-5

N_LAT = BATCH * SEQ
N_CTX = BATCH * CTX_LEN
T_ROWS = N_LAT + N_CTX
LOW_WIDTH = 4096
HI_WIDTH = IN_WIDTH - LOW_WIDTH
QK_SCALE = HEAD_DIM ** -0.5 * math.log2(math.e)

LANE = 128
PROJ_TM = 512
PROJ_TN = 512
ATT_TQ = 256
ATT_TK = 1024
SCAN_C = 128
SCAN_LEVELS = 7
OUT_TM = 256
MOE_TM = 256
COMBINE_TM = 128
MOE_ROWS = ((T_ROWS * TOP_K + N_EXPERTS * (MOE_TM - 1)) // MOE_TM) * MOE_TM
MOE_TILES = MOE_ROWS // MOE_TM
VMEM_LIMIT = 48 * 1024 * 1024


def _cparams(sem, vmem=VMEM_LIMIT):
    return pltpu.CompilerParams(dimension_semantics=sem, vmem_limit_bytes=vmem)


def _sigmoid(x):
    return 1.0 / (1.0 + jnp.exp(-x))


def _gelu_tanh(x):
    return 0.5 * x * (1.0 + jnp.tanh(math.sqrt(2.0 / math.pi) * (x + 0.044715 * (x * x * x))))


def _dot_nt(a, b):
    return lax.dot_general(a, b, (((1,), (1,)), ((), ())), preferred_element_type=F32)


def _dot_tn(a, b):
    return lax.dot_general(a, b, (((0,), (0,)), ((), ())), preferred_element_type=F32)


def _mod_kernel(c_ref, w_ref, b_ref, o_ref):
    c = c_ref[...]
    s = (c * _sigmoid(c)).astype(BF16)
    o_ref[0] = jnp.dot(s, w_ref[0].astype(BF16), preferred_element_type=F32) + b_ref[0]


def _modulation(cond, w_mod, b_mod):
    tn = 1024
    return pl.pallas_call(
        _mod_kernel,
        grid=(DEPTH, 6 * D_MODEL // tn),
        in_specs=[pl.BlockSpec((8, D_MODEL), lambda l, n: (0, 0)),
                  pl.BlockSpec((1, D_MODEL, tn), lambda l, n: (l, 0, n)),
                  pl.BlockSpec((1, 1, tn), lambda l, n: (l, 0, n))],
        out_specs=pl.BlockSpec((1, 8, tn), lambda l, n: (l, 0, n)),
        out_shape=jax.ShapeDtypeStruct((DEPTH, 8, 6 * D_MODEL), F32),
        compiler_params=_cparams(("parallel", "parallel")),
        name="modulation",
    )(cond, w_mod, b_mod.reshape(DEPTH, 1, 6 * D_MODEL))


def _mod_row(m, tm):
    return jnp.minimum(m // (SEQ // tm), BATCH)


def _inproj_kernel(x_ref, mod_ref, w_ref, cos_ref, sa_ref, sb_ref, o_ref, h_scr, *, rope):
    n = pl.program_id(1)

    @pl.when(n == 0)
    def _():
        sh = mod_ref[0, 0:1, :]
        sc = mod_ref[0, 1:2, :]
        h_scr[...] = (x_ref[...] * (1.0 + sc) + sh).astype(BF16)

    acc = jnp.dot(h_scr[...], w_ref[...], preferred_element_type=F32)
    if not rope:
        o_ref[...] = acc.astype(o_ref.dtype)
        return

    is_qk = (n >= 2) & (n < 6)

    @pl.when(is_qk)
    def _():
        scale = jnp.where(n < 4, QK_SCALE, 1.0).astype(F32)
        cos = cos_ref[...]
        sa = sa_ref[...]
        sb = sb_ref[...]
        for g in range(PROJ_TN // LANE):
            blk = acc[:, g * LANE:(g + 1) * LANE]
            r = blk * cos + pltpu.roll(blk, 32, 1) * sa + pltpu.roll(blk, 96, 1) * sb
            o_ref[:, g * LANE:(g + 1) * LANE] = (r * scale).astype(o_ref.dtype)

    @pl.when(jnp.logical_not(is_qk))
    def _():
        o_ref[...] = acc.astype(o_ref.dtype)


def _inproj(x, mod3, w, tables, out_dtype, rope):
    width = w.shape[1]
    cos, sa, sb = tables
    tab_spec = pl.BlockSpec((PROJ_TM, LANE), lambda m, n: (m, 0))
    return pl.pallas_call(
        functools.partial(_inproj_kernel, rope=rope),
        grid=(T_ROWS // PROJ_TM, width // PROJ_TN),
        in_specs=[pl.BlockSpec((PROJ_TM, D_MODEL), lambda m, n: (m, 0)),
                  pl.BlockSpec((1, 6, D_MODEL), lambda m, n: (_mod_row(m, PROJ_TM), 0, 0)),
                  pl.BlockSpec((D_MODEL, PROJ_TN), lambda m, n: (0, n)),
                  tab_spec, tab_spec, tab_spec],
        out_specs=pl.BlockSpec((PROJ_TM, PROJ_TN), lambda m, n: (m, n)),
        out_shape=jax.ShapeDtypeStruct((T_ROWS, width), out_dtype),
        scratch_shapes=[pltpu.VMEM((PROJ_TM, D_MODEL), BF16)],
        compiler_params=_cparams(("parallel", "arbitrary")),
        name="inproj_rope" if rope else "inproj",
    )(x, mod3, w, cos, sa, sb)


def _gmlp_kernel(u_ref, v_ref, g_ref, b_ref, ws_ref, bs_ref, o_ref):
    for h in range(A_HEADS):
        sl = slice(h * HEAD_DIM, (h + 1) * HEAD_DIM)
        v = _gelu_tanh(v_ref[:, sl].astype(F32))
        mu = jnp.mean(v, axis=-1, keepdims=True)
        var = jnp.mean(jnp.square(v - mu), axis=-1, keepdims=True)
        vn = (v - mu) * lax.rsqrt(var + LN_EPS) * g_ref[:, sl] + b_ref[:, sl]
        s = jnp.dot(ws_ref[h], vn.astype(BF16), preferred_element_type=F32) + bs_ref[:, h:h + 1]
        u = _gelu_tanh(u_ref[:, sl].astype(F32))
        o_ref[:, sl] = (u * s).astype(o_ref.dtype)


def _gmlp(p_lo, ln_g, ln_b, ws, bs_t):
    return pl.pallas_call(
        _gmlp_kernel,
        grid=(T_ROWS // A_CHUNK,),
        in_specs=[pl.BlockSpec((A_CHUNK, A_WIDTH), lambda i: (i, 0)),
                  pl.BlockSpec((A_CHUNK, A_WIDTH), lambda i: (i, 1)),
                  pl.BlockSpec((1, A_WIDTH), lambda i: (0, 0)),
                  pl.BlockSpec((1, A_WIDTH), lambda i: (0, 0)),
                  pl.BlockSpec((A_HEADS, A_CHUNK, A_CHUNK), lambda i: (0, 0, 0)),
                  pl.BlockSpec((A_CHUNK, A_HEADS), lambda i: (0, 0))],
        out_specs=pl.BlockSpec((A_CHUNK, A_WIDTH), lambda i: (i, 0)),
        out_shape=jax.ShapeDtypeStruct((T_ROWS, A_WIDTH), BF16),
        compiler_params=_cparams(("parallel",)),
        name="gmlp",
    )(p_lo, p_lo, ln_g, ln_b, ws, bs_t)


NQ_LAT = SEQ // ATT_TQ


def _attn_kernel(lam_ref, q_ref, kl_ref, vl_ref, kc_ref, vc_ref, g_ref, o_ref, m_scr, l_scr, acc_scr):
    qi = pl.program_id(2)
    m_scr[...] = jnp.full(m_scr.shape, -1e30, F32)
    l_scr[...] = jnp.zeros(l_scr.shape, F32)
    acc_scr[...] = jnp.zeros(acc_scr.shape, F32)

    def process(k, v):
        for mp in range(2):
            sl = slice(mp * HEAD_DIM, (mp + 1) * HEAD_DIM)
            s = _dot_nt(q_ref[:, sl], k[:, sl])
            cols = [s[:, c * LANE:(c + 1) * LANE] for c in range(k.shape[0] // LANE)]
            m_part = cols[0]
            for col in cols[1:]:
                m_part = jnp.maximum(m_part, col)
            m_old = m_scr[mp]
            m_new = jnp.maximum(m_old, jnp.max(m_part, axis=-1, keepdims=True))
            alpha = jnp.exp2(m_old - m_new)
            l_part = alpha * l_scr[mp]
            ps = []
            for col in cols:
                p = jnp.exp2(col - m_new)
                l_part = l_part + p
                ps.append(p.astype(BF16))
            pv = jnp.dot(jnp.concatenate(ps, axis=1), v, preferred_element_type=F32)
            for half in range(B_V_DIM // LANE):
                hs = slice(half * LANE, (half + 1) * LANE)
                acc_scr[mp, :, hs] = alpha * acc_scr[mp, :, hs] + pv[:, hs]
            l_scr[mp] = l_part
            m_scr[mp] = m_new

    process(kc_ref[...], vc_ref[...])

    @pl.when(qi < NQ_LAT)
    def _():
        for j in range(SEQ // ATT_TK):
            rows = slice(j * ATT_TK, (j + 1) * ATT_TK)
            process(kl_ref[rows, :], vl_ref[rows, :])

    lam = lam_ref[0]
    post = lam_ref[1]
    l0 = jnp.sum(l_scr[0], axis=-1, keepdims=True)
    l1 = jnp.sum(l_scr[1], axis=-1, keepdims=True)
    o = acc_scr[0] / l0 - lam * (acc_scr[1] / l1)
    r = o * lax.rsqrt(jnp.mean(jnp.square(o), axis=-1, keepdims=True) + NORM_EPS)
    o_ref[...] = (r * g_ref[...] * post).astype(o_ref.dtype)


def _diff_attention(p_lo, lam2, subln_g):
    def q_rows(b, h, qi):
        return jnp.where(qi < NQ_LAT, b * NQ_LAT + qi, BATCH * NQ_LAT + b)

    return pl.pallas_call(
        _attn_kernel,
        grid=(BATCH, B_HEADS, NQ_LAT + 1),
        in_specs=[pl.BlockSpec(memory_space=pltpu.SMEM),
                  pl.BlockSpec((ATT_TQ, B_V_DIM), lambda b, h, qi: (q_rows(b, h, qi), 4 + h)),
                  pl.BlockSpec((SEQ, B_V_DIM), lambda b, h, qi: (b, 8 + h)),
                  pl.BlockSpec((SEQ, B_V_DIM), lambda b, h, qi: (b, 12 + h)),
                  pl.BlockSpec((CTX_LEN, B_V_DIM), lambda b, h, qi: (N_LAT // CTX_LEN + b, 8 + h)),
                  pl.BlockSpec((CTX_LEN, B_V_DIM), lambda b, h, qi: (N_LAT // CTX_LEN + b, 12 + h)),
                  pl.BlockSpec((1, B_V_DIM), lambda b, h, qi: (0, 0))],
        out_specs=pl.BlockSpec((ATT_TQ, B_V_DIM), lambda b, h, qi: (q_rows(b, h, qi), h)),
        scratch_shapes=[pltpu.VMEM((2, ATT_TQ, LANE), F32),
                        pltpu.VMEM((2, ATT_TQ, LANE), F32),
                        pltpu.VMEM((2, ATT_TQ, B_V_DIM), F32)],
        out_shape=jax.ShapeDtypeStruct((T_ROWS, B_WIDTH), BF16),
        compiler_params=_cparams(("parallel", "parallel", "arbitrary")),
        name="diff_attention",
    )(lam2, p_lo, p_lo, p_lo, p_lo, p_lo, subln_g)


def _scan_structure():
    c = SCAN_C
    x = np.zeros((2, (2 + SCAN_LEVELS) * c, c), np.float32)
    msk = np.zeros((2, SCAN_LEVELS + 1, c, c), np.float32)
    r = np.arange(c)
    j = np.arange(c)[None, :]
    xf = np.zeros(((2 + SCAN_LEVELS) * c, c), np.float32)
    mf = np.zeros((SCAN_LEVELS + 1, c, c), np.float32)
    xf[0:c] = (j <= r[:, None])
    xf[c:2 * c] = (j > r[:, None])
    for lev in range(SCAN_LEVELS):
        half = (c // 2) >> lev
        start = (r // (2 * half)) * (2 * half)
        mid = start + half - 1
        later = r > mid
        rows = np.where(later[:, None], (j > mid[:, None]) & (j <= r[:, None]),
                        (j > r[:, None]) & (j <= mid[:, None]))
        xf[(2 + lev) * c:(3 + lev) * c] = rows
        same = start[:, None] == start[None, :]
        mf[lev] = same & later[:, None] & (~later)[None, :]
    mf[SCAN_LEVELS] = np.eye(c)
    x[0] = xf
    msk[0] = mf
    x[1] = xf.reshape(2 + SCAN_LEVELS, c, c)[:, ::-1, ::-1].reshape(-1, c)
    msk[1] = mf[:, ::-1, ::-1]
    return x, msk


def _hgrn_kernel(qf_ref, ff_ref, if_ref, qb_ref, fb_ref, ib_ref, lb_ref, x_ref, msk_ref,
                 of_ref, ob_ref, st_scr):
    j = pl.program_id(1)

    @pl.when(j == 0)
    def _():
        st_scr[...] = jnp.zeros(st_scr.shape, F32)

    c = SCAN_C
    dirs = ((qf_ref, ff_ref, if_ref, of_ref, c - 1), (qb_ref, fb_ref, ib_ref, ob_ref, 0))
    for d, (q_ref, f_ref, i_ref, o_ref, end_row) in enumerate(dirs):
        for h in range(C_HEADS):
            sl = slice(h * HEAD_DIM, (h + 1) * HEAD_DIM)
            qraw = q_ref[:, sl]
            q = qraw * _sigmoid(qraw)
            lb = lb_ref[d:d + 1, sl]
            f = lb + (1.0 - lb) * _sigmoid(f_ref[:, sl])
            logf = jnp.log(f)
            k = 1.0 - f
            v = i_ref[:, sl].astype(BF16)
            hi = logf.astype(BF16)
            lo = (logf - hi.astype(F32)).astype(BF16)
            e2 = jnp.dot(x_ref[d], jnp.concatenate([hi, lo], axis=1), preferred_element_type=F32)
            w = jnp.exp(e2[:, 0:c] + e2[:, c:2 * c])
            a = msk_ref[d, SCAN_LEVELS] * _dot_nt(q.astype(BF16), k.astype(BF16))
            for lev in range(SCAN_LEVELS):
                wl = w[(2 + lev) * c:(3 + lev) * c]
                a = a + msk_ref[d, lev] * _dot_nt((q * wl).astype(BF16), (k * wl).astype(BF16))
            st = st_scr[d, h]
            o = jnp.dot(a.astype(BF16), v, preferred_element_type=F32)
            o = o + _dot_nt((q * w[0:c]).astype(BF16), st.astype(BF16))
            o_ref[:, sl] = o
            g_end = w[end_row:end_row + 1, :]
            st_scr[d, h] = st * g_end + _dot_tn(v, (k * w[c:2 * c]).astype(BF16))


def _hgrn_scan(p_hi, lb2, xmat, masks):
    lat_chunks = SEQ // SCAN_C
    ctx_chunks = CTX_LEN // SCAN_C
    steps = ctx_chunks + lat_chunks

    def fwd_rows(b, j):
        return jnp.where(j < ctx_chunks, N_LAT // SCAN_C + b * ctx_chunks + j,
                         b * lat_chunks + j - ctx_chunks)

    def bwd_rows(b, j):
        return jnp.where(j < ctx_chunks, N_LAT // SCAN_C + b * ctx_chunks + (ctx_chunks - 1 - j),
                         b * lat_chunks + (steps - 1 - j))

    def spec(rows, col):
        return pl.BlockSpec((SCAN_C, C_WIDTH), lambda b, j: (rows(b, j), col))

    nx = (2 + SCAN_LEVELS) * SCAN_C
    return pl.pallas_call(
        _hgrn_kernel,
        grid=(BATCH, steps),
        in_specs=[spec(fwd_rows, 0), spec(fwd_rows, 1), spec(fwd_rows, 3),
                  spec(bwd_rows, 0), spec(bwd_rows, 2), spec(bwd_rows, 3),
                  pl.BlockSpec((2, C_WIDTH), lambda b, j: (0, 0)),
                  pl.BlockSpec((2, nx, SCAN_C), lambda b, j: (0, 0, 0)),
                  pl.BlockSpec((2, SCAN_LEVELS + 1, SCAN_C, SCAN_C), lambda b, j: (0, 0, 0, 0))],
        out_specs=[spec(fwd_rows, 0), spec(bwd_rows, 0)],
        out_shape=[jax.ShapeDtypeStruct((T_ROWS, C_WIDTH), F32)] * 2,
        scratch_shapes=[pltpu.VMEM((2, C_HEADS, HEAD_DIM, HEAD_DIM), F32)],
        compiler_params=_cparams(("parallel", "arbitrary")),
        name="hgrn2_scan",
    )(p_hi, p_hi, p_hi, p_hi, p_hi, p_hi, lb2, xmat, masks)


def _layer_norm_rows(z, g, b):
    mu = jnp.mean(z, axis=-1, keepdims=True)
    var = jnp.mean(jnp.square(z - mu), axis=-1, keepdims=True)
    return (z - mu) * lax.rsqrt(var + LN_EPS) * g + b


def _outproj_kernel(a_ref, b_ref, of_ref, ob_ref, cg_ref, hg_ref, x_ref, mod_ref, w_ref,
                    l1g_ref, l1b_ref, rw_ref, rb_ref,
                    x1_ref, hx_ref, idx_ref, wt_ref, rank_ref, cnt_ref, cnt_scr):
    parts = []
    for h in range(C_HEADS):
        sl = slice(h * HEAD_DIM, (h + 1) * HEAD_DIM)
        o = of_ref[:, sl] + ob_ref[:, sl]
        on = o * lax.rsqrt(jnp.mean(jnp.square(o), axis=-1, keepdims=True) + NORM_EPS) * hg_ref[...]
        g = cg_ref[:, sl]
        parts.append((on * (g * _sigmoid(g))).astype(BF16))
    c_x = jnp.concatenate(parts, axis=1)
    mix = jnp.dot(a_ref[...], w_ref[0:A_WIDTH, :], preferred_element_type=F32)
    mix = mix + jnp.dot(b_ref[...], w_ref[A_WIDTH:A_WIDTH + B_WIDTH, :], preferred_element_type=F32)
    mix = mix + jnp.dot(c_x, w_ref[A_WIDTH + B_WIDTH:, :], preferred_element_type=F32)
    g1 = mod_ref[0, 2:3, :]
    sh2 = mod_ref[0, 3:4, :]
    sc2 = mod_ref[0, 4:5, :]
    x1 = _layer_norm_rows(DEEPNORM_ALPHA * x_ref[...] + g1 * mix, l1g_ref[...], l1b_ref[...])
    x1_ref[...] = x1
    hx = x1 * (1.0 + sc2) + sh2
    hx_ref[...] = hx.astype(BF16)

    logits = jnp.dot(hx.astype(BF16), rw_ref[...], preferred_element_type=F32)
    scores = _sigmoid(logits)
    sel = scores + rb_ref[...]
    tm = sel.shape[0]
    lane = lax.broadcasted_iota(jnp.int32, sel.shape, 1).astype(F32)
    slot = lax.broadcasted_iota(jnp.int32, (tm, LANE), 1)
    idx_acc = jnp.zeros((tm, LANE), F32)
    wt_acc = jnp.zeros((tm, LANE), F32)
    chosen = jnp.zeros(sel.shape, F32)
    hits = []
    for k in range(TOP_K):
        mx = jnp.max(sel, axis=-1, keepdims=True)
        idx = jnp.min(jnp.where(sel == mx, lane, float(N_EXPERTS)), axis=-1, keepdims=True)
        hit = lane == idx
        hits.append(hit)
        chosen = jnp.where(hit, 1.0, chosen)
        w_k = jnp.sum(jnp.where(hit, scores, 0.0), axis=-1, keepdims=True)
        idx_acc = jnp.where(slot == k, idx, idx_acc)
        wt_acc = jnp.where(slot == k, w_k, wt_acc)
        sel = jnp.where(hit, -jnp.inf, sel)
    idx_ref[...] = idx_acc.astype(jnp.int32)
    wt_ref[...] = wt_acc / jnp.sum(wt_acc, axis=-1, keepdims=True) * ROUTED_SCALE

    @pl.when(pl.program_id(0) == 0)
    def _():
        cnt_scr[...] = jnp.zeros(cnt_scr.shape, F32)

    below = (lax.broadcasted_iota(jnp.int32, (tm, tm), 0) > lax.broadcasted_iota(jnp.int32, (tm, tm), 1))
    before = jnp.dot(below.astype(BF16), chosen.astype(BF16), preferred_element_type=F32) + cnt_scr[0:1, :]
    rank_acc = jnp.zeros((tm, LANE), F32)
    for k in range(TOP_K):
        r_k = jnp.sum(jnp.where(hits[k], before, 0.0), axis=-1, keepdims=True)
        rank_acc = jnp.where(slot == k, r_k, rank_acc)
    rank_ref[...] = rank_acc.astype(jnp.int32)
    cnt = cnt_scr[...] + jnp.sum(chosen, axis=0, keepdims=True)
    cnt_scr[...] = cnt
    cnt_ref[...] = cnt


def _outproj(a_x, b_x, o_f, o_b, p_hi, hnorm_g, x, mod3, w_out, ln_g, ln_b, router_w, router_b):
    row = lambda width: pl.BlockSpec((OUT_TM, width), lambda i: (i, 0))
    const = lambda shape: pl.BlockSpec(shape, lambda i: tuple(0 for _ in shape))
    return pl.pallas_call(
        _outproj_kernel,
        grid=(T_ROWS // OUT_TM,),
        in_specs=[row(A_WIDTH), row(B_WIDTH), row(C_WIDTH), row(C_WIDTH),
                  pl.BlockSpec((OUT_TM, C_WIDTH), lambda i: (i, 4)),
                  const((1, HEAD_DIM)),
                  row(D_MODEL),
                  pl.BlockSpec((1, 6, D_MODEL), lambda i: (_mod_row(i, OUT_TM), 0, 0)),
                  const((D_MODEL, D_MODEL)),
                  const((1, D_MODEL)), const((1, D_MODEL)),
                  const((D_MODEL, N_EXPERTS)), const((1, N_EXPERTS))],
        out_specs=[row(D_MODEL), row(D_MODEL), row(LANE), row(LANE), row(LANE), const((8, N_EXPERTS))],
        out_shape=[jax.ShapeDtypeStruct((T_ROWS, D_MODEL), F32),
                   jax.ShapeDtypeStruct((T_ROWS, D_MODEL), BF16),
                   jax.ShapeDtypeStruct((T_ROWS, LANE), jnp.int32),
                   jax.ShapeDtypeStruct((T_ROWS, LANE), F32),
                   jax.ShapeDtypeStruct((T_ROWS, LANE), jnp.int32),
                   jax.ShapeDtypeStruct((8, N_EXPERTS), F32)],
        scratch_shapes=[pltpu.VMEM((8, N_EXPERTS), F32)],
        compiler_params=_cparams(("arbitrary",)),
        name="outproj_ln_router",
    )(a_x, b_x, o_f, o_b, p_hi, hnorm_g, x, mod3, w_out, ln_g, ln_b, router_w, router_b)


def _experts_kernel(te_ref, nt_ref, xs_ref, wg_ref, wu_ref, wd_ref, ys_ref, wgu_scr, wd_scr):
    i = pl.program_id(0)
    live = i < nt_ref[0]
    new_expert = jnp.logical_or(i == 0, te_ref[i] != te_ref[jnp.maximum(i - 1, 0)])

    @pl.when(jnp.logical_and(live, new_expert))
    def _():
        wgu_scr[:, :EXPERT_DIM] = wg_ref[0, 0].astype(BF16)
        wgu_scr[:, EXPERT_DIM:] = wu_ref[0, 0].astype(BF16)
        wd_scr[...] = wd_ref[0, 0].astype(BF16)

    @pl.when(live)
    def _():
        h = jnp.dot(xs_ref[...], wgu_scr[...], preferred_element_type=F32)
        g = h[:, :EXPERT_DIM]
        u = h[:, EXPERT_DIM:]
        a = (g * _sigmoid(g) * u).astype(BF16)
        ys_ref[...] = jnp.dot(a, wd_scr[...], preferred_element_type=F32).astype(ys_ref.dtype)

    @pl.when(jnp.logical_not(live))
    def _():
        ys_ref[...] = jnp.zeros(ys_ref.shape, ys_ref.dtype)


def _experts(tile_expert, n_tiles, xs, w_gate, w_up, w_down, layer, out_dtype):
    tiles = xs.shape[0] // MOE_TM
    last = lambda i, nt: jnp.minimum(i, nt[0] - 1)
    grid_spec = pltpu.PrefetchScalarGridSpec(
        num_scalar_prefetch=2,
        grid=(tiles,),
        in_specs=[pl.BlockSpec((MOE_TM, D_MODEL), lambda i, te, nt: (last(i, nt), 0)),
                  pl.BlockSpec((1, 1, D_MODEL, EXPERT_DIM), lambda i, te, nt: (layer, te[i], 0, 0)),
                  pl.BlockSpec((1, 1, D_MODEL, EXPERT_DIM), lambda i, te, nt: (layer, te[i], 0, 0)),
                  pl.BlockSpec((1, 1, EXPERT_DIM, D_MODEL), lambda i, te, nt: (layer, te[i], 0, 0))],
        out_specs=pl.BlockSpec((MOE_TM, D_MODEL), lambda i, te, nt: (i, 0)),
        scratch_shapes=[pltpu.VMEM((D_MODEL, 2 * EXPERT_DIM), BF16),
                        pltpu.VMEM((EXPERT_DIM, D_MODEL), BF16)],
    )
    return pl.pallas_call(
        _experts_kernel,
        grid_spec=grid_spec,
        out_shape=jax.ShapeDtypeStruct((xs.shape[0], D_MODEL), out_dtype),
        compiler_params=_cparams(("arbitrary",)),
        name="experts",
    )(tile_expert, n_tiles, xs, w_gate, w_up, w_down)


def _combine_kernel(pos_ref, wt_ref, ysh_ref, x1_ref, mod_ref, g_ref, b_ref, ys_hbm, o_ref, buf, sem):
    tm = COMBINE_TM
    n_rows = TOP_K * tm

    def row_copy(j):
        return pltpu.make_async_copy(ys_hbm.at[pl.ds(pos_ref[j], 1), :], buf.at[pl.ds(j, 1), :], sem)

    def issue(j, carry):
        row_copy(j).start()
        return carry

    lax.fori_loop(0, n_rows, issue, 0, unroll=8)
    pltpu.make_async_copy(ys_hbm.at[pl.ds(0, n_rows), :], buf, sem).wait()

    y = ysh_ref[...].astype(F32)
    for k in range(TOP_K):
        y = y + wt_ref[:, k:k + 1] * buf[k * tm:(k + 1) * tm, :]
    g2 = mod_ref[0, 5:6, :]
    o_ref[...] = _layer_norm_rows(DEEPNORM_ALPHA * x1_ref[...] + g2 * y, g_ref[...], b_ref[...])


def _combine(ys, pos_blocks, wt, ysh, x1, mod3, ln_g, ln_b):
    tm = COMBINE_TM
    row = lambda width: pl.BlockSpec((tm, width), lambda i: (i, 0))
    return pl.pallas_call(
        _combine_kernel,
        grid=(T_ROWS // tm,),
        in_specs=[pl.BlockSpec((TOP_K * tm,), lambda i: (i,), memory_space=pltpu.SMEM),
                  row(LANE), row(D_MODEL), row(D_MODEL),
                  pl.BlockSpec((1, 6, D_MODEL), lambda i: (_mod_row(i, tm), 0, 0)),
                  pl.BlockSpec((1, D_MODEL), lambda i: (0, 0)),
                  pl.BlockSpec((1, D_MODEL), lambda i: (0, 0)),
                  pl.BlockSpec(memory_space=pl.ANY)],
        out_specs=row(D_MODEL),
        out_shape=jax.ShapeDtypeStruct((T_ROWS, D_MODEL), F32),
        scratch_shapes=[pltpu.VMEM((TOP_K * tm, D_MODEL), F32), pltpu.SemaphoreType.DMA(())],
        compiler_params=_cparams(("arbitrary",)),
        name="moe_combine_ln",
    )(pos_blocks, wt, ysh, x1, mod3, ln_g, ln_b, ys)


def _dispatch(idx, rank, counts):
    tiles_per = (counts + MOE_TM - 1) // MOE_TM
    tile_end = jnp.cumsum(tiles_per)
    start = (tile_end - tiles_per) * MOE_TM
    experts = jnp.arange(N_EXPERTS, dtype=jnp.int32)
    pos = rank + jnp.sum(jnp.where(idx[:, :, None] == experts, start, 0), axis=-1)
    tok = jnp.broadcast_to(jnp.arange(T_ROWS, dtype=jnp.int32)[:, None], idx.shape)
    tok_of_row = jnp.zeros((MOE_ROWS,), jnp.int32).at[pos.reshape(-1)].set(
        tok.reshape(-1), unique_indices=True)
    n_tiles = tile_end[-1:].astype(jnp.int32)
    tile = jnp.arange(MOE_TILES, dtype=jnp.int32)
    tile_expert = jnp.sum((tile_end[None, :] <= tile[:, None]).astype(jnp.int32), axis=-1)
    tile_expert = jnp.minimum(tile_expert, N_EXPERTS - 1)
    return tok_of_row, pos, tile_expert, n_tiles


def _rope_tables():
    t = jnp.arange(SEQ)
    row = (t // GRID_W).astype(F32)
    col = (t % GRID_W).astype(F32)
    axis_dim = HEAD_DIM // 2
    inv = ROPE_THETA ** (-jnp.arange(0, axis_dim, 2, dtype=F32) / axis_dim)
    ar = row[:, None] * inv
    ac = col[:, None] * inv
    cos = jnp.concatenate([jnp.cos(ar), jnp.cos(ar), jnp.cos(ac), jnp.cos(ac)], axis=-1)
    sin = jnp.concatenate([jnp.sin(ar), jnp.sin(ar), jnp.sin(ac), jnp.sin(ac)], axis=-1)
    odd = ((jnp.arange(HEAD_DIM) // 32) % 2 == 1)[None, :]
    sa = jnp.where(odd, sin, 0.0)
    sb = jnp.where(odd, 0.0, -sin)
    lat = lambda z: jnp.tile(z, (BATCH, 1))
    cos_all = jnp.concatenate([lat(cos), jnp.ones((N_CTX, HEAD_DIM), F32)], axis=0)
    sa_all = jnp.concatenate([lat(sa), jnp.zeros((N_CTX, HEAD_DIM), F32)], axis=0)
    sb_all = jnp.concatenate([lat(sb), jnp.zeros((N_CTX, HEAD_DIM), F32)], axis=0)
    return cos_all, sa_all, sb_all


def kernel(x, c, ctx, c_ctx, w_mod, b_mod, w_in, w_out, gmlp_ln_g, gmlp_ln_b, gmlp_ws, gmlp_bs,
           diff_lam, diff_subln_g, hgrn_lb, hgrn_norm_g, ln1_g, ln1_b, ln2_g, ln2_b,
           router_w, router_b, exp_w_gate, exp_w_up, exp_w_down, sh_w_gate, sh_w_up, sh_w_down):
    assert x.shape == (BATCH, SEQ, D_MODEL) and ctx.shape == (BATCH, CTX_LEN, D_MODEL)
    xs_all = jnp.concatenate([x.reshape(N_LAT, D_MODEL), ctx.reshape(N_CTX, D_MODEL)], axis=0)

    cond = jnp.zeros((8, D_MODEL), F32).at[0:BATCH].set(c).at[BATCH].set(c_ctx)
    mod_all = _modulation(cond, w_mod, b_mod)[:, :BATCH + 1].reshape(DEPTH, BATCH + 1, 6, D_MODEL)

    sm = jax.nn.softmax(hgrn_lb.astype(F32), axis=0)
    lb_all = jnp.cumsum(sm, axis=0) - sm[0]
    tables = _rope_tables()
    xmat_np, masks_np = _scan_structure()
    xmat = jnp.asarray(xmat_np, BF16)
    masks = jnp.asarray(masks_np, F32)

    for l in range(DEPTH):
        mod3 = mod_all[l]
        w_in_l = w_in[l].astype(BF16)
        p_lo = _inproj(xs_all, mod3, w_in_l[:, :LOW_WIDTH], tables, BF16, True)
        p_hi = _inproj(xs_all, mod3, w_in_l[:, LOW_WIDTH:], tables, F32, False)

        a_x = _gmlp(p_lo, gmlp_ln_g[l][None, :], gmlp_ln_b[l][None, :],
                    gmlp_ws[l].astype(BF16), gmlp_bs[l].T)

        lam_init = 0.8 - 0.6 * math.exp(-0.3 * l)
        dl = diff_lam[l].astype(F32)
        lam = jnp.exp(jnp.sum(dl[0] * dl[1])) - jnp.exp(jnp.sum(dl[2] * dl[3])) + lam_init
        lam2 = jnp.stack([lam, jnp.asarray(1.0 - lam_init, F32)]).astype(F32)
        b_x = _diff_attention(p_lo, lam2, diff_subln_g[l][None, :])

        o_f, o_b = _hgrn_scan(p_hi, lb_all[l], xmat, masks)

        x1, hx, idx, wt, rank, cnt = _outproj(
            a_x, b_x, o_f, o_b, p_hi, hgrn_norm_g[l][None, :], xs_all, mod3,
            w_out[l].astype(BF16), ln1_g[l][None, :], ln1_b[l][None, :],
            router_w[l].astype(BF16), router_b[l][None, :])

        tok_of_row, pos, tile_expert, n_tiles = _dispatch(
            idx[:, :TOP_K], rank[:, :TOP_K], cnt[0].astype(jnp.int32))
        xs_grouped = jnp.take(hx, tok_of_row, axis=0)
        ys = _experts(tile_expert, n_tiles, xs_grouped, exp_w_gate, exp_w_up, exp_w_down, l, F32)
        pos_blocks = pos.reshape(T_ROWS // COMBINE_TM, COMBINE_TM, TOP_K).transpose(0, 2, 1).reshape(-1)

        ysh = _experts(jnp.zeros((T_ROWS // MOE_TM,), jnp.int32),
                       jnp.full((1,), T_ROWS // MOE_TM, jnp.int32),
                       hx, sh_w_gate[:, None], sh_w_up[:, None], sh_w_down[:, None], l, BF16)

        xs_all = _combine(ys, pos_blocks, wt, ysh, x1, mod3, ln2_g[l][None, :], ln2_b[l][None, :])

    return xs_all[:N_LAT].reshape(BATCH, SEQ, D_MODEL)
```

```python
import functools
import math

import numpy as np
import jax
import jax.numpy as jnp
from jax import lax
from jax.experimental import pallas as pl
from jax.experimental.pallas import tpu as pltpu

F32 = jnp.float32
BF16 = jnp.bfloat16

D_MODEL = 2048
BATCH = 2
SEQ = 4096
DEPTH = 4
CTX_LEN = 256
GRID_W = 64
HEAD_DIM = 128
A_WIDTH = 512
A_HEADS = 4
A_CHUNK = 128
B_WIDTH = 1024
B_HEADS = 4
B_QK_WIDTH = 1024
B_V_DIM = 256
ROPE_THETA = 10000.0
C_WIDTH = 512
C_HEADS = 4
IN_WIDTH = 6656
N_EXPERTS = 64
TOP_K = 8
EXPERT_DIM = 384
ROUTED_SCALE = 2.5
DEEPNORM_ALPHA = (2.0 * DEPTH) ** 0.25
LN_EPS = 1e-5
NORM_EPS = 1e-5

N_LAT = BATCH * SEQ
N_CTX = BATCH * CTX_LEN
T_ROWS = N_LAT + N_CTX
LOW_WIDTH = 4096
HI_WIDTH = IN_WIDTH - LOW_WIDTH
QK_SCALE = HEAD_DIM ** -0.5 * math.log2(math.e)

LANE = 128
PROJ_TM = 512
PROJ_TN = 512
ATT_TQ = 256
ATT_TK = 1024
SCAN_C = 128
SCAN_LEVELS = 7
OUT_TM = 256
MOE_TM = 256
COMBINE_TM = 128
DISPATCH_TM = 256
MOE_ROWS = ((T_ROWS * TOP_K + N_EXPERTS * (MOE_TM - 1)) // MOE_TM) * MOE_TM
MOE_TILES = MOE_ROWS // MOE_TM
VMEM_LIMIT = 48 * 1024 * 1024


def _cparams(sem, vmem=VMEM_LIMIT):
    return pltpu.CompilerParams(dimension_semantics=sem, vmem_limit_bytes=vmem)


def _sigmoid(x):
    return 1.0 / (1.0 + jnp.exp(-x))


def _gelu_tanh(x):
    return 0.5 * x * (1.0 + jnp.tanh(math.sqrt(2.0 / math.pi) * (x + 0.044715 * (x * x * x))))


def _dot_nt(a, b):
    return lax.dot_general(a, b, (((1,), (1,)), ((), ())), preferred_element_type=F32)


def _dot_tn(a, b):
    return lax.dot_general(a, b, (((0,), (0,)), ((), ())), preferred_element_type=F32)


def _mod_kernel(c_ref, w_ref, b_ref, o_ref):
    c = c_ref[...]
    s = (c * _sigmoid(c)).astype(BF16)
    o_ref[0] = jnp.dot(s, w_ref[0].astype(BF16), preferred_element_type=F32) + b_ref[0]


def _modulation(cond, w_mod, b_mod):
    tn = 1024
    return pl.pallas_call(
        _mod_kernel,
        grid=(DEPTH, 6 * D_MODEL // tn),
        in_specs=[pl.BlockSpec((8, D_MODEL), lambda l, n: (0, 0)),
                  pl.BlockSpec((1, D_MODEL, tn), lambda l, n: (l, 0, n)),
                  pl.BlockSpec((1, 1, tn), lambda l, n: (l, 0, n))],
        out_specs=pl.BlockSpec((1, 8, tn), lambda l, n: (l, 0, n)),
        out_shape=jax.ShapeDtypeStruct((DEPTH, 8, 6 * D_MODEL), F32),
        compiler_params=_cparams(("parallel", "parallel")),
        name="modulation",
    )(cond, w_mod, b_mod.reshape(DEPTH, 1, 6 * D_MODEL))


def _mod_row(m, tm):
    return jnp.minimum(m // (SEQ // tm), BATCH)


def _inproj_kernel(x_ref, mod_ref, w_ref, cos_ref, sa_ref, sb_ref, o_ref, h_scr, *, rope):
    n = pl.program_id(1)

    @pl.when(n == 0)
    def _():
        sh = mod_ref[0, 0:1, :]
        sc = mod_ref[0, 1:2, :]
        h_scr[...] = (x_ref[...] * (1.0 + sc) + sh).astype(BF16)

    acc = jnp.dot(h_scr[...], w_ref[...], preferred_element_type=F32)
    if not rope:
        o_ref[...] = acc.astype(o_ref.dtype)
        return

    is_qk = (n >= 2) & (n < 6)

    @pl.when(is_qk)
    def _():
        scale = jnp.where(n < 4, QK_SCALE, 1.0).astype(F32)
        cos = cos_ref[...]
        sa = sa_ref[...]
        sb = sb_ref[...]
        for g in range(PROJ_TN // LANE):
            blk = acc[:, g * LANE:(g + 1) * LANE]
            r = blk * cos + pltpu.roll(blk, 32, 1) * sa + pltpu.roll(blk, 96, 1) * sb
            o_ref[:, g * LANE:(g + 1) * LANE] = (r * scale).astype(o_ref.dtype)

    @pl.when(jnp.logical_not(is_qk))
    def _():
        o_ref[...] = acc.astype(o_ref.dtype)


def _inproj(x, mod3, w, tables, out_dtype, rope):
    width = w.shape[1]
    cos, sa, sb = tables
    tab_spec = pl.BlockSpec((PROJ_TM, LANE), lambda m, n: (m, 0))
    return pl.pallas_call(
        functools.partial(_inproj_kernel, rope=rope),
        grid=(T_ROWS // PROJ_TM, width // PROJ_TN),
        in_specs=[pl.BlockSpec((PROJ_TM, D_MODEL), lambda m, n: (m, 0)),
                  pl.BlockSpec((1, 6, D_MODEL), lambda m, n: (_mod_row(m, PROJ_TM), 0, 0)),
                  pl.BlockSpec((D_MODEL, PROJ_TN), lambda m, n: (0, n)),
                  tab_spec, tab_spec, tab_spec],
        out_specs=pl.BlockSpec((PROJ_TM, PROJ_TN), lambda m, n: (m, n)),
        out_shape=jax.ShapeDtypeStruct((T_ROWS, width), out_dtype),
        scratch_shapes=[pltpu.VMEM((PROJ_TM, D_MODEL), BF16)],
        compiler_params=_cparams(("parallel", "arbitrary")),
        name="inproj_rope" if rope else "inproj",
    )(x, mod3, w, cos, sa, sb)


def _gmlp_kernel(u_ref, v_ref, g_ref, b_ref, ws_ref, bs_ref, o_ref):
    for h in range(A_HEADS):
        sl = slice(h * HEAD_DIM, (h + 1) * HEAD_DIM)
        v = _gelu_tanh(v_ref[:, sl].astype(F32))
        mu = jnp.mean(v, axis=-1, keepdims=True)
        var = jnp.mean(jnp.square(v - mu), axis=-1, keepdims=True)
        vn = (v - mu) * lax.rsqrt(var + LN_EPS) * g_ref[:, sl] + b_ref[:, sl]
        s = jnp.dot(ws_ref[h], vn.astype(BF16), preferred_element_type=F32) + bs_ref[:, h:h + 1]
        u = _gelu_tanh(u_ref[:, sl].astype(F32))
        o_ref[:, sl] = (u * s).astype(o_ref.dtype)


def _gmlp(p_lo, ln_g, ln_b, ws, bs_t):
    return pl.pallas_call(
        _gmlp_kernel,
        grid=(T_ROWS // A_CHUNK,),
        in_specs=[pl.BlockSpec((A_CHUNK, A_WIDTH), lambda i: (i, 0)),
                  pl.BlockSpec((A_CHUNK, A_WIDTH), lambda i: (i, 1)),
                  pl.BlockSpec((1, A_WIDTH), lambda i: (0, 0)),
                  pl.BlockSpec((1, A_WIDTH), lambda i: (0, 0)),
                  pl.BlockSpec((A_HEADS, A_CHUNK, A_CHUNK), lambda i: (0, 0, 0)),
                  pl.BlockSpec((A_CHUNK, A_HEADS), lambda i: (0, 0))],
        out_specs=pl.BlockSpec((A_CHUNK, A_WIDTH), lambda i: (i, 0)),
        out_shape=jax.ShapeDtypeStruct((T_ROWS, A_WIDTH), BF16),
        compiler_params=_cparams(("parallel",)),
        name="gmlp",
    )(p_lo, p_lo, ln_g, ln_b, ws, bs_t)


NQ_LAT = SEQ // ATT_TQ


def _attn_kernel(lam_ref, q_ref, kl_ref, vl_ref, kc_ref, vc_ref, g_ref, o_ref, m_scr, l_scr, acc_scr):
    qi = pl.program_id(2)
    m_scr[...] = jnp.full(m_scr.shape, -1e30, F32)
    l_scr[...] = jnp.zeros(l_scr.shape, F32)
    acc_scr[...] = jnp.zeros(acc_scr.shape, F32)

    def process(k, v):
        for mp in range(2):
            sl = slice(mp * HEAD_DIM, (mp + 1) * HEAD_DIM)
            s = _dot_nt(q_ref[:, sl], k[:, sl])
            cols = [s[:, c * LANE:(c + 1) * LANE] for c in range(k.shape[0] // LANE)]
            m_part = cols[0]
            for col in cols[1:]:
                m_part = jnp.maximum(m_part, col)
            m_old = m_scr[mp]
            m_new = jnp.maximum(m_old, jnp.max(m_part, axis=-1, keepdims=True))
            alpha = jnp.exp2(m_old - m_new)
            l_part = alpha * l_scr[mp]
            ps = []
            for col in cols:
                p = jnp.exp2(col - m_new)
                l_part = l_part + p
                ps.append(p.astype(BF16))
            pv = jnp.dot(jnp.concatenate(ps, axis=1), v, preferred_element_type=F32)
            for half in range(B_V_DIM // LANE):
                hs = slice(half * LANE, (half + 1) * LANE)
                acc_scr[mp, :, hs] = alpha * acc_scr[mp, :, hs] + pv[:, hs]
            l_scr[mp] = l_part
            m_scr[mp] = m_new

    process(kc_ref[...], vc_ref[...])

    @pl.when(qi < NQ_LAT)
    def _():
        for j in range(SEQ // ATT_TK):
            rows = slice(j * ATT_TK, (j + 1) * ATT_TK)
            process(kl_ref[rows, :], vl_ref[rows, :])

    lam = lam_ref[0]
    post = lam_ref[1]
    l0 = jnp.sum(l_scr[0], axis=-1, keepdims=True)
    l1 = jnp.sum(l_scr[1], axis=-1, keepdims=True)
    o = acc_scr[0] / l0 - lam * (acc_scr[1] / l1)
    r = o * lax.rsqrt(jnp.mean(jnp.square(o), axis=-1, keepdims=True) + NORM_EPS)
    o_ref[...] = (r * g_ref[...] * post).astype(o_ref.dtype)


def _diff_attention(p_lo, lam2, subln_g):
    def q_rows(b, h, qi):
        return jnp.where(qi < NQ_LAT, b * NQ_LAT + qi, BATCH * NQ_LAT + b)

    return pl.pallas_call(
        _attn_kernel,
        grid=(BATCH, B_HEADS, NQ_LAT + 1),
        in_specs=[pl.BlockSpec(memory_space=pltpu.SMEM),
                  pl.BlockSpec((ATT_TQ, B_V_DIM), lambda b, h, qi: (q_rows(b, h, qi), 4 + h)),
                  pl.BlockSpec((SEQ, B_V_DIM), lambda b, h, qi: (b, 8 + h)),
                  pl.BlockSpec((SEQ, B_V_DIM), lambda b, h, qi: (b, 12 + h)),
                  pl.BlockSpec((CTX_LEN, B_V_DIM), lambda b, h, qi: (N_LAT // CTX_LEN + b, 8 + h)),
                  pl.BlockSpec((CTX_LEN, B_V_DIM), lambda b, h, qi: (N_LAT // CTX_LEN + b, 12 + h)),
                  pl.BlockSpec((1, B_V_DIM), lambda b, h, qi: (0, 0))],
        out_specs=pl.BlockSpec((ATT_TQ, B_V_DIM), lambda b, h, qi: (q_rows(b, h, qi), h)),
        scratch_shapes=[pltpu.VMEM((2, ATT_TQ, LANE), F32),
                        pltpu.VMEM((2, ATT_TQ, LANE), F32),
                        pltpu.VMEM((2, ATT_TQ, B_V_DIM), F32)],
        out_shape=jax.ShapeDtypeStruct((T_ROWS, B_WIDTH), BF16),
        compiler_params=_cparams(("parallel", "parallel", "arbitrary")),
        name="diff_attention",
    )(lam2, p_lo, p_lo, p_lo, p_lo, p_lo, subln_g)


def _scan_structure():
    c = SCAN_C
    x = np.zeros((2, (2 + SCAN_LEVELS) * c, c), np.float32)
    msk = np.zeros((2, SCAN_LEVELS + 1, c, c), np.float32)
    r = np.arange(c)
    j = np.arange(c)[None, :]
    xf = np.zeros(((2 + SCAN_LEVELS) * c, c), np.float32)
    mf = np.zeros((SCAN_LEVELS + 1, c, c), np.float32)
    xf[0:c] = (j <= r[:, None])
    xf[c:2 * c] = (j > r[:, None])
    for lev in range(SCAN_LEVELS):
        half = (c // 2) >> lev
        start = (r // (2 * half)) * (2 * half)
        mid = start + half - 1
        later = r > mid
        rows = np.where(later[:, None], (j > mid[:, None]) & (j <= r[:, None]),
                        (j > r[:, None]) & (j <= mid[:, None]))
        xf[(2 + lev) * c:(3 + lev) * c] = rows
        same = start[:, None] == start[None, :]
        mf[lev] = same & later[:, None] & (~later)[None, :]
    mf[SCAN_LEVELS] = np.eye(c)
    x[0] = xf
    msk[0] = mf
    x[1] = xf.reshape(2 + SCAN_LEVELS, c, c)[:, ::-1, ::-1].reshape(-1, c)
    msk[1] = mf[:, ::-1, ::-1]
    return x, msk


def _hgrn_kernel(qf_ref, ff_ref, if_ref, qb_ref, fb_ref, ib_ref, lb_ref, x_ref, msk_ref,
                 of_ref, ob_ref, st_scr):
    j = pl.program_id(1)

    @pl.when(j == 0)
    def _():
        st_scr[...] = jnp.zeros(st_scr.shape, F32)

    c = SCAN_C
    dirs = ((qf_ref, ff_ref, if_ref, of_ref, c - 1), (qb_ref, fb_ref, ib_ref, ob_ref, 0))
    for d, (q_ref, f_ref, i_ref, o_ref, end_row) in enumerate(dirs):
        for h in range(C_HEADS):
            sl = slice(h * HEAD_DIM, (h + 1) * HEAD_DIM)
            qraw = q_ref[:, sl]
            q = qraw * _sigmoid(qraw)
            lb = lb_ref[d:d + 1, sl]
            f = lb + (1.0 - lb) * _sigmoid(f_ref[:, sl])
            logf = jnp.log(f)
            k = 1.0 - f
            v = i_ref[:, sl].astype(BF16)
            hi = logf.astype(BF16)
            lo = (logf - hi.astype(F32)).astype(BF16)
            e2 = jnp.dot(x_ref[d], jnp.concatenate([hi, lo], axis=1), preferred_element_type=F32)
            w = jnp.exp(e2[:, 0:c] + e2[:, c:2 * c])
            a = msk_ref[d, SCAN_LEVELS] * _dot_nt(q.astype(BF16), k.astype(BF16))
            for lev in range(SCAN_LEVELS):
                wl = w[(2 + lev) * c:(3 + lev) * c]
                a = a + msk_ref[d, lev] * _dot_nt((q * wl).astype(BF16), (k * wl).astype(BF16))
            st = st_scr[d, h]
            o = jnp.dot(a.astype(BF16), v, preferred_element_type=F32)
            o = o + _dot_nt((q * w[0:c]).astype(BF16), st.astype(BF16))
            o_ref[:, sl] = o
            g_end = w[end_row:end_row + 1, :]
            st_scr[d, h] = st * g_end + _dot_tn(v, (k * w[c:2 * c]).astype(BF16))


def _hgrn_scan(p_hi, lb2, xmat, masks):
    lat_chunks = SEQ // SCAN_C
    ctx_chunks = CTX_LEN // SCAN_C
    steps = ctx_chunks + lat_chunks

    def fwd_rows(b, j):
        return jnp.where(j < ctx_chunks, N_LAT // SCAN_C + b * ctx_chunks + j,
                         b * lat_chunks + j - ctx_chunks)

    def bwd_rows(b, j):
        return jnp.where(j < ctx_chunks, N_LAT // SCAN_C + b * ctx_chunks + (ctx_chunks - 1 - j),
                         b * lat_chunks + (steps - 1 - j))

    def spec(rows, col):
        return pl.BlockSpec((SCAN_C, C_WIDTH), lambda b, j: (rows(b, j), col))

    nx = (2 + SCAN_LEVELS) * SCAN_C
    return pl.pallas_call(
        _hgrn_kernel,
        grid=(BATCH, steps),
        in_specs=[spec(fwd_rows, 0), spec(fwd_rows, 1), spec(fwd_rows, 3),
                  spec(bwd_rows, 0), spec(bwd_rows, 2), spec(bwd_rows, 3),
                  pl.BlockSpec((2, C_WIDTH), lambda b, j: (0, 0)),
                  pl.BlockSpec((2, nx, SCAN_C), lambda b, j: (0, 0, 0)),
                  pl.BlockSpec((2, SCAN_LEVELS + 1, SCAN_C, SCAN_C), lambda b, j: (0, 0, 0, 0))],
        out_specs=[spec(fwd_rows, 0), spec(bwd_rows, 0)],
        out_shape=[jax.ShapeDtypeStruct((T_ROWS, C_WIDTH), F32)] * 2,
        scratch_shapes=[pltpu.VMEM((2, C_HEADS, HEAD_DIM, HEAD_DIM), F32)],
        compiler_params=_cparams(("parallel", "arbitrary")),
        name="hgrn2_scan",
    )(p_hi, p_hi, p_hi, p_hi, p_hi, p_hi, lb2, xmat, masks)


def _layer_norm_rows(z, g, b):
    mu = jnp.mean(z, axis=-1, keepdims=True)
    var = jnp.mean(jnp.square(z - mu), axis=-1, keepdims=True)
    return (z - mu) * lax.rsqrt(var + LN_EPS) * g + b


def _outproj_kernel(a_ref, b_ref, of_ref, ob_ref, cg_ref, hg_ref, x_ref, mod_ref, w_ref,
                    l1g_ref, l1b_ref, rw_ref, rb_ref,
                    x1_ref, hx_ref, hxp_ref, idx_ref, wt_ref, rank_ref, cnt_ref, cnt_scr):
    parts = []
    for h in range(C_HEADS):
        sl = slice(h * HEAD_DIM, (h + 1) * HEAD_DIM)
        o = of_ref[:, sl] + ob_ref[:, sl]
        on = o * lax.rsqrt(jnp.mean(jnp.square(o), axis=-1, keepdims=True) + NORM_EPS) * hg_ref[...]
        g = cg_ref[:, sl]
        parts.append((on * (g * _sigmoid(g))).astype(BF16))
    c_x = jnp.concatenate(parts, axis=1)
    mix = jnp.dot(a_ref[...], w_ref[0:A_WIDTH, :], preferred_element_type=F32)
    mix = mix + jnp.dot(b_ref[...], w_ref[A_WIDTH:A_WIDTH + B_WIDTH, :], preferred_element_type=F32)
    mix = mix + jnp.dot(c_x, w_ref[A_WIDTH + B_WIDTH:, :], preferred_element_type=F32)
    g1 = mod_ref[0, 2:3, :]
    sh2 = mod_ref[0, 3:4, :]
    sc2 = mod_ref[0, 4:5, :]
    x1 = _layer_norm_rows(DEEPNORM_ALPHA * x_ref[...] + g1 * mix, l1g_ref[...], l1b_ref[...])
    x1_ref[...] = x1
    hx = x1 * (1.0 + sc2) + sh2
    hx_bf = hx.astype(BF16)
    hx_ref[...] = hx_bf
    bits = pltpu.bitcast(hx_bf.astype(F32), jnp.uint32)
    hxp_ref[...] = bits[:, D_MODEL // 2:] | (bits[:, :D_MODEL // 2] >> 16)

    logits = jnp.dot(hx.astype(BF16), rw_ref[...], preferred_element_type=F32)
    scores = _sigmoid(logits)
    sel = scores + rb_ref[...]
    tm = sel.shape[0]
    lane = lax.broadcasted_iota(jnp.int32, sel.shape, 1).astype(F32)
    slot = lax.broadcasted_iota(jnp.int32, (tm, LANE), 1)
    idx_acc = jnp.zeros((tm, LANE), F32)
    wt_acc = jnp.zeros((tm, LANE), F32)
    chosen = jnp.zeros(sel.shape, F32)
    hits = []
    for k in range(TOP_K):
        mx = jnp.max(sel, axis=-1, keepdims=True)
        idx = jnp.min(jnp.where(sel == mx, lane, float(N_EXPERTS)), axis=-1, keepdims=True)
        hit = lane == idx
        hits.append(hit)
        chosen = jnp.where(hit, 1.0, chosen)
        w_k = jnp.sum(jnp.where(hit, scores, 0.0), axis=-1, keepdims=True)
        idx_acc = jnp.where(slot == k, idx, idx_acc)
        wt_acc = jnp.where(slot == k, w_k, wt_acc)
        sel = jnp.where(hit, -jnp.inf, sel)
    idx_ref[...] = idx_acc.astype(jnp.int32)
    wt_ref[...] = wt_acc / jnp.sum(wt_acc, axis=-1, keepdims=True) * ROUTED_SCALE

    @pl.when(pl.program_id(0) == 0)
    def _():
        cnt_scr[...] = jnp.zeros(cnt_scr.shape, F32)

    below = (lax.broadcasted_iota(jnp.int32, (tm, tm), 0) > lax.broadcasted_iota(jnp.int32, (tm, tm), 1))
    before = jnp.dot(below.astype(BF16), chosen.astype(BF16), preferred_element_type=F32) + cnt_scr[0:1, :]
    rank_acc = jnp.zeros((tm, LANE), F32)
    for k in range(TOP_K):
        r_k = jnp.sum(jnp.where(hits[k], before, 0.0), axis=-1, keepdims=True)
        rank_acc = jnp.where(slot == k, r_k, rank_acc)
    rank_ref[...] = rank_acc.astype(jnp.int32)
    cnt = cnt_scr[...] + jnp.sum(chosen, axis=0, keepdims=True)
    cnt_scr[...] = cnt
    cnt_ref[...] = cnt


def _outproj(a_x, b_x, o_f, o_b, p_hi, hnorm_g, x, mod3, w_out, ln_g, ln_b, router_w, router_b):
    row = lambda width: pl.BlockSpec((OUT_TM, width), lambda i: (i, 0))
    const = lambda shape: pl.BlockSpec(shape, lambda i: tuple(0 for _ in shape))
    return pl.pallas_call(
        _outproj_kernel,
        grid=(T_ROWS // OUT_TM,),
        in_specs=[row(A_WIDTH), row(B_WIDTH), row(C_WIDTH), row(C_WIDTH),
                  pl.BlockSpec((OUT_TM, C_WIDTH), lambda i: (i, 4)),
                  const((1, HEAD_DIM)),
                  row(D_MODEL),
                  pl.BlockSpec((1, 6, D_MODEL), lambda i: (_mod_row(i, OUT_TM), 0, 0)),
                  const((D_MODEL, D_MODEL)),
                  const((1, D_MODEL)), const((1, D_MODEL)),
                  const((D_MODEL, N_EXPERTS)), const((1, N_EXPERTS))],
        out_specs=[row(D_MODEL), row(D_MODEL), row(D_MODEL // 2), row(LANE), row(LANE), row(LANE),
                   const((8, N_EXPERTS))],
        out_shape=[jax.ShapeDtypeStruct((T_ROWS, D_MODEL), F32),
                   jax.ShapeDtypeStruct((T_ROWS, D_MODEL), BF16),
                   jax.ShapeDtypeStruct((T_ROWS, D_MODEL // 2), jnp.uint32),
                   jax.ShapeDtypeStruct((T_ROWS, LANE), jnp.int32),
                   jax.ShapeDtypeStruct((T_ROWS, LANE), F32),
                   jax.ShapeDtypeStruct((T_ROWS, LANE), jnp.int32),
                   jax.ShapeDtypeStruct((8, N_EXPERTS), F32)],
        scratch_shapes=[pltpu.VMEM((8, N_EXPERTS), F32)],
        compiler_params=_cparams(("arbitrary",)),
        name="outproj_ln_router",
    )(a_x, b_x, o_f, o_b, p_hi, hnorm_g, x, mod3, w_out, ln_g, ln_b, router_w, router_b)


def _experts_kernel(te_ref, nt_ref, xs_ref, wg_ref, wu_ref, wd_ref, ys_ref, wgu_scr, wd_scr, *, packed):
    i = pl.program_id(0)
    live = i < nt_ref[0]
    new_expert = jnp.logical_or(i == 0, te_ref[i] != te_ref[jnp.maximum(i - 1, 0)])

    @pl.when(jnp.logical_and(live, new_expert))
    def _():
        wgu_scr[:, :EXPERT_DIM] = wg_ref[0, 0].astype(BF16)
        wgu_scr[:, EXPERT_DIM:] = wu_ref[0, 0].astype(BF16)
        wd_scr[...] = wd_ref[0, 0].astype(BF16)

    @pl.when(live)
    def _():
        if packed:
            words = xs_ref[...]
            low = pltpu.bitcast(words << 16, F32).astype(BF16)
            high = pltpu.bitcast(words & jnp.uint32(0xFFFF0000), F32).astype(BF16)
            x = jnp.concatenate([low, high], axis=1)
        else:
            x = xs_ref[...]
        h = jnp.dot(x, wgu_scr[...], preferred_element_type=F32)
        g = h[:, :EXPERT_DIM]
        u = h[:, EXPERT_DIM:]
        a = (g * _sigmoid(g) * u).astype(BF16)
        ys_ref[...] = jnp.dot(a, wd_scr[...], preferred_element_type=F32).astype(ys_ref.dtype)

    @pl.when(jnp.logical_not(live))
    def _():
        ys_ref[...] = jnp.zeros(ys_ref.shape, ys_ref.dtype)


def _experts(tile_expert, n_tiles, xs, w_gate, w_up, w_down, layer, out_dtype):
    tiles = xs.shape[0] // MOE_TM
    last = lambda i, nt: jnp.minimum(i, nt[0] - 1)
    grid_spec = pltpu.PrefetchScalarGridSpec(
        num_scalar_prefetch=2,
        grid=(tiles,),
        in_specs=[pl.BlockSpec((MOE_TM, xs.shape[1]), lambda i, te, nt: (last(i, nt), 0)),
                  pl.BlockSpec((1, 1, D_MODEL, EXPERT_DIM), lambda i, te, nt: (layer, te[i], 0, 0)),
                  pl.BlockSpec((1, 1, D_MODEL, EXPERT_DIM), lambda i, te, nt: (layer, te[i], 0, 0)),
                  pl.BlockSpec((1, 1, EXPERT_DIM, D_MODEL), lambda i, te, nt: (layer, te[i], 0, 0))],
        out_specs=pl.BlockSpec((MOE_TM, D_MODEL), lambda i, te, nt: (i, 0)),
        scratch_shapes=[pltpu.VMEM((D_MODEL, 2 * EXPERT_DIM), BF16),
                        pltpu.VMEM((EXPERT_DIM, D_MODEL), BF16)],
    )
    return pl.pallas_call(
        functools.partial(_experts_kernel, packed=xs.dtype == jnp.uint32),
        grid_spec=grid_spec,
        out_shape=jax.ShapeDtypeStruct((xs.shape[0], D_MODEL), out_dtype),
        compiler_params=_cparams(("arbitrary",)),
        name="experts",
    )(tile_expert, n_tiles, xs, w_gate, w_up, w_down)


def _combine_kernel(pos_ref, wt_ref, ysh_ref, x1_ref, mod_ref, g_ref, b_ref, ys_hbm, o_ref, buf, sem):
    tm = COMBINE_TM
    n_rows = TOP_K * tm

    def row_copy(j):
        return pltpu.make_async_copy(ys_hbm.at[pl.ds(pos_ref[j], 1), :], buf.at[pl.ds(j, 1), :], sem)

    def issue(j, carry):
        row_copy(j).start()
        return carry

    lax.fori_loop(0, n_rows, issue, 0, unroll=8)
    pltpu.make_async_copy(ys_hbm.at[pl.ds(0, n_rows), :], buf, sem).wait()

    y = ysh_ref[...].astype(F32)
    for k in range(TOP_K):
        y = y + wt_ref[:, k:k + 1] * buf[k * tm:(k + 1) * tm, :]
    g2 = mod_ref[0, 5:6, :]
    o_ref[...] = _layer_norm_rows(DEEPNORM_ALPHA * x1_ref[...] + g2 * y, g_ref[...], b_ref[...])


def _combine(ys, pos_blocks, wt, ysh, x1, mod3, ln_g, ln_b):
    tm = COMBINE_TM
    row = lambda width: pl.BlockSpec((tm, width), lambda i: (i, 0))
    return pl.pallas_call(
        _combine_kernel,
        grid=(T_ROWS // tm,),
        in_specs=[pl.BlockSpec((TOP_K * tm,), lambda i: (i,), memory_space=pltpu.SMEM),
                  row(LANE), row(D_MODEL), row(D_MODEL),
                  pl.BlockSpec((1, 6, D_MODEL), lambda i: (_mod_row(i, tm), 0, 0)),
                  pl.BlockSpec((1, D_MODEL), lambda i: (0, 0)),
                  pl.BlockSpec((1, D_MODEL), lambda i: (0, 0)),
                  pl.BlockSpec(memory_space=pl.ANY)],
        out_specs=row(D_MODEL),
        out_shape=jax.ShapeDtypeStruct((T_ROWS, D_MODEL), F32),
        scratch_shapes=[pltpu.VMEM((TOP_K * tm, D_MODEL), F32), pltpu.SemaphoreType.DMA(())],
        compiler_params=_cparams(("arbitrary",)),
        name="moe_combine_ln",
    )(pos_blocks, wt, ysh, x1, mod3, ln_g, ln_b, ys)


def _layout(idx, rank, counts):
    tiles_per = (counts + MOE_TM - 1) // MOE_TM
    tile_end = jnp.cumsum(tiles_per)
    start = (tile_end - tiles_per) * MOE_TM
    experts = jnp.arange(N_EXPERTS, dtype=jnp.int32)
    pos = rank + jnp.sum(jnp.where(idx[:, :, None] == experts, start, 0), axis=-1)
    n_tiles = tile_end[-1:].astype(jnp.int32)
    tile = jnp.arange(MOE_TILES, dtype=jnp.int32)
    tile_expert = jnp.sum((tile_end[None, :] <= tile[:, None]).astype(jnp.int32), axis=-1)
    tile_expert = jnp.minimum(tile_expert, N_EXPERTS - 1)
    return pos, tile_expert, n_tiles


def _dispatch_kernel(pos_ref, hxw_hbm, init_hbm, xs_hbm, sem):
    base = pl.program_id(0) * DISPATCH_TM

    def issue(t, carry):
        for k in range(TOP_K):
            pltpu.make_async_copy(hxw_hbm.at[pl.ds(base + t, 1), :],
                                  xs_hbm.at[pl.ds(pos_ref[t * TOP_K + k], 1), :], sem).start()
        return carry

    lax.fori_loop(0, DISPATCH_TM, issue, 0)
    for _ in range(TOP_K):
        pltpu.make_async_copy(hxw_hbm.at[pl.ds(0, DISPATCH_TM), :],
                              xs_hbm.at[pl.ds(0, DISPATCH_TM), :], sem).wait()


def _dispatch_rows(pos, hx_words):
    init = jnp.zeros((MOE_ROWS, D_MODEL // 2), jnp.uint32)
    return pl.pallas_call(
        _dispatch_kernel,
        grid=(T_ROWS // DISPATCH_TM,),
        in_specs=[pl.BlockSpec((DISPATCH_TM * TOP_K,), lambda i: (i,), memory_space=pltpu.SMEM),
                  pl.BlockSpec(memory_space=pl.ANY),
                  pl.BlockSpec(memory_space=pl.ANY)],
        out_specs=pl.BlockSpec(memory_space=pl.ANY),
        out_shape=jax.ShapeDtypeStruct((MOE_ROWS, D_MODEL // 2), jnp.uint32),
        scratch_shapes=[pltpu.SemaphoreType.DMA(())],
        input_output_aliases={2: 0},
        compiler_params=pltpu.CompilerParams(dimension_semantics=("arbitrary",), has_side_effects=True),
        name="moe_dispatch_rows",
    )(pos.reshape(-1), hx_words, init)


def _rope_tables():
    t = jnp.arange(SEQ)
    row = (t // GRID_W).astype(F32)
    col = (t % GRID_W).astype(F32)
    axis_dim = HEAD_DIM // 2
    inv = ROPE_THETA ** (-jnp.arange(0, axis_dim, 2, dtype=F32) / axis_dim)
    ar = row[:, None] * inv
    ac = col[:, None] * inv
    cos = jnp.concatenate([jnp.cos(ar), jnp.cos(ar), jnp.cos(ac), jnp.cos(ac)], axis=-1)
    sin = jnp.concatenate([jnp.sin(ar), jnp.sin(ar), jnp.sin(ac), jnp.sin(ac)], axis=-1)
    odd = ((jnp.arange(HEAD_DIM) // 32) % 2 == 1)[None, :]
    sa = jnp.where(odd, sin, 0.0)
    sb = jnp.where(odd, 0.0, -sin)
    lat = lambda z: jnp.tile(z, (BATCH, 1))
    cos_all = jnp.concatenate([lat(cos), jnp.ones((N_CTX, HEAD_DIM), F32)], axis=0)
    sa_all = jnp.concatenate([lat(sa), jnp.zeros((N_CTX, HEAD_DIM), F32)], axis=0)
    sb_all = jnp.concatenate([lat(sb), jnp.zeros((N_CTX, HEAD_DIM), F32)], axis=0)
    return cos_all, sa_all, sb_all


def kernel(x, c, ctx, c_ctx, w_mod, b_mod, w_in, w_out, gmlp_ln_g, gmlp_ln_b, gmlp_ws, gmlp_bs,
           diff_lam, diff_subln_g, hgrn_lb, hgrn_norm_g, ln1_g, ln1_b, ln2_g, ln2_b,
           router_w, router_b, exp_w_gate, exp_w_up, exp_w_down, sh_w_gate, sh_w_up, sh_w_down):
    assert x.shape == (BATCH, SEQ, D_MODEL) and ctx.shape == (BATCH, CTX_LEN, D_MODEL)
    xs_all = jnp.concatenate([x.reshape(N_LAT, D_MODEL), ctx.reshape(N_CTX, D_MODEL)], axis=0)

    cond = jnp.zeros((8, D_MODEL), F32).at[0:BATCH].set(c).at[BATCH].set(c_ctx)
    mod_all = _modulation(cond, w_mod, b_mod)[:, :BATCH + 1].reshape(DEPTH, BATCH + 1, 6, D_MODEL)

    sm = jax.nn.softmax(hgrn_lb.astype(F32), axis=0)
    lb_all = jnp.cumsum(sm, axis=0) - sm[0]
    tables = _rope_tables()
    xmat_np, masks_np = _scan_structure()
    xmat = jnp.asarray(xmat_np, BF16)
    masks = jnp.asarray(masks_np, F32)

    for l in range(DEPTH):
        mod3 = mod_all[l]
        w_in_l = w_in[l].astype(BF16)
        p_lo = _inproj(xs_all, mod3, w_in_l[:, :LOW_WIDTH], tables, BF16, True)
        p_hi = _inproj(xs_all, mod3, w_in_l[:, LOW_WIDTH:], tables, F32, False)

        a_x = _gmlp(p_lo, gmlp_ln_g[l][None, :], gmlp_ln_b[l][None, :],
                    gmlp_ws[l].astype(BF16), gmlp_bs[l].T)

        lam_init = 0.8 - 0.6 * math.exp(-0.3 * l)
        dl = diff_lam[l].astype(F32)
        lam = jnp.exp(jnp.sum(dl[0] * dl[1])) - jnp.exp(jnp.sum(dl[2] * dl[3])) + lam_init
        lam2 = jnp.stack([lam, jnp.asarray(1.0 - lam_init, F32)]).astype(F32)
        b_x = _diff_attention(p_lo, lam2, diff_subln_g[l][None, :])

        o_f, o_b = _hgrn_scan(p_hi, lb_all[l], xmat, masks)

        x1, hx, hx_words, idx, wt, rank, cnt = _outproj(
            a_x, b_x, o_f, o_b, p_hi, hgrn_norm_g[l][None, :], xs_all, mod3,
            w_out[l].astype(BF16), ln1_g[l][None, :], ln1_b[l][None, :],
            router_w[l].astype(BF16), router_b[l][None, :])

        pos, tile_expert, n_tiles = _layout(idx[:, :TOP_K], rank[:, :TOP_K], cnt[0].astype(jnp.int32))
        xs_grouped = _dispatch_rows(pos, hx_words)
        ys = _experts(tile_expert, n_tiles, xs_grouped, exp_w_gate, exp_w_up, exp_w_down, l, F32)
        pos_blocks = pos.reshape(T_ROWS // COMBINE_TM, COMBINE_TM, TOP_K).transpose(0, 2, 1).reshape(-1)

        ysh = _experts(jnp.zeros((T_ROWS // MOE_TM,), jnp.int32),
                       jnp.full((1,), T_ROWS // MOE_TM, jnp.int32),
                       hx, sh_w_gate[:, None], sh_w_up[:, None], sh_w_down[:, None], l, BF16)

        xs_all = _combine(ys, pos_blocks, wt, ysh, x1, mod3, ln2_g[l][None, :], ln2_b[l][None, :])

    return xs_all[:N_LAT].reshape(BATCH, SEQ, D_MODEL)
```

```python
import functools
import math

import numpy as np
import jax
import jax.numpy as jnp
from jax import lax
from jax.experimental import pallas as pl
from jax.experimental.pallas import tpu as pltpu

F32 = jnp.float32
BF16 = jnp.bfloat16

D_MODEL = 2048
BATCH = 2
SEQ = 4096
DEPTH = 4
CTX_LEN = 256
GRID_W = 64
HEAD_DIM = 128
A_WIDTH = 512
A_HEADS = 4
A_CHUNK = 128
B_WIDTH = 1024
B_HEADS = 4
B_QK_WIDTH = 1024
B_V_DIM = 256
ROPE_THETA = 10000.0
C_WIDTH = 512
C_HEADS = 4
IN_WIDTH = 6656
N_EXPERTS = 64
TOP_K = 8
EXPERT_DIM = 384
ROUTED_SCALE = 2.5
DEEPNORM_ALPHA = (2.0 * DEPTH) ** 0.25
LN_EPS = 1e-5
NORM_EPS = 1e-5

N_LAT = BATCH * SEQ
N_CTX = BATCH * CTX_LEN
T_ROWS = N_LAT + N_CTX
LOW_WIDTH = 4096
HI_WIDTH = IN_WIDTH - LOW_WIDTH
QK_SCALE = HEAD_DIM ** -0.5 * math.log2(math.e)

LANE = 128
PROJ_TM = 512
PROJ_TN = 512
ATT_TQ = 256
ATT_TK = 1024
SCAN_C = 128
SCAN_LEVELS = 7
OUT_TM = 256
MOE_TM = 256
COMBINE_TM = 128
DISPATCH_TM = 256
MOE_ROWS = ((T_ROWS * TOP_K + N_EXPERTS * (MOE_TM - 1)) // MOE_TM) * MOE_TM
MOE_TILES = MOE_ROWS // MOE_TM
VMEM_LIMIT = 48 * 1024 * 1024


def _cparams(sem, vmem=VMEM_LIMIT):
    return pltpu.CompilerParams(dimension_semantics=sem, vmem_limit_bytes=vmem)


def _sigmoid(x):
    return 1.0 / (1.0 + jnp.exp(-x))


def _gelu_tanh(x):
    return 0.5 * x * (1.0 + jnp.tanh(math.sqrt(2.0 / math.pi) * (x + 0.044715 * (x * x * x))))


def _pack_pairs(x):
    n = x.shape[1] // 2
    bits = pltpu.bitcast(x.astype(BF16).astype(F32), jnp.uint32)
    return bits[:, n:] | (bits[:, :n] >> 16)


def _unpack_pairs(words):
    low = pltpu.bitcast(words << 16, F32)
    high = pltpu.bitcast(words & jnp.uint32(0xFFFF0000), F32)
    return low, high


def _dot_nt(a, b):
    return lax.dot_general(a, b, (((1,), (1,)), ((), ())), preferred_element_type=F32)


def _dot_tn(a, b):
    return lax.dot_general(a, b, (((0,), (0,)), ((), ())), preferred_element_type=F32)


def _mod_kernel(c_ref, w_ref, b_ref, o_ref):
    c = c_ref[...]
    s = (c * _sigmoid(c)).astype(BF16)
    o_ref[0] = jnp.dot(s, w_ref[0].astype(BF16), preferred_element_type=F32) + b_ref[0]


def _modulation(cond, w_mod, b_mod):
    tn = 1024
    return pl.pallas_call(
        _mod_kernel,
        grid=(DEPTH, 6 * D_MODEL // tn),
        in_specs=[pl.BlockSpec((8, D_MODEL), lambda l, n: (0, 0)),
                  pl.BlockSpec((1, D_MODEL, tn), lambda l, n: (l, 0, n)),
                  pl.BlockSpec((1, 1, tn), lambda l, n: (l, 0, n))],
        out_specs=pl.BlockSpec((1, 8, tn), lambda l, n: (l, 0, n)),
        out_shape=jax.ShapeDtypeStruct((DEPTH, 8, 6 * D_MODEL), F32),
        compiler_params=_cparams(("parallel", "parallel")),
        name="modulation",
    )(cond, w_mod, b_mod.reshape(DEPTH, 1, 6 * D_MODEL))


def _mod_row(m, tm):
    return jnp.minimum(m // (SEQ // tm), BATCH)


def _inproj_kernel(x_ref, mod_ref, w_ref, cos_ref, sa_ref, sb_ref, o_ref, h_scr, *, rope):
    n = pl.program_id(1)

    @pl.when(n == 0)
    def _():
        sh = mod_ref[0, 0:1, :]
        sc = mod_ref[0, 1:2, :]
        h_scr[...] = (x_ref[...] * (1.0 + sc) + sh).astype(BF16)

    acc = jnp.dot(h_scr[...], w_ref[...], preferred_element_type=F32)
    if not rope:
        o_ref[...] = acc.astype(o_ref.dtype)
        return

    is_qk = (n >= 2) & (n < 6)

    @pl.when(is_qk)
    def _():
        scale = jnp.where(n < 4, QK_SCALE, 1.0).astype(F32)
        cos = cos_ref[...]
        sa = sa_ref[...]
        sb = sb_ref[...]
        for g in range(PROJ_TN // LANE):
            blk = acc[:, g * LANE:(g + 1) * LANE]
            r = blk * cos + pltpu.roll(blk, 32, 1) * sa + pltpu.roll(blk, 96, 1) * sb
            o_ref[:, g * LANE:(g + 1) * LANE] = (r * scale).astype(o_ref.dtype)

    @pl.when(jnp.logical_not(is_qk))
    def _():
        o_ref[...] = acc.astype(o_ref.dtype)


def _inproj(x, mod3, w, tables, out_dtype, rope):
    width = w.shape[1]
    cos, sa, sb = tables
    tab_spec = pl.BlockSpec((PROJ_TM, LANE), lambda m, n: (m, 0))
    return pl.pallas_call(
        functools.partial(_inproj_kernel, rope=rope),
        grid=(T_ROWS // PROJ_TM, width // PROJ_TN),
        in_specs=[pl.BlockSpec((PROJ_TM, D_MODEL), lambda m, n: (m, 0)),
                  pl.BlockSpec((1, 6, D_MODEL), lambda m, n: (_mod_row(m, PROJ_TM), 0, 0)),
                  pl.BlockSpec((D_MODEL, PROJ_TN), lambda m, n: (0, n)),
                  tab_spec, tab_spec, tab_spec],
        out_specs=pl.BlockSpec((PROJ_TM, PROJ_TN), lambda m, n: (m, n)),
        out_shape=jax.ShapeDtypeStruct((T_ROWS, width), out_dtype),
        scratch_shapes=[pltpu.VMEM((PROJ_TM, D_MODEL), BF16)],
        compiler_params=_cparams(("parallel", "arbitrary")),
        name="inproj_rope" if rope else "inproj",
    )(x, mod3, w, cos, sa, sb)


def _gmlp_kernel(u_ref, v_ref, g_ref, b_ref, ws_ref, bs_ref, o_ref):
    for h in range(A_HEADS):
        sl = slice(h * HEAD_DIM, (h + 1) * HEAD_DIM)
        v = _gelu_tanh(v_ref[:, sl].astype(F32))
        mu = jnp.mean(v, axis=-1, keepdims=True)
        var = jnp.mean(jnp.square(v - mu), axis=-1, keepdims=True)
        vn = (v - mu) * lax.rsqrt(var + LN_EPS) * g_ref[:, sl] + b_ref[:, sl]
        s = jnp.dot(ws_ref[h], vn.astype(BF16), preferred_element_type=F32) + bs_ref[:, h:h + 1]
        u = _gelu_tanh(u_ref[:, sl].astype(F32))
        o_ref[:, sl] = (u * s).astype(o_ref.dtype)


def _gmlp(p_lo, ln_g, ln_b, ws, bs_t):
    return pl.pallas_call(
        _gmlp_kernel,
        grid=(T_ROWS // A_CHUNK,),
        in_specs=[pl.BlockSpec((A_CHUNK, A_WIDTH), lambda i: (i, 0)),
                  pl.BlockSpec((A_CHUNK, A_WIDTH), lambda i: (i, 1)),
                  pl.BlockSpec((1, A_WIDTH), lambda i: (0, 0)),
                  pl.BlockSpec((1, A_WIDTH), lambda i: (0, 0)),
                  pl.BlockSpec((A_HEADS, A_CHUNK, A_CHUNK), lambda i: (0, 0, 0)),
                  pl.BlockSpec((A_CHUNK, A_HEADS), lambda i: (0, 0))],
        out_specs=pl.BlockSpec((A_CHUNK, A_WIDTH), lambda i: (i, 0)),
        out_shape=jax.ShapeDtypeStruct((T_ROWS, A_WIDTH), BF16),
        compiler_params=_cparams(("parallel",)),
        name="gmlp",
    )(p_lo, p_lo, ln_g, ln_b, ws, bs_t)


NQ_LAT = SEQ // ATT_TQ


def _attn_kernel(lam_ref, q_ref, kl_ref, vl_ref, kc_ref, vc_ref, g_ref, o_ref, m_scr, l_scr, acc_scr):
    qi = pl.program_id(2)
    m_scr[...] = jnp.full(m_scr.shape, -1e30, F32)
    l_scr[...] = jnp.zeros(l_scr.shape, F32)
    acc_scr[...] = jnp.zeros(acc_scr.shape, F32)

    def process(k, v):
        for mp in range(2):
            sl = slice(mp * HEAD_DIM, (mp + 1) * HEAD_DIM)
            s = _dot_nt(q_ref[:, sl], k[:, sl])
            cols = [s[:, c * LANE:(c + 1) * LANE] for c in range(k.shape[0] // LANE)]
            m_part = cols[0]
            for col in cols[1:]:
                m_part = jnp.maximum(m_part, col)
            m_old = m_scr[mp]
            m_new = jnp.maximum(m_old, jnp.max(m_part, axis=-1, keepdims=True))
            alpha = jnp.exp2(m_old - m_new)
            l_part = alpha * l_scr[mp]
            ps = []
            for col in cols:
                p = jnp.exp2(col - m_new)
                l_part = l_part + p
                ps.append(p.astype(BF16))
            pv = jnp.dot(jnp.concatenate(ps, axis=1), v, preferred_element_type=F32)
            for half in range(B_V_DIM // LANE):
                hs = slice(half * LANE, (half + 1) * LANE)
                acc_scr[mp, :, hs] = alpha * acc_scr[mp, :, hs] + pv[:, hs]
            l_scr[mp] = l_part
            m_scr[mp] = m_new

    process(kc_ref[...], vc_ref[...])

    @pl.when(qi < NQ_LAT)
    def _():
        for j in range(SEQ // ATT_TK):
            rows = slice(j * ATT_TK, (j + 1) * ATT_TK)
            process(kl_ref[rows, :], vl_ref[rows, :])

    lam = lam_ref[0]
    post = lam_ref[1]
    l0 = jnp.sum(l_scr[0], axis=-1, keepdims=True)
    l1 = jnp.sum(l_scr[1], axis=-1, keepdims=True)
    o = acc_scr[0] / l0 - lam * (acc_scr[1] / l1)
    r = o * lax.rsqrt(jnp.mean(jnp.square(o), axis=-1, keepdims=True) + NORM_EPS)
    o_ref[...] = (r * g_ref[...] * post).astype(o_ref.dtype)


def _diff_attention(p_lo, lam2, subln_g):
    def q_rows(b, h, qi):
        return jnp.where(qi < NQ_LAT, b * NQ_LAT + qi, BATCH * NQ_LAT + b)

    return pl.pallas_call(
        _attn_kernel,
        grid=(BATCH, B_HEADS, NQ_LAT + 1),
        in_specs=[pl.BlockSpec(memory_space=pltpu.SMEM),
                  pl.BlockSpec((ATT_TQ, B_V_DIM), lambda b, h, qi: (q_rows(b, h, qi), 4 + h)),
                  pl.BlockSpec((SEQ, B_V_DIM), lambda b, h, qi: (b, 8 + h)),
                  pl.BlockSpec((SEQ, B_V_DIM), lambda b, h, qi: (b, 12 + h)),
                  pl.BlockSpec((CTX_LEN, B_V_DIM), lambda b, h, qi: (N_LAT // CTX_LEN + b, 8 + h)),
                  pl.BlockSpec((CTX_LEN, B_V_DIM), lambda b, h, qi: (N_LAT // CTX_LEN + b, 12 + h)),
                  pl.BlockSpec((1, B_V_DIM), lambda b, h, qi: (0, 0))],
        out_specs=pl.BlockSpec((ATT_TQ, B_V_DIM), lambda b, h, qi: (q_rows(b, h, qi), h)),
        scratch_shapes=[pltpu.VMEM((2, ATT_TQ, LANE), F32),
                        pltpu.VMEM((2, ATT_TQ, LANE), F32),
                        pltpu.VMEM((2, ATT_TQ, B_V_DIM), F32)],
        out_shape=jax.ShapeDtypeStruct((T_ROWS, B_WIDTH), BF16),
        compiler_params=_cparams(("parallel", "parallel", "arbitrary")),
        name="diff_attention",
    )(lam2, p_lo, p_lo, p_lo, p_lo, p_lo, subln_g)


def _scan_structure():
    c = SCAN_C
    x = np.zeros((2, (2 + SCAN_LEVELS) * c, c), np.float32)
    msk = np.zeros((2, SCAN_LEVELS + 1, c, c), np.float32)
    r = np.arange(c)
    j = np.arange(c)[None, :]
    xf = np.zeros(((2 + SCAN_LEVELS) * c, c), np.float32)
    mf = np.zeros((SCAN_LEVELS + 1, c, c), np.float32)
    xf[0:c] = (j <= r[:, None])
    xf[c:2 * c] = (j > r[:, None])
    for lev in range(SCAN_LEVELS):
        half = (c // 2) >> lev
        start = (r // (2 * half)) * (2 * half)
        mid = start + half - 1
        later = r > mid
        rows = np.where(later[:, None], (j > mid[:, None]) & (j <= r[:, None]),
                        (j > r[:, None]) & (j <= mid[:, None]))
        xf[(2 + lev) * c:(3 + lev) * c] = rows
        same = start[:, None] == start[None, :]
        mf[lev] = same & later[:, None] & (~later)[None, :]
    mf[SCAN_LEVELS] = np.eye(c)
    x[0] = xf
    msk[0] = mf
    x[1] = xf.reshape(2 + SCAN_LEVELS, c, c)[:, ::-1, ::-1].reshape(-1, c)
    msk[1] = mf[:, ::-1, ::-1]
    return x, msk


def _hgrn_kernel(qf_ref, ff_ref, if_ref, qb_ref, fb_ref, ib_ref, lb_ref, x_ref, msk_ref,
                 of_ref, ob_ref, st_scr):
    j = pl.program_id(1)

    @pl.when(j == 0)
    def _():
        st_scr[...] = jnp.zeros(st_scr.shape, F32)

    c = SCAN_C
    dirs = ((qf_ref, ff_ref, if_ref, of_ref, c - 1), (qb_ref, fb_ref, ib_ref, ob_ref, 0))
    for d, (q_ref, f_ref, i_ref, o_ref, end_row) in enumerate(dirs):
        for h in range(C_HEADS):
            sl = slice(h * HEAD_DIM, (h + 1) * HEAD_DIM)
            qraw = q_ref[:, sl]
            q = qraw * _sigmoid(qraw)
            lb = lb_ref[d:d + 1, sl]
            f = lb + (1.0 - lb) * _sigmoid(f_ref[:, sl])
            logf = jnp.log(f)
            k = 1.0 - f
            v = i_ref[:, sl].astype(BF16)
            hi = logf.astype(BF16)
            lo = (logf - hi.astype(F32)).astype(BF16)
            e2 = jnp.dot(x_ref[d], jnp.concatenate([hi, lo], axis=1), preferred_element_type=F32)
            w = jnp.exp(e2[:, 0:c] + e2[:, c:2 * c])
            a = msk_ref[d, SCAN_LEVELS] * _dot_nt(q.astype(BF16), k.astype(BF16))
            for lev in range(SCAN_LEVELS):
                wl = w[(2 + lev) * c:(3 + lev) * c]
                a = a + msk_ref[d, lev] * _dot_nt((q * wl).astype(BF16), (k * wl).astype(BF16))
            st = st_scr[d, h]
            o = jnp.dot(a.astype(BF16), v, preferred_element_type=F32)
            o = o + _dot_nt((q * w[0:c]).astype(BF16), st.astype(BF16))
            o_ref[:, sl] = o
            g_end = w[end_row:end_row + 1, :]
            st_scr[d, h] = st * g_end + _dot_tn(v, (k * w[c:2 * c]).astype(BF16))


def _hgrn_scan(p_hi, lb2, xmat, masks):
    lat_chunks = SEQ // SCAN_C
    ctx_chunks = CTX_LEN // SCAN_C
    steps = ctx_chunks + lat_chunks

    def fwd_rows(b, j):
        return jnp.where(j < ctx_chunks, N_LAT // SCAN_C + b * ctx_chunks + j,
                         b * lat_chunks + j - ctx_chunks)

    def bwd_rows(b, j):
        return jnp.where(j < ctx_chunks, N_LAT // SCAN_C + b * ctx_chunks + (ctx_chunks - 1 - j),
                         b * lat_chunks + (steps - 1 - j))

    def spec(rows, col):
        return pl.BlockSpec((SCAN_C, C_WIDTH), lambda b, j: (rows(b, j), col))

    nx = (2 + SCAN_LEVELS) * SCAN_C
    return pl.pallas_call(
        _hgrn_kernel,
        grid=(BATCH, steps),
        in_specs=[spec(fwd_rows, 0), spec(fwd_rows, 1), spec(fwd_rows, 3),
                  spec(bwd_rows, 0), spec(bwd_rows, 2), spec(bwd_rows, 3),
                  pl.BlockSpec((2, C_WIDTH), lambda b, j: (0, 0)),
                  pl.BlockSpec((2, nx, SCAN_C), lambda b, j: (0, 0, 0)),
                  pl.BlockSpec((2, SCAN_LEVELS + 1, SCAN_C, SCAN_C), lambda b, j: (0, 0, 0, 0))],
        out_specs=[spec(fwd_rows, 0), spec(bwd_rows, 0)],
        out_shape=[jax.ShapeDtypeStruct((T_ROWS, C_WIDTH), F32)] * 2,
        scratch_shapes=[pltpu.VMEM((2, C_HEADS, HEAD_DIM, HEAD_DIM), F32)],
        compiler_params=_cparams(("parallel", "arbitrary")),
        name="hgrn2_scan",
    )(p_hi, p_hi, p_hi, p_hi, p_hi, p_hi, lb2, xmat, masks)


def _layer_norm_rows(z, g, b):
    mu = jnp.mean(z, axis=-1, keepdims=True)
    var = jnp.mean(jnp.square(z - mu), axis=-1, keepdims=True)
    return (z - mu) * lax.rsqrt(var + LN_EPS) * g + b


def _outproj_kernel(a_ref, b_ref, of_ref, ob_ref, cg_ref, hg_ref, x_ref, mod_ref, w_ref,
                    l1g_ref, l1b_ref, rw_ref, rb_ref,
                    x1_ref, hx_ref, hxp_ref, idx_ref, wt_ref, rank_ref, cnt_ref, cnt_scr):
    parts = []
    for h in range(C_HEADS):
        sl = slice(h * HEAD_DIM, (h + 1) * HEAD_DIM)
        o = of_ref[:, sl] + ob_ref[:, sl]
        on = o * lax.rsqrt(jnp.mean(jnp.square(o), axis=-1, keepdims=True) + NORM_EPS) * hg_ref[...]
        g = cg_ref[:, sl]
        parts.append((on * (g * _sigmoid(g))).astype(BF16))
    c_x = jnp.concatenate(parts, axis=1)
    mix = jnp.dot(a_ref[...], w_ref[0:A_WIDTH, :], preferred_element_type=F32)
    mix = mix + jnp.dot(b_ref[...], w_ref[A_WIDTH:A_WIDTH + B_WIDTH, :], preferred_element_type=F32)
    mix = mix + jnp.dot(c_x, w_ref[A_WIDTH + B_WIDTH:, :], preferred_element_type=F32)
    g1 = mod_ref[0, 2:3, :]
    sh2 = mod_ref[0, 3:4, :]
    sc2 = mod_ref[0, 4:5, :]
    x1 = _layer_norm_rows(DEEPNORM_ALPHA * x_ref[...] + g1 * mix, l1g_ref[...], l1b_ref[...])
    x1_ref[...] = x1
    hx = x1 * (1.0 + sc2) + sh2
    hx_bf = hx.astype(BF16)
    hx_ref[...] = hx_bf
    hxp_ref[...] = _pack_pairs(hx)

    logits = jnp.dot(hx.astype(BF16), rw_ref[...], preferred_element_type=F32)
    scores = _sigmoid(logits)
    sel = scores + rb_ref[...]
    tm = sel.shape[0]
    lane = lax.broadcasted_iota(jnp.int32, sel.shape, 1).astype(F32)
    slot = lax.broadcasted_iota(jnp.int32, (tm, LANE), 1)
    idx_acc = jnp.zeros((tm, LANE), F32)
    wt_acc = jnp.zeros((tm, LANE), F32)
    chosen = jnp.zeros(sel.shape, F32)
    hits = []
    for k in range(TOP_K):
        mx = jnp.max(sel, axis=-1, keepdims=True)
        idx = jnp.min(jnp.where(sel == mx, lane, float(N_EXPERTS)), axis=-1, keepdims=True)
        hit = lane == idx
        hits.append(hit)
        chosen = jnp.where(hit, 1.0, chosen)
        w_k = jnp.sum(jnp.where(hit, scores, 0.0), axis=-1, keepdims=True)
        idx_acc = jnp.where(slot == k, idx, idx_acc)
        wt_acc = jnp.where(slot == k, w_k, wt_acc)
        sel = jnp.where(hit, -jnp.inf, sel)
    idx_ref[...] = idx_acc.astype(jnp.int32)
    wt_ref[...] = wt_acc / jnp.sum(wt_acc, axis=-1, keepdims=True) * ROUTED_SCALE

    @pl.when(pl.program_id(0) == 0)
    def _():
        cnt_scr[...] = jnp.zeros(cnt_scr.shape, F32)

    below = (lax.broadcasted_iota(jnp.int32, (tm, tm), 0) > lax.broadcasted_iota(jnp.int32, (tm, tm), 1))
    before = jnp.dot(below.astype(BF16), chosen.astype(BF16), preferred_element_type=F32) + cnt_scr[0:1, :]
    rank_acc = jnp.zeros((tm, LANE), F32)
    for k in range(TOP_K):
        r_k = jnp.sum(jnp.where(hits[k], before, 0.0), axis=-1, keepdims=True)
        rank_acc = jnp.where(slot == k, r_k, rank_acc)
    rank_ref[...] = rank_acc.astype(jnp.int32)
    cnt = cnt_scr[...] + jnp.sum(chosen, axis=0, keepdims=True)
    cnt_scr[...] = cnt
    cnt_ref[...] = cnt


def _outproj(a_x, b_x, o_f, o_b, p_hi, hnorm_g, x, mod3, w_out, ln_g, ln_b, router_w, router_b):
    row = lambda width: pl.BlockSpec((OUT_TM, width), lambda i: (i, 0))
    const = lambda shape: pl.BlockSpec(shape, lambda i: tuple(0 for _ in shape))
    return pl.pallas_call(
        _outproj_kernel,
        grid=(T_ROWS // OUT_TM,),
        in_specs=[row(A_WIDTH), row(B_WIDTH), row(C_WIDTH), row(C_WIDTH),
                  pl.BlockSpec((OUT_TM, C_WIDTH), lambda i: (i, 4)),
                  const((1, HEAD_DIM)),
                  row(D_MODEL),
                  pl.BlockSpec((1, 6, D_MODEL), lambda i: (_mod_row(i, OUT_TM), 0, 0)),
                  const((D_MODEL, D_MODEL)),
                  const((1, D_MODEL)), const((1, D_MODEL)),
                  const((D_MODEL, N_EXPERTS)), const((1, N_EXPERTS))],
        out_specs=[row(D_MODEL), row(D_MODEL), row(D_MODEL // 2), row(LANE), row(LANE), row(LANE),
                   const((8, N_EXPERTS))],
        out_shape=[jax.ShapeDtypeStruct((T_ROWS, D_MODEL), F32),
                   jax.ShapeDtypeStruct((T_ROWS, D_MODEL), BF16),
                   jax.ShapeDtypeStruct((T_ROWS, D_MODEL // 2), jnp.uint32),
                   jax.ShapeDtypeStruct((T_ROWS, LANE), jnp.int32),
                   jax.ShapeDtypeStruct((T_ROWS, LANE), F32),
                   jax.ShapeDtypeStruct((T_ROWS, LANE), jnp.int32),
                   jax.ShapeDtypeStruct((8, N_EXPERTS), F32)],
        scratch_shapes=[pltpu.VMEM((8, N_EXPERTS), F32)],
        compiler_params=_cparams(("arbitrary",)),
        name="outproj_ln_router",
    )(a_x, b_x, o_f, o_b, p_hi, hnorm_g, x, mod3, w_out, ln_g, ln_b, router_w, router_b)


def _experts_kernel(te_ref, nt_ref, xs_ref, wg_ref, wu_ref, wd_ref, ys_ref, wgu_scr, wd_scr, *, packed):
    i = pl.program_id(0)
    live = i < nt_ref[0]
    new_expert = jnp.logical_or(i == 0, te_ref[i] != te_ref[jnp.maximum(i - 1, 0)])

    @pl.when(jnp.logical_and(live, new_expert))
    def _():
        wgu_scr[:, :EXPERT_DIM] = wg_ref[0, 0].astype(BF16)
        wgu_scr[:, EXPERT_DIM:] = wu_ref[0, 0].astype(BF16)
        wd_scr[...] = wd_ref[0, 0].astype(BF16)

    @pl.when(live)
    def _():
        if packed:
            low, high = _unpack_pairs(xs_ref[...])
            x = jnp.concatenate([low.astype(BF16), high.astype(BF16)], axis=1)
        else:
            x = xs_ref[...]
        h = jnp.dot(x, wgu_scr[...], preferred_element_type=F32)
        g = h[:, :EXPERT_DIM]
        u = h[:, EXPERT_DIM:]
        a = (g * _sigmoid(g) * u).astype(BF16)
        y = jnp.dot(a, wd_scr[...], preferred_element_type=F32)
        ys_ref[...] = _pack_pairs(y) if packed else y.astype(ys_ref.dtype)

    @pl.when(jnp.logical_not(live))
    def _():
        ys_ref[...] = jnp.zeros(ys_ref.shape, ys_ref.dtype)


def _experts(tile_expert, n_tiles, xs, w_gate, w_up, w_down, layer):
    tiles = xs.shape[0] // MOE_TM
    last = lambda i, nt: jnp.minimum(i, nt[0] - 1)
    grid_spec = pltpu.PrefetchScalarGridSpec(
        num_scalar_prefetch=2,
        grid=(tiles,),
        in_specs=[pl.BlockSpec((MOE_TM, xs.shape[1]), lambda i, te, nt: (last(i, nt), 0)),
                  pl.BlockSpec((1, 1, D_MODEL, EXPERT_DIM), lambda i, te, nt: (layer, te[i], 0, 0)),
                  pl.BlockSpec((1, 1, D_MODEL, EXPERT_DIM), lambda i, te, nt: (layer, te[i], 0, 0)),
                  pl.BlockSpec((1, 1, EXPERT_DIM, D_MODEL), lambda i, te, nt: (layer, te[i], 0, 0))],
        out_specs=pl.BlockSpec((MOE_TM, xs.shape[1]), lambda i, te, nt: (i, 0)),
        scratch_shapes=[pltpu.VMEM((D_MODEL, 2 * EXPERT_DIM), BF16),
                        pltpu.VMEM((EXPERT_DIM, D_MODEL), BF16)],
    )
    return pl.pallas_call(
        functools.partial(_experts_kernel, packed=xs.dtype == jnp.uint32),
        grid_spec=grid_spec,
        out_shape=jax.ShapeDtypeStruct(xs.shape, xs.dtype),
        compiler_params=_cparams(("arbitrary",)),
        name="experts",
    )(tile_expert, n_tiles, xs, w_gate, w_up, w_down)


def _combine_kernel(pos_ref, wt_ref, ysh_ref, x1_ref, mod_ref, g_ref, b_ref, ys_hbm, o_ref, buf, sem):
    tm = COMBINE_TM
    n_rows = TOP_K * tm

    def row_copy(j):
        return pltpu.make_async_copy(ys_hbm.at[pl.ds(pos_ref[j], 1), :], buf.at[pl.ds(j, 1), :], sem)

    def issue(j, carry):
        row_copy(j).start()
        return carry

    lax.fori_loop(0, n_rows, issue, 0, unroll=8)
    pltpu.make_async_copy(ys_hbm.at[pl.ds(0, n_rows), :], buf, sem).wait()

    low = jnp.zeros((tm, D_MODEL // 2), F32)
    high = jnp.zeros((tm, D_MODEL // 2), F32)
    for k in range(TOP_K):
        lo_k, hi_k = _unpack_pairs(buf[k * tm:(k + 1) * tm, :])
        low = low + wt_ref[:, k:k + 1] * lo_k
        high = high + wt_ref[:, k:k + 1] * hi_k
    y = ysh_ref[...].astype(F32) + jnp.concatenate([low, high], axis=1)
    g2 = mod_ref[0, 5:6, :]
    o_ref[...] = _layer_norm_rows(DEEPNORM_ALPHA * x1_ref[...] + g2 * y, g_ref[...], b_ref[...])


def _combine(ys, pos_blocks, wt, ysh, x1, mod3, ln_g, ln_b):
    tm = COMBINE_TM
    row = lambda width: pl.BlockSpec((tm, width), lambda i: (i, 0))
    return pl.pallas_call(
        _combine_kernel,
        grid=(T_ROWS // tm,),
        in_specs=[pl.BlockSpec((TOP_K * tm,), lambda i: (i,), memory_space=pltpu.SMEM),
                  row(LANE), row(D_MODEL), row(D_MODEL),
                  pl.BlockSpec((1, 6, D_MODEL), lambda i: (_mod_row(i, tm), 0, 0)),
                  pl.BlockSpec((1, D_MODEL), lambda i: (0, 0)),
                  pl.BlockSpec((1, D_MODEL), lambda i: (0, 0)),
                  pl.BlockSpec(memory_space=pl.ANY)],
        out_specs=row(D_MODEL),
        out_shape=jax.ShapeDtypeStruct((T_ROWS, D_MODEL), F32),
        scratch_shapes=[pltpu.VMEM((TOP_K * tm, D_MODEL // 2), jnp.uint32), pltpu.SemaphoreType.DMA(())],
        compiler_params=_cparams(("arbitrary",)),
        name="moe_combine_ln",
    )(pos_blocks, wt, ysh, x1, mod3, ln_g, ln_b, ys)


def _layout(idx, rank, counts):
    tiles_per = (counts + MOE_TM - 1) // MOE_TM
    tile_end = jnp.cumsum(tiles_per)
    start = (tile_end - tiles_per) * MOE_TM
    experts = jnp.arange(N_EXPERTS, dtype=jnp.int32)
    pos = rank + jnp.sum(jnp.where(idx[:, :, None] == experts, start, 0), axis=-1)
    n_tiles = tile_end[-1:].astype(jnp.int32)
    tile = jnp.arange(MOE_TILES, dtype=jnp.int32)
    tile_expert = jnp.sum((tile_end[None, :] <= tile[:, None]).astype(jnp.int32), axis=-1)
    tile_expert = jnp.minimum(tile_expert, N_EXPERTS - 1)
    return pos, tile_expert, n_tiles


def _dispatch_kernel(pos_ref, hxw_ref, init_hbm, xs_hbm, sem):
    def issue(t, carry):
        for k in range(TOP_K):
            pltpu.make_async_copy(hxw_ref.at[pl.ds(t, 1), :],
                                  xs_hbm.at[pl.ds(pos_ref[t * TOP_K + k], 1), :], sem).start()
        return carry

    lax.fori_loop(0, DISPATCH_TM, issue, 0)
    for _ in range(TOP_K):
        pltpu.make_async_copy(hxw_ref, xs_hbm.at[pl.ds(0, DISPATCH_TM), :], sem).wait()


def _dispatch_rows(pos, hx_words):
    init = jnp.zeros((MOE_ROWS, D_MODEL // 2), jnp.uint32)
    return pl.pallas_call(
        _dispatch_kernel,
        grid=(T_ROWS // DISPATCH_TM,),
        in_specs=[pl.BlockSpec((DISPATCH_TM * TOP_K,), lambda i: (i,), memory_space=pltpu.SMEM),
                  pl.BlockSpec((DISPATCH_TM, D_MODEL // 2), lambda i: (i, 0)),
                  pl.BlockSpec(memory_space=pl.ANY)],
        out_specs=pl.BlockSpec(memory_space=pl.ANY),
        out_shape=jax.ShapeDtypeStruct((MOE_ROWS, D_MODEL // 2), jnp.uint32),
        scratch_shapes=[pltpu.SemaphoreType.DMA(())],
        input_output_aliases={2: 0},
        compiler_params=pltpu.CompilerParams(dimension_semantics=("arbitrary",), has_side_effects=True),
        name="moe_dispatch_rows",
    )(pos.reshape(-1), hx_words, init)


def _rope_tables():
    t = jnp.arange(SEQ)
    row = (t // GRID_W).astype(F32)
    col = (t % GRID_W).astype(F32)
    axis_dim = HEAD_DIM // 2
    inv = ROPE_THETA ** (-jnp.arange(0, axis_dim, 2, dtype=F32) / axis_dim)
    ar = row[:, None] * inv
    ac = col[:, None] * inv
    cos = jnp.concatenate([jnp.cos(ar), jnp.cos(ar), jnp.cos(ac), jnp.cos(ac)], axis=-1)
    sin = jnp.concatenate([jnp.sin(ar), jnp.sin(ar), jnp.sin(ac), jnp.sin(ac)], axis=-1)
    odd = ((jnp.arange(HEAD_DIM) // 32) % 2 == 1)[None, :]
    sa = jnp.where(odd, sin, 0.0)
    sb = jnp.where(odd, 0.0, -sin)
    lat = lambda z: jnp.tile(z, (BATCH, 1))
    cos_all = jnp.concatenate([lat(cos), jnp.ones((N_CTX, HEAD_DIM), F32)], axis=0)
    sa_all = jnp.concatenate([lat(sa), jnp.zeros((N_CTX, HEAD_DIM), F32)], axis=0)
    sb_all = jnp.concatenate([lat(sb), jnp.zeros((N_CTX, HEAD_DIM), F32)], axis=0)
    return cos_all, sa_all, sb_all


def kernel(x, c, ctx, c_ctx, w_mod, b_mod, w_in, w_out, gmlp_ln_g, gmlp_ln_b, gmlp_ws, gmlp_bs,
           diff_lam, diff_subln_g, hgrn_lb, hgrn_norm_g, ln1_g, ln1_b, ln2_g, ln2_b,
           router_w, router_b, exp_w_gate, exp_w_up, exp_w_down, sh_w_gate, sh_w_up, sh_w_down):
    assert x.shape == (BATCH, SEQ, D_MODEL) and ctx.shape == (BATCH, CTX_LEN, D_MODEL)
    xs_all = jnp.concatenate([x.reshape(N_LAT, D_MODEL), ctx.reshape(N_CTX, D_MODEL)], axis=0)

    cond = jnp.zeros((8, D_MODEL), F32).at[0:BATCH].set(c).at[BATCH].set(c_ctx)
    mod_all = _modulation(cond, w_mod, b_mod)[:, :BATCH + 1].reshape(DEPTH, BATCH + 1, 6, D_MODEL)

    sm = jax.nn.softmax(hgrn_lb.astype(F32), axis=0)
    lb_all = jnp.cumsum(sm, axis=0) - sm[0]
    tables = _rope_tables()
    xmat_np, masks_np = _scan_structure()
    xmat = jnp.asarray(xmat_np, BF16)
    masks = jnp.asarray(masks_np, F32)

    for l in range(DEPTH):
        mod3 = mod_all[l]
        w_in_l = w_in[l].astype(BF16)
        p_lo = _inproj(xs_all, mod3, w_in_l[:, :LOW_WIDTH], tables, BF16, True)
        p_hi = _inproj(xs_all, mod3, w_in_l[:, LOW_WIDTH:], tables, F32, False)

        a_x = _gmlp(p_lo, gmlp_ln_g[l][None, :], gmlp_ln_b[l][None, :],
                    gmlp_ws[l].astype(BF16), gmlp_bs[l].T)

        lam_init = 0.8 - 0.6 * math.exp(-0.3 * l)
        dl = diff_lam[l].astype(F32)
        lam = jnp.exp(jnp.sum(dl[0] * dl[1])) - jnp.exp(jnp.sum(dl[2] * dl[3])) + lam_init
        lam2 = jnp.stack([lam, jnp.asarray(1.0 - lam_init, F32)]).astype(F32)
        b_x = _diff_attention(p_lo, lam2, diff_subln_g[l][None, :])

        o_f, o_b = _hgrn_scan(p_hi, lb_all[l], xmat, masks)

        x1, hx, hx_words, idx, wt, rank, cnt = _outproj(
            a_x, b_x, o_f, o_b, p_hi, hgrn_norm_g[l][None, :], xs_all, mod3,
            w_out[l].astype(BF16), ln1_g[l][None, :], ln1_b[l][None, :],
            router_w[l].astype(BF16), router_b[l][None, :])

        pos, tile_expert, n_tiles = _layout(idx[:, :TOP_K], rank[:, :TOP_K], cnt[0].astype(jnp.int32))
        xs_grouped = _dispatch_rows(pos, hx_words)
        ys = _experts(tile_expert, n_tiles, xs_grouped, exp_w_gate, exp_w_up, exp_w_down, l)
        pos_blocks = pos.reshape(T_ROWS // COMBINE_TM, COMBINE_TM, TOP_K).transpose(0, 2, 1).reshape(-1)

        ysh = _experts(jnp.zeros((T_ROWS // MOE_TM,), jnp.int32),
                       jnp.full((1,), T_ROWS // MOE_TM, jnp.int32),
                       hx, sh_w_gate[:, None], sh_w_up[:, None], sh_w_down[:, None], l)

        xs_all = _combine(ys, pos_blocks, wt, ysh, x1, mod3, ln2_g[l][None, :], ln2_b[l][None, :])

    return xs_all[:N_LAT].reshape(BATCH, SEQ, D_MODEL)
```

```python
import functools
import math

import numpy as np
import jax
import jax.numpy as jnp
from jax import lax
from jax.experimental import pallas as pl
from jax.experimental.pallas import tpu as pltpu

F32 = jnp.float32
BF16 = jnp.bfloat16

D_MODEL = 2048
BATCH = 2
SEQ = 4096
DEPTH = 4
CTX_LEN = 256
GRID_W = 64
HEAD_DIM = 128
A_WIDTH = 512
A_HEADS = 4
A_CHUNK = 128
B_WIDTH = 1024
B_HEADS = 4
B_QK_WIDTH = 1024
B_V_DIM = 256
ROPE_THETA = 10000.0
C_WIDTH = 512
C_HEADS = 4
IN_WIDTH = 6656
N_EXPERTS = 64
TOP_K = 8
EXPERT_DIM = 384
ROUTED_SCALE = 2.5
DEEPNORM_ALPHA = (2.0 * DEPTH) ** 0.25
LN_EPS = 1e-5
NORM_EPS = 1e-5

N_LAT = BATCH * SEQ
N_CTX = BATCH * CTX_LEN
T_ROWS = N_LAT + N_CTX
LOW_WIDTH = 4096
HI_WIDTH = IN_WIDTH - LOW_WIDTH
QK_SCALE = HEAD_DIM ** -0.5 * math.log2(math.e)

LANE = 128
PROJ_TM = 512
PROJ_TN = 512
ATT_TQ = 256
ATT_TK = 1024
SCAN_C = 128
SCAN_LEVELS = 7
OUT_TM = 256
MOE_TM = 512
COMBINE_TM = 128
DISPATCH_TM = 256
DMA_UNROLL = 8
MOE_ROWS = ((T_ROWS * TOP_K + N_EXPERTS * (MOE_TM - 1)) // MOE_TM) * MOE_TM
MOE_TILES = MOE_ROWS // MOE_TM
VMEM_LIMIT = 48 * 1024 * 1024


def _cparams(sem, vmem=VMEM_LIMIT):
    return pltpu.CompilerParams(dimension_semantics=sem, vmem_limit_bytes=vmem)


def _sigmoid(x):
    return 1.0 / (1.0 + jnp.exp(-x))


def _gelu_tanh(x):
    return 0.5 * x * (1.0 + jnp.tanh(math.sqrt(2.0 / math.pi) * (x + 0.044715 * (x * x * x))))


def _pack_pairs(x):
    n = x.shape[1] // 2
    bits = pltpu.bitcast(x.astype(BF16).astype(F32), jnp.uint32)
    return bits[:, n:] | (bits[:, :n] >> 16)


def _unpack_pairs(words):
    low = pltpu.bitcast(words << 16, F32)
    high = pltpu.bitcast(words & jnp.uint32(0xFFFF0000), F32)
    return low, high


def _dot_nt(a, b):
    return lax.dot_general(a, b, (((1,), (1,)), ((), ())), preferred_element_type=F32)


def _dot_tn(a, b):
    return lax.dot_general(a, b, (((0,), (0,)), ((), ())), preferred_element_type=F32)


def _mod_kernel(c_ref, w_ref, b_ref, o_ref):
    c = c_ref[...]
    s = (c * _sigmoid(c)).astype(BF16)
    o_ref[0] = jnp.dot(s, w_ref[0].astype(BF16), preferred_element_type=F32) + b_ref[0]


def _modulation(cond, w_mod, b_mod):
    tn = 1024
    return pl.pallas_call(
        _mod_kernel,
        grid=(DEPTH, 6 * D_MODEL // tn),
        in_specs=[pl.BlockSpec((8, D_MODEL), lambda l, n: (0, 0)),
                  pl.BlockSpec((1, D_MODEL, tn), lambda l, n: (l, 0, n)),
                  pl.BlockSpec((1, 1, tn), lambda l, n: (l, 0, n))],
        out_specs=pl.BlockSpec((1, 8, tn), lambda l, n: (l, 0, n)),
        out_shape=jax.ShapeDtypeStruct((DEPTH, 8, 6 * D_MODEL), F32),
        compiler_params=_cparams(("parallel", "parallel")),
        name="modulation",
    )(cond, w_mod, b_mod.reshape(DEPTH, 1, 6 * D_MODEL))


def _mod_row(m, tm):
    return jnp.minimum(m // (SEQ // tm), BATCH)


def _inproj_kernel(x_ref, mod_ref, w_ref, cos_ref, sa_ref, sb_ref, o_ref, h_scr, *, rope):
    n = pl.program_id(1)

    @pl.when(n == 0)
    def _():
        sh = mod_ref[0, 0:1, :]
        sc = mod_ref[0, 1:2, :]
        h_scr[...] = (x_ref[...] * (1.0 + sc) + sh).astype(BF16)

    acc = jnp.dot(h_scr[...], w_ref[...], preferred_element_type=F32)
    if not rope:
        o_ref[...] = acc.astype(o_ref.dtype)
        return

    is_qk = (n >= 2) & (n < 6)

    @pl.when(is_qk)
    def _():
        scale = jnp.where(n < 4, QK_SCALE, 1.0).astype(F32)
        cos = cos_ref[...]
        sa = sa_ref[...]
        sb = sb_ref[...]
        for g in range(PROJ_TN // LANE):
            blk = acc[:, g * LANE:(g + 1) * LANE]
            r = blk * cos + pltpu.roll(blk, 32, 1) * sa + pltpu.roll(blk, 96, 1) * sb
            o_ref[:, g * LANE:(g + 1) * LANE] = (r * scale).astype(o_ref.dtype)

    @pl.when(jnp.logical_not(is_qk))
    def _():
        o_ref[...] = acc.astype(o_ref.dtype)


def _inproj(x, mod3, w, tables, out_dtype, rope):
    width = w.shape[1]
    cos, sa, sb = tables
    tab_spec = pl.BlockSpec((PROJ_TM, LANE), lambda m, n: (m, 0))
    return pl.pallas_call(
        functools.partial(_inproj_kernel, rope=rope),
        grid=(T_ROWS // PROJ_TM, width // PROJ_TN),
        in_specs=[pl.BlockSpec((PROJ_TM, D_MODEL), lambda m, n: (m, 0)),
                  pl.BlockSpec((1, 6, D_MODEL), lambda m, n: (_mod_row(m, PROJ_TM), 0, 0)),
                  pl.BlockSpec((D_MODEL, PROJ_TN), lambda m, n: (0, n)),
                  tab_spec, tab_spec, tab_spec],
        out_specs=pl.BlockSpec((PROJ_TM, PROJ_TN), lambda m, n: (m, n)),
        out_shape=jax.ShapeDtypeStruct((T_ROWS, width), out_dtype),
        scratch_shapes=[pltpu.VMEM((PROJ_TM, D_MODEL), BF16)],
        compiler_params=_cparams(("parallel", "arbitrary")),
        name="inproj_rope" if rope else "inproj",
    )(x, mod3, w, cos, sa, sb)


def _gmlp_kernel(u_ref, v_ref, g_ref, b_ref, ws_ref, bs_ref, o_ref):
    for h in range(A_HEADS):
        sl = slice(h * HEAD_DIM, (h + 1) * HEAD_DIM)
        v = _gelu_tanh(v_ref[:, sl].astype(F32))
        mu = jnp.mean(v, axis=-1, keepdims=True)
        var = jnp.mean(jnp.square(v - mu), axis=-1, keepdims=True)
        vn = (v - mu) * lax.rsqrt(var + LN_EPS) * g_ref[:, sl] + b_ref[:, sl]
        s = jnp.dot(ws_ref[h], vn.astype(BF16), preferred_element_type=F32) + bs_ref[:, h:h + 1]
        u = _gelu_tanh(u_ref[:, sl].astype(F32))
        o_ref[:, sl] = (u * s).astype(o_ref.dtype)


def _gmlp(p_lo, ln_g, ln_b, ws, bs_t):
    return pl.pallas_call(
        _gmlp_kernel,
        grid=(T_ROWS // A_CHUNK,),
        in_specs=[pl.BlockSpec((A_CHUNK, A_WIDTH), lambda i: (i, 0)),
                  pl.BlockSpec((A_CHUNK, A_WIDTH), lambda i: (i, 1)),
                  pl.BlockSpec((1, A_WIDTH), lambda i: (0, 0)),
                  pl.BlockSpec((1, A_WIDTH), lambda i: (0, 0)),
                  pl.BlockSpec((A_HEADS, A_CHUNK, A_CHUNK), lambda i: (0, 0, 0)),
                  pl.BlockSpec((A_CHUNK, A_HEADS), lambda i: (0, 0))],
        out_specs=pl.BlockSpec((A_CHUNK, A_WIDTH), lambda i: (i, 0)),
        out_shape=jax.ShapeDtypeStruct((T_ROWS, A_WIDTH), BF16),
        compiler_params=_cparams(("parallel",)),
        name="gmlp",
    )(p_lo, p_lo, ln_g, ln_b, ws, bs_t)


NQ_LAT = SEQ // ATT_TQ


def _attn_kernel(lam_ref, q_ref, kl_ref, vl_ref, kc_ref, vc_ref, g_ref, o_ref, m_scr, l_scr, acc_scr):
    qi = pl.program_id(2)
    m_scr[...] = jnp.full(m_scr.shape, -1e30, F32)
    l_scr[...] = jnp.zeros(l_scr.shape, F32)
    acc_scr[...] = jnp.zeros(acc_scr.shape, F32)

    def process(k, v):
        for mp in range(2):
            sl = slice(mp * HEAD_DIM, (mp + 1) * HEAD_DIM)
            s = _dot_nt(q_ref[:, sl], k[:, sl])
            cols = [s[:, c * LANE:(c + 1) * LANE] for c in range(k.shape[0] // LANE)]
            m_part = cols[0]
            for col in cols[1:]:
                m_part = jnp.maximum(m_part, col)
            m_old = m_scr[mp]
            m_new = jnp.maximum(m_old, jnp.max(m_part, axis=-1, keepdims=True))
            alpha = jnp.exp2(m_old - m_new)
            l_part = alpha * l_scr[mp]
            ps = []
            for col in cols:
                p = jnp.exp2(col - m_new)
                l_part = l_part + p
                ps.append(p.astype(BF16))
            pv = jnp.dot(jnp.concatenate(ps, axis=1), v, preferred_element_type=F32)
            for half in range(B_V_DIM // LANE):
                hs = slice(half * LANE, (half + 1) * LANE)
                acc_scr[mp, :, hs] = alpha * acc_scr[mp, :, hs] + pv[:, hs]
            l_scr[mp] = l_part
            m_scr[mp] = m_new

    process(kc_ref[...], vc_ref[...])

    @pl.when(qi < NQ_LAT)
    def _():
        for j in range(SEQ // ATT_TK):
            rows = slice(j * ATT_TK, (j + 1) * ATT_TK)
            process(kl_ref[rows, :], vl_ref[rows, :])

    lam = lam_ref[0]
    post = lam_ref[1]
    l0 = jnp.sum(l_scr[0], axis=-1, keepdims=True)
    l1 = jnp.sum(l_scr[1], axis=-1, keepdims=True)
    o = acc_scr[0] / l0 - lam * (acc_scr[1] / l1)
    r = o * lax.rsqrt(jnp.mean(jnp.square(o), axis=-1, keepdims=True) + NORM_EPS)
    o_ref[...] = (r * g_ref[...] * post).astype(o_ref.dtype)


def _diff_attention(p_lo, lam2, subln_g):
    def q_rows(b, h, qi):
        return jnp.where(qi < NQ_LAT, b * NQ_LAT + qi, BATCH * NQ_LAT + b)

    return pl.pallas_call(
        _attn_kernel,
        grid=(BATCH, B_HEADS, NQ_LAT + 1),
        in_specs=[pl.BlockSpec(memory_space=pltpu.SMEM),
                  pl.BlockSpec((ATT_TQ, B_V_DIM), lambda b, h, qi: (q_rows(b, h, qi), 4 + h)),
                  pl.BlockSpec((SEQ, B_V_DIM), lambda b, h, qi: (b, 8 + h)),
                  pl.BlockSpec((SEQ, B_V_DIM), lambda b, h, qi: (b, 12 + h)),
                  pl.BlockSpec((CTX_LEN, B_V_DIM), lambda b, h, qi: (N_LAT // CTX_LEN + b, 8 + h)),
                  pl.BlockSpec((CTX_LEN, B_V_DIM), lambda b, h, qi: (N_LAT // CTX_LEN + b, 12 + h)),
                  pl.BlockSpec((1, B_V_DIM), lambda b, h, qi: (0, 0))],
        out_specs=pl.BlockSpec((ATT_TQ, B_V_DIM), lambda b, h, qi: (q_rows(b, h, qi), h)),
        scratch_shapes=[pltpu.VMEM((2, ATT_TQ, LANE), F32),
                        pltpu.VMEM((2, ATT_TQ, LANE), F32),
                        pltpu.VMEM((2, ATT_TQ, B_V_DIM), F32)],
        out_shape=jax.ShapeDtypeStruct((T_ROWS, B_WIDTH), BF16),
        compiler_params=_cparams(("parallel", "parallel", "arbitrary")),
        name="diff_attention",
    )(lam2, p_lo, p_lo, p_lo, p_lo, p_lo, subln_g)


def _scan_structure():
    c = SCAN_C
    x = np.zeros((2, (2 + SCAN_LEVELS) * c, c), np.float32)
    msk = np.zeros((2, SCAN_LEVELS + 1, c, c), np.float32)
    r = np.arange(c)
    j = np.arange(c)[None, :]
    xf = np.zeros(((2 + SCAN_LEVELS) * c, c), np.float32)
    mf = np.zeros((SCAN_LEVELS + 1, c, c), np.float32)
    xf[0:c] = (j <= r[:, None])
    xf[c:2 * c] = (j > r[:, None])
    for lev in range(SCAN_LEVELS):
        half = (c // 2) >> lev
        start = (r // (2 * half)) * (2 * half)
        mid = start + half - 1
        later = r > mid
        rows = np.where(later[:, None], (j > mid[:, None]) & (j <= r[:, None]),
                        (j > r[:, None]) & (j <= mid[:, None]))
        xf[(2 + lev) * c:(3 + lev) * c] = rows
        same = start[:, None] == start[None, :]
        mf[lev] = same & later[:, None] & (~later)[None, :]
    mf[SCAN_LEVELS] = np.eye(c)
    x[0] = xf
    msk[0] = mf
    x[1] = xf.reshape(2 + SCAN_LEVELS, c, c)[:, ::-1, ::-1].reshape(-1, c)
    msk[1] = mf[:, ::-1, ::-1]
    return x, msk


def _hgrn_kernel(qf_ref, ff_ref, if_ref, qb_ref, fb_ref, ib_ref, lb_ref, x_ref, msk_ref,
                 of_ref, ob_ref, st_scr):
    j = pl.program_id(1)

    @pl.when(j == 0)
    def _():
        st_scr[...] = jnp.zeros(st_scr.shape, F32)

    c = SCAN_C
    dirs = ((qf_ref, ff_ref, if_ref, of_ref, c - 1), (qb_ref, fb_ref, ib_ref, ob_ref, 0))
    for d, (q_ref, f_ref, i_ref, o_ref, end_row) in enumerate(dirs):
        for h in range(C_HEADS):
            sl = slice(h * HEAD_DIM, (h + 1) * HEAD_DIM)
            qraw = q_ref[:, sl]
            q = qraw * _sigmoid(qraw)
            lb = lb_ref[d:d + 1, sl]
            f = lb + (1.0 - lb) * _sigmoid(f_ref[:, sl])
            logf = jnp.log(f)
            k = 1.0 - f
            v = i_ref[:, sl].astype(BF16)
            hi = logf.astype(BF16)
            lo = (logf - hi.astype(F32)).astype(BF16)
            e2 = jnp.dot(x_ref[d], jnp.concatenate([hi, lo], axis=1), preferred_element_type=F32)
            w = jnp.exp(e2[:, 0:c] + e2[:, c:2 * c])
            a = msk_ref[d, SCAN_LEVELS] * _dot_nt(q.astype(BF16), k.astype(BF16))
            for lev in range(SCAN_LEVELS):
                wl = w[(2 + lev) * c:(3 + lev) * c]
                a = a + msk_ref[d, lev] * _dot_nt((q * wl).astype(BF16), (k * wl).astype(BF16))
            st = st_scr[d, h]
            o = jnp.dot(a.astype(BF16), v, preferred_element_type=F32)
            o = o + _dot_nt((q * w[0:c]).astype(BF16), st.astype(BF16))
            o_ref[:, sl] = o
            g_end = w[end_row:end_row + 1, :]
            st_scr[d, h] = st * g_end + _dot_tn(v, (k * w[c:2 * c]).astype(BF16))


def _hgrn_scan(p_hi, lb2, xmat, masks):
    lat_chunks = SEQ // SCAN_C
    ctx_chunks = CTX_LEN // SCAN_C
    steps = ctx_chunks + lat_chunks

    def fwd_rows(b, j):
        return jnp.where(j < ctx_chunks, N_LAT // SCAN_C + b * ctx_chunks + j,
                         b * lat_chunks + j - ctx_chunks)

    def bwd_rows(b, j):
        return jnp.where(j < ctx_chunks, N_LAT // SCAN_C + b * ctx_chunks + (ctx_chunks - 1 - j),
                         b * lat_chunks + (steps - 1 - j))

    def spec(rows, col):
        return pl.BlockSpec((SCAN_C, C_WIDTH), lambda b, j: (rows(b, j), col))

    nx = (2 + SCAN_LEVELS) * SCAN_C
    return pl.pallas_call(
        _hgrn_kernel,
        grid=(BATCH, steps),
        in_specs=[spec(fwd_rows, 0), spec(fwd_rows, 1), spec(fwd_rows, 3),
                  spec(bwd_rows, 0), spec(bwd_rows, 2), spec(bwd_rows, 3),
                  pl.BlockSpec((2, C_WIDTH), lambda b, j: (0, 0)),
                  pl.BlockSpec((2, nx, SCAN_C), lambda b, j: (0, 0, 0)),
                  pl.BlockSpec((2, SCAN_LEVELS + 1, SCAN_C, SCAN_C), lambda b, j: (0, 0, 0, 0))],
        out_specs=[spec(fwd_rows, 0), spec(bwd_rows, 0)],
        out_shape=[jax.ShapeDtypeStruct((T_ROWS, C_WIDTH), F32)] * 2,
        scratch_shapes=[pltpu.VMEM((2, C_HEADS, HEAD_DIM, HEAD_DIM), F32)],
        compiler_params=_cparams(("parallel", "arbitrary")),
        name="hgrn2_scan",
    )(p_hi, p_hi, p_hi, p_hi, p_hi, p_hi, lb2, xmat, masks)


def _layer_norm_rows(z, g, b):
    mu = jnp.mean(z, axis=-1, keepdims=True)
    var = jnp.mean(jnp.square(z - mu), axis=-1, keepdims=True)
    return (z - mu) * lax.rsqrt(var + LN_EPS) * g + b


def _outproj_kernel(a_ref, b_ref, of_ref, ob_ref, cg_ref, hg_ref, x_ref, mod_ref, w_ref,
                    l1g_ref, l1b_ref, rw_ref, rb_ref,
                    x1_ref, hx_ref, hxp_ref, idx_ref, wt_ref, rank_ref, cnt_ref, cnt_scr):
    parts = []
    for h in range(C_HEADS):
        sl = slice(h * HEAD_DIM, (h + 1) * HEAD_DIM)
        o = of_ref[:, sl] + ob_ref[:, sl]
        on = o * lax.rsqrt(jnp.mean(jnp.square(o), axis=-1, keepdims=True) + NORM_EPS) * hg_ref[...]
        g = cg_ref[:, sl]
        parts.append((on * (g * _sigmoid(g))).astype(BF16))
    c_x = jnp.concatenate(parts, axis=1)
    mix = jnp.dot(a_ref[...], w_ref[0:A_WIDTH, :], preferred_element_type=F32)
    mix = mix + jnp.dot(b_ref[...], w_ref[A_WIDTH:A_WIDTH + B_WIDTH, :], preferred_element_type=F32)
    mix = mix + jnp.dot(c_x, w_ref[A_WIDTH + B_WIDTH:, :], preferred_element_type=F32)
    g1 = mod_ref[0, 2:3, :]
    sh2 = mod_ref[0, 3:4, :]
    sc2 = mod_ref[0, 4:5, :]
    x1 = _layer_norm_rows(DEEPNORM_ALPHA * x_ref[...] + g1 * mix, l1g_ref[...], l1b_ref[...])
    x1_ref[...] = x1
    hx = x1 * (1.0 + sc2) + sh2
    hx_bf = hx.astype(BF16)
    hx_ref[...] = hx_bf
    hxp_ref[...] = _pack_pairs(hx)

    logits = jnp.dot(hx.astype(BF16), rw_ref[...], preferred_element_type=F32)
    scores = _sigmoid(logits)
    sel = scores + rb_ref[...]
    tm = sel.shape[0]
    lane = lax.broadcasted_iota(jnp.int32, sel.shape, 1).astype(F32)
    slot = lax.broadcasted_iota(jnp.int32, (tm, LANE), 1)
    idx_acc = jnp.zeros((tm, LANE), F32)
    wt_acc = jnp.zeros((tm, LANE), F32)
    chosen = jnp.zeros(sel.shape, F32)
    hits = []
    for k in range(TOP_K):
        mx = jnp.max(sel, axis=-1, keepdims=True)
        idx = jnp.min(jnp.where(sel == mx, lane, float(N_EXPERTS)), axis=-1, keepdims=True)
        hit = lane == idx
        hits.append(hit)
        chosen = jnp.where(hit, 1.0, chosen)
        w_k = jnp.sum(jnp.where(hit, scores, 0.0), axis=-1, keepdims=True)
        idx_acc = jnp.where(slot == k, idx, idx_acc)
        wt_acc = jnp.where(slot == k, w_k, wt_acc)
        sel = jnp.where(hit, -jnp.inf, sel)
    idx_ref[...] = idx_acc.astype(jnp.int32)
    wt_ref[...] = wt_acc / jnp.sum(wt_acc, axis=-1, keepdims=True) * ROUTED_SCALE

    @pl.when(pl.program_id(0) == 0)
    def _():
        cnt_scr[...] = jnp.zeros(cnt_scr.shape, F32)

    below = (lax.broadcasted_iota(jnp.int32, (tm, tm), 0) > lax.broadcasted_iota(jnp.int32, (tm, tm), 1))
    before = jnp.dot(below.astype(BF16), chosen.astype(BF16), preferred_element_type=F32) + cnt_scr[0:1, :]
    rank_acc = jnp.zeros((tm, LANE), F32)
    for k in range(TOP_K):
        r_k = jnp.sum(jnp.where(hits[k], before, 0.0), axis=-1, keepdims=True)
        rank_acc = jnp.where(slot == k, r_k, rank_acc)
    rank_ref[...] = rank_acc.astype(jnp.int32)
    cnt = cnt_scr[...] + jnp.sum(chosen, axis=0, keepdims=True)
    cnt_scr[...] = cnt
    cnt_ref[...] = cnt


def _outproj(a_x, b_x, o_f, o_b, p_hi, hnorm_g, x, mod3, w_out, ln_g, ln_b, router_w, router_b):
    row = lambda width: pl.BlockSpec((OUT_TM, width), lambda i: (i, 0))
    const = lambda shape: pl.BlockSpec(shape, lambda i: tuple(0 for _ in shape))
    return pl.pallas_call(
        _outproj_kernel,
        grid=(T_ROWS // OUT_TM,),
        in_specs=[row(A_WIDTH), row(B_WIDTH), row(C_WIDTH), row(C_WIDTH),
                  pl.BlockSpec((OUT_TM, C_WIDTH), lambda i: (i, 4)),
                  const((1, HEAD_DIM)),
                  row(D_MODEL),
                  pl.BlockSpec((1, 6, D_MODEL), lambda i: (_mod_row(i, OUT_TM), 0, 0)),
                  const((D_MODEL, D_MODEL)),
                  const((1, D_MODEL)), const((1, D_MODEL)),
                  const((D_MODEL, N_EXPERTS)), const((1, N_EXPERTS))],
        out_specs=[row(D_MODEL), row(D_MODEL), row(D_MODEL // 2), row(LANE), row(LANE), row(LANE),
                   const((8, N_EXPERTS))],
        out_shape=[jax.ShapeDtypeStruct((T_ROWS, D_MODEL), F32),
                   jax.ShapeDtypeStruct((T_ROWS, D_MODEL), BF16),
                   jax.ShapeDtypeStruct((T_ROWS, D_MODEL // 2), jnp.uint32),
                   jax.ShapeDtypeStruct((T_ROWS, LANE), jnp.int32),
                   jax.ShapeDtypeStruct((T_ROWS, LANE), F32),
                   jax.ShapeDtypeStruct((T_ROWS, LANE), jnp.int32),
                   jax.ShapeDtypeStruct((8, N_EXPERTS), F32)],
        scratch_shapes=[pltpu.VMEM((8, N_EXPERTS), F32)],
        compiler_params=_cparams(("arbitrary",)),
        name="outproj_ln_router",
    )(a_x, b_x, o_f, o_b, p_hi, hnorm_g, x, mod3, w_out, ln_g, ln_b, router_w, router_b)


def _experts_kernel(te_ref, nt_ref, xs_ref, wg_ref, wu_ref, wd_ref, ys_ref, wgu_scr, wd_scr, *, packed):
    i = pl.program_id(0)
    live = i < nt_ref[0]
    new_expert = jnp.logical_or(i == 0, te_ref[i] != te_ref[jnp.maximum(i - 1, 0)])

    @pl.when(jnp.logical_and(live, new_expert))
    def _():
        wgu_scr[:, :EXPERT_DIM] = wg_ref[0, 0].astype(BF16)
        wgu_scr[:, EXPERT_DIM:] = wu_ref[0, 0].astype(BF16)
        wd_scr[...] = wd_ref[0, 0].astype(BF16)

    @pl.when(live)
    def _():
        if packed:
            low, high = _unpack_pairs(xs_ref[...])
            x = jnp.concatenate([low.astype(BF16), high.astype(BF16)], axis=1)
        else:
            x = xs_ref[...]
        h = jnp.dot(x, wgu_scr[...], preferred_element_type=F32)
        g = h[:, :EXPERT_DIM]
        u = h[:, EXPERT_DIM:]
        a = (g * _sigmoid(g) * u).astype(BF16)
        y = jnp.dot(a, wd_scr[...], preferred_element_type=F32)
        ys_ref[...] = _pack_pairs(y) if packed else y.astype(ys_ref.dtype)

    @pl.when(jnp.logical_not(live))
    def _():
        ys_ref[...] = jnp.zeros(ys_ref.shape, ys_ref.dtype)


def _experts(tile_expert, n_tiles, xs, w_gate, w_up, w_down, layer):
    tiles = xs.shape[0] // MOE_TM
    last = lambda i, nt: jnp.minimum(i, nt[0] - 1)
    grid_spec = pltpu.PrefetchScalarGridSpec(
        num_scalar_prefetch=2,
        grid=(tiles,),
        in_specs=[pl.BlockSpec((MOE_TM, xs.shape[1]), lambda i, te, nt: (last(i, nt), 0)),
                  pl.BlockSpec((1, 1, D_MODEL, EXPERT_DIM), lambda i, te, nt: (layer, te[i], 0, 0)),
                  pl.BlockSpec((1, 1, D_MODEL, EXPERT_DIM), lambda i, te, nt: (layer, te[i], 0, 0)),
                  pl.BlockSpec((1, 1, EXPERT_DIM, D_MODEL), lambda i, te, nt: (layer, te[i], 0, 0))],
        out_specs=pl.BlockSpec((MOE_TM, xs.shape[1]), lambda i, te, nt: (i, 0)),
        scratch_shapes=[pltpu.VMEM((D_MODEL, 2 * EXPERT_DIM), BF16),
                        pltpu.VMEM((EXPERT_DIM, D_MODEL), BF16)],
    )
    return pl.pallas_call(
        functools.partial(_experts_kernel, packed=xs.dtype == jnp.uint32),
        grid_spec=grid_spec,
        out_shape=jax.ShapeDtypeStruct(xs.shape, xs.dtype),
        compiler_params=_cparams(("arbitrary",)),
        name="experts",
    )(tile_expert, n_tiles, xs, w_gate, w_up, w_down)


def _combine_kernel(pos_ref, wt_ref, ysh_ref, x1_ref, mod_ref, g_ref, b_ref, ys_hbm, o_ref, buf, sem):
    tm = COMBINE_TM
    n_rows = TOP_K * tm

    def row_copy(j):
        return pltpu.make_async_copy(ys_hbm.at[pl.ds(pos_ref[j], 1), :], buf.at[pl.ds(j, 1), :], sem)

    def issue(g, carry):
        for u in range(DMA_UNROLL):
            row_copy(g * DMA_UNROLL + u).start(priority=u % 2)
        return carry

    lax.fori_loop(0, n_rows // DMA_UNROLL, issue, 0)
    pltpu.make_async_copy(ys_hbm.at[pl.ds(0, n_rows), :], buf, sem).wait()

    low = jnp.zeros((tm, D_MODEL // 2), F32)
    high = jnp.zeros((tm, D_MODEL // 2), F32)
    for k in range(TOP_K):
        lo_k, hi_k = _unpack_pairs(buf[k * tm:(k + 1) * tm, :])
        low = low + wt_ref[:, k:k + 1] * lo_k
        high = high + wt_ref[:, k:k + 1] * hi_k
    y = ysh_ref[...].astype(F32) + jnp.concatenate([low, high], axis=1)
    g2 = mod_ref[0, 5:6, :]
    o_ref[...] = _layer_norm_rows(DEEPNORM_ALPHA * x1_ref[...] + g2 * y, g_ref[...], b_ref[...])


def _combine(ys, pos_blocks, wt, ysh, x1, mod3, ln_g, ln_b):
    tm = COMBINE_TM
    row = lambda width: pl.BlockSpec((tm, width), lambda i: (i, 0))
    return pl.pallas_call(
        _combine_kernel,
        grid=(T_ROWS // tm,),
        in_specs=[pl.BlockSpec((TOP_K * tm,), lambda i: (i,), memory_space=pltpu.SMEM),
                  row(LANE), row(D_MODEL), row(D_MODEL),
                  pl.BlockSpec((1, 6, D_MODEL), lambda i: (_mod_row(i, tm), 0, 0)),
                  pl.BlockSpec((1, D_MODEL), lambda i: (0, 0)),
                  pl.BlockSpec((1, D_MODEL), lambda i: (0, 0)),
                  pl.BlockSpec(memory_space=pl.ANY)],
        out_specs=row(D_MODEL),
        out_shape=jax.ShapeDtypeStruct((T_ROWS, D_MODEL), F32),
        scratch_shapes=[pltpu.VMEM((TOP_K * tm, D_MODEL // 2), jnp.uint32), pltpu.SemaphoreType.DMA(())],
        compiler_params=_cparams(("arbitrary",)),
        name="moe_combine_ln",
    )(pos_blocks, wt, ysh, x1, mod3, ln_g, ln_b, ys)


def _layout(idx, rank, counts):
    tiles_per = (counts + MOE_TM - 1) // MOE_TM
    tile_end = jnp.cumsum(tiles_per)
    start = (tile_end - tiles_per) * MOE_TM
    experts = jnp.arange(N_EXPERTS, dtype=jnp.int32)
    pos = rank + jnp.sum(jnp.where(idx[:, :, None] == experts, start, 0), axis=-1)
    n_tiles = tile_end[-1:].astype(jnp.int32)
    tile = jnp.arange(MOE_TILES, dtype=jnp.int32)
    tile_expert = jnp.sum((tile_end[None, :] <= tile[:, None]).astype(jnp.int32), axis=-1)
    tile_expert = jnp.minimum(tile_expert, N_EXPERTS - 1)
    return pos, tile_expert, n_tiles


def _dispatch_kernel(pos_ref, hxw_ref, init_hbm, xs_hbm, sem):
    def issue(t, carry):
        for k in range(TOP_K):
            pltpu.make_async_copy(hxw_ref.at[pl.ds(t, 1), :],
                                  xs_hbm.at[pl.ds(pos_ref[t * TOP_K + k], 1), :], sem).start(priority=k % 2)
        return carry

    lax.fori_loop(0, DISPATCH_TM, issue, 0)
    for _ in range(TOP_K):
        pltpu.make_async_copy(hxw_ref, xs_hbm.at[pl.ds(0, DISPATCH_TM), :], sem).wait()


def _dispatch_rows(pos, hx_words):
    init = jnp.zeros((MOE_ROWS, D_MODEL // 2), jnp.uint32)
    return pl.pallas_call(
        _dispatch_kernel,
        grid=(T_ROWS // DISPATCH_TM,),
        in_specs=[pl.BlockSpec((DISPATCH_TM * TOP_K,), lambda i: (i,), memory_space=pltpu.SMEM),
                  pl.BlockSpec((DISPATCH_TM, D_MODEL // 2), lambda i: (i, 0)),
                  pl.BlockSpec(memory_space=pl.ANY)],
        out_specs=pl.BlockSpec(memory_space=pl.ANY),
        out_shape=jax.ShapeDtypeStruct((MOE_ROWS, D_MODEL // 2), jnp.uint32),
        scratch_shapes=[pltpu.SemaphoreType.DMA(())],
        input_output_aliases={2: 0},
        compiler_params=pltpu.CompilerParams(dimension_semantics=("arbitrary",), has_side_effects=True),
        name="moe_dispatch_rows",
    )(pos.reshape(-1), hx_words, init)


def _rope_tables():
    t = jnp.arange(SEQ)
    row = (t // GRID_W).astype(F32)
    col = (t % GRID_W).astype(F32)
    axis_dim = HEAD_DIM // 2
    inv = ROPE_THETA ** (-jnp.arange(0, axis_dim, 2, dtype=F32) / axis_dim)
    ar = row[:, None] * inv
    ac = col[:, None] * inv
    cos = jnp.concatenate([jnp.cos(ar), jnp.cos(ar), jnp.cos(ac), jnp.cos(ac)], axis=-1)
    sin = jnp.concatenate([jnp.sin(ar), jnp.sin(ar), jnp.sin(ac), jnp.sin(ac)], axis=-1)
    odd = ((jnp.arange(HEAD_DIM) // 32) % 2 == 1)[None, :]
    sa = jnp.where(odd, sin, 0.0)
    sb = jnp.where(odd, 0.0, -sin)
    lat = lambda z: jnp.tile(z, (BATCH, 1))
    cos_all = jnp.concatenate([lat(cos), jnp.ones((N_CTX, HEAD_DIM), F32)], axis=0)
    sa_all = jnp.concatenate([lat(sa), jnp.zeros((N_CTX, HEAD_DIM), F32)], axis=0)
    sb_all = jnp.concatenate([lat(sb), jnp.zeros((N_CTX, HEAD_DIM), F32)], axis=0)
    return cos_all, sa_all, sb_all


def kernel(x, c, ctx, c_ctx, w_mod, b_mod, w_in, w_out, gmlp_ln_g, gmlp_ln_b, gmlp_ws, gmlp_bs,
           diff_lam, diff_subln_g, hgrn_lb, hgrn_norm_g, ln1_g, ln1_b, ln2_g, ln2_b,
           router_w, router_b, exp_w_gate, exp_w_up, exp_w_down, sh_w_gate, sh_w_up, sh_w_down):
    assert x.shape == (BATCH, SEQ, D_MODEL) and ctx.shape == (BATCH, CTX_LEN, D_MODEL)
    xs_all = jnp.concatenate([x.reshape(N_LAT, D_MODEL), ctx.reshape(N_CTX, D_MODEL)], axis=0)

    cond = jnp.zeros((8, D_MODEL), F32).at[0:BATCH].set(c).at[BATCH].set(c_ctx)
    mod_all = _modulation(cond, w_mod, b_mod)[:, :BATCH + 1].reshape(DEPTH, BATCH + 1, 6, D_MODEL)

    sm = jax.nn.softmax(hgrn_lb.astype(F32), axis=0)
    lb_all = jnp.cumsum(sm, axis=0) - sm[0]
    tables = _rope_tables()
    xmat_np, masks_np = _scan_structure()
    xmat = jnp.asarray(xmat_np, BF16)
    masks = jnp.asarray(masks_np, F32)

    for l in range(DEPTH):
        mod3 = mod_all[l]
        w_in_l = w_in[l].astype(BF16)
        p_lo = _inproj(xs_all, mod3, w_in_l[:, :LOW_WIDTH], tables, BF16, True)
        p_hi = _inproj(xs_all, mod3, w_in_l[:, LOW_WIDTH:], tables, F32, False)

        a_x = _gmlp(p_lo, gmlp_ln_g[l][None, :], gmlp_ln_b[l][None, :],
                    gmlp_ws[l].astype(BF16), gmlp_bs[l].T)

        lam_init = 0.8 - 0.6 * math.exp(-0.3 * l)
        dl = diff_lam[l].astype(F32)
        lam = jnp.exp(jnp.sum(dl[0] * dl[1])) - jnp.exp(jnp.sum(dl[2] * dl[3])) + lam_init
        lam2 = jnp.stack([lam, jnp.asarray(1.0 - lam_init, F32)]).astype(F32)
        b_x = _diff_attention(p_lo, lam2, diff_subln_g[l][None, :])

        o_f, o_b = _hgrn_scan(p_hi, lb_all[l], xmat, masks)

        x1, hx, hx_words, idx, wt, rank, cnt = _outproj(
            a_x, b_x, o_f, o_b, p_hi, hgrn_norm_g[l][None, :], xs_all, mod3,
            w_out[l].astype(BF16), ln1_g[l][None, :], ln1_b[l][None, :],
            router_w[l].astype(BF16), router_b[l][None, :])

        pos, tile_expert, n_tiles = _layout(idx[:, :TOP_K], rank[:, :TOP_K], cnt[0].astype(jnp.int32))
        xs_grouped = _dispatch_rows(pos, hx_words)
        ys = _experts(tile_expert, n_tiles, xs_grouped, exp_w_gate, exp_w_up, exp_w_down, l)
        pos_blocks = pos.reshape(T_ROWS // COMBINE_TM, COMBINE_TM, TOP_K).transpose(0, 2, 1).reshape(-1)

        ysh = _experts(jnp.zeros((T_ROWS // MOE_TM,), jnp.int32),
                       jnp.full((1,), T_ROWS // MOE_TM, jnp.int32),
                       hx, sh_w_gate[:, None], sh_w_up[:, None], sh_w_down[:, None], l)

        xs_all = _combine(ys, pos_blocks, wt, ysh, x1, mod3, ln2_g[l][None, :], ln2_b[l][None, :])

    return xs_all[:N_LAT].reshape(BATCH, SEQ, D_MODEL)
```

```python
import functools
import math

import numpy as np
import jax
import jax.numpy as jnp
from jax import lax
from jax.experimental import pallas as pl
from jax.experimental.pallas import tpu as pltpu

F32 = jnp.float32
BF16 = jnp.bfloat16

D_MODEL = 2048
BATCH = 2
SEQ = 4096
DEPTH = 4
CTX_LEN = 256
GRID_W = 64
HEAD_DIM = 128
A_WIDTH = 512
A_HEADS = 4
A_CHUNK = 128
B_WIDTH = 1024
B_HEADS = 4
B_QK_WIDTH = 1024
B_V_DIM = 256
ROPE_THETA = 10000.0
C_WIDTH = 512
C_HEADS = 4
IN_WIDTH = 6656
N_EXPERTS = 64
TOP_K = 8
EXPERT_DIM = 384
ROUTED_SCALE = 2.5
DEEPNORM_ALPHA = (2.0 * DEPTH) ** 0.25
LN_EPS = 1e-5
NORM_EPS = 1e-5

N_LAT = BATCH * SEQ
N_CTX = BATCH * CTX_LEN
T_ROWS = N_LAT + N_CTX
LOW_WIDTH = 4096
HI_WIDTH = IN_WIDTH - LOW_WIDTH
QK_SCALE = HEAD_DIM ** -0.5 * math.log2(math.e)

LANE = 128
PROJ_TM = 512
PROJ_TN = 512
ATT_TQ = 256
ATT_TK = 1024
SCAN_C = 128
SCAN_LEVELS = 7
OUT_TM = 256
MOE_TM = 512
COMBINE_TM = 256
DISPATCH_TM = 256
MOE_ROWS = ((T_ROWS * TOP_K + N_EXPERTS * (MOE_TM - 1)) // MOE_TM) * MOE_TM
MOE_TILES = MOE_ROWS // MOE_TM
VMEM_LIMIT = 48 * 1024 * 1024


def _cparams(sem, vmem=VMEM_LIMIT):
    return pltpu.CompilerParams(dimension_semantics=sem, vmem_limit_bytes=vmem)


def _sigmoid(x):
    return 1.0 / (1.0 + jnp.exp(-x))


def _gelu_tanh(x):
    return 0.5 * x * (1.0 + jnp.tanh(math.sqrt(2.0 / math.pi) * (x + 0.044715 * (x * x * x))))


def _pack_pairs(x):
    n = x.shape[1] // 2
    bits = pltpu.bitcast(x.astype(BF16).astype(F32), jnp.uint32)
    return bits[:, n:] | (bits[:, :n] >> 16)


def _unpack_pairs(words):
    low = pltpu.bitcast(words << 16, F32)
    high = pltpu.bitcast(words & jnp.uint32(0xFFFF0000), F32)
    return low, high


def _dot_nt(a, b):
    return lax.dot_general(a, b, (((1,), (1,)), ((), ())), preferred_element_type=F32)


def _dot_tn(a, b):
    return lax.dot_general(a, b, (((0,), (0,)), ((), ())), preferred_element_type=F32)


def _mod_kernel(c_ref, w_ref, b_ref, o_ref):
    c = c_ref[...]
    s = (c * _sigmoid(c)).astype(BF16)
    o_ref[0] = jnp.dot(s, w_ref[0].astype(BF16), preferred_element_type=F32) + b_ref[0]


def _modulation(cond, w_mod, b_mod):
    tn = 1024
    return pl.pallas_call(
        _mod_kernel,
        grid=(DEPTH, 6 * D_MODEL // tn),
        in_specs=[pl.BlockSpec((8, D_MODEL), lambda l, n: (0, 0)),
                  pl.BlockSpec((1, D_MODEL, tn), lambda l, n: (l, 0, n)),
                  pl.BlockSpec((1, 1, tn), lambda l, n: (l, 0, n))],
        out_specs=pl.BlockSpec((1, 8, tn), lambda l, n: (l, 0, n)),
        out_shape=jax.ShapeDtypeStruct((DEPTH, 8, 6 * D_MODEL), F32),
        compiler_params=_cparams(("parallel", "parallel")),
        name="modulation",
    )(cond, w_mod, b_mod.reshape(DEPTH, 1, 6 * D_MODEL))


def _mod_row(m, tm):
    return jnp.minimum(m // (SEQ // tm), BATCH)


def _inproj_kernel(x_ref, mod_ref, w_ref, cos_ref, sa_ref, sb_ref, o_ref, h_scr, *, rope):
    n = pl.program_id(1)

    @pl.when(n == 0)
    def _():
        sh = mod_ref[0, 0:1, :]
        sc = mod_ref[0, 1:2, :]
        h_scr[...] = (x_ref[...] * (1.0 + sc) + sh).astype(BF16)

    acc = jnp.dot(h_scr[...], w_ref[...], preferred_element_type=F32)
    if not rope:
        o_ref[...] = acc.astype(o_ref.dtype)
        return

    is_qk = (n >= 2) & (n < 6)

    @pl.when(is_qk)
    def _():
        scale = jnp.where(n < 4, QK_SCALE, 1.0).astype(F32)
        cos = cos_ref[...]
        sa = sa_ref[...]
        sb = sb_ref[...]
        for g in range(PROJ_TN // LANE):
            blk = acc[:, g * LANE:(g + 1) * LANE]
            r = blk * cos + pltpu.roll(blk, 32, 1) * sa + pltpu.roll(blk, 96, 1) * sb
            o_ref[:, g * LANE:(g + 1) * LANE] = (r * scale).astype(o_ref.dtype)

    @pl.when(jnp.logical_not(is_qk))
    def _():
        o_ref[...] = acc.astype(o_ref.dtype)


def _inproj(x, mod3, w, tables, out_dtype, rope):
    width = w.shape[1]
    cos, sa, sb = tables
    tab_spec = pl.BlockSpec((PROJ_TM, LANE), lambda m, n: (m, 0))
    return pl.pallas_call(
        functools.partial(_inproj_kernel, rope=rope),
        grid=(T_ROWS // PROJ_TM, width // PROJ_TN),
        in_specs=[pl.BlockSpec((PROJ_TM, D_MODEL), lambda m, n: (m, 0)),
                  pl.BlockSpec((1, 6, D_MODEL), lambda m, n: (_mod_row(m, PROJ_TM), 0, 0)),
                  pl.BlockSpec((D_MODEL, PROJ_TN), lambda m, n: (0, n)),
                  tab_spec, tab_spec, tab_spec],
        out_specs=pl.BlockSpec((PROJ_TM, PROJ_TN), lambda m, n: (m, n)),
        out_shape=jax.ShapeDtypeStruct((T_ROWS, width), out_dtype),
        scratch_shapes=[pltpu.VMEM((PROJ_TM, D_MODEL), BF16)],
        compiler_params=_cparams(("parallel", "arbitrary")),
        name="inproj_rope" if rope else "inproj",
    )(x, mod3, w, cos, sa, sb)


def _gmlp_kernel(u_ref, v_ref, g_ref, b_ref, ws_ref, bs_ref, o_ref):
    for h in range(A_HEADS):
        sl = slice(h * HEAD_DIM, (h + 1) * HEAD_DIM)
        v = _gelu_tanh(v_ref[:, sl].astype(F32))
        mu = jnp.mean(v, axis=-1, keepdims=True)
        var = jnp.mean(jnp.square(v - mu), axis=-1, keepdims=True)
        vn = (v - mu) * lax.rsqrt(var + LN_EPS) * g_ref[:, sl] + b_ref[:, sl]
        s = jnp.dot(ws_ref[h], vn.astype(BF16), preferred_element_type=F32) + bs_ref[:, h:h + 1]
        u = _gelu_tanh(u_ref[:, sl].astype(F32))
        o_ref[:, sl] = (u * s).astype(o_ref.dtype)


def _gmlp(p_lo, ln_g, ln_b, ws, bs_t):
    return pl.pallas_call(
        _gmlp_kernel,
        grid=(T_ROWS // A_CHUNK,),
        in_specs=[pl.BlockSpec((A_CHUNK, A_WIDTH), lambda i: (i, 0)),
                  pl.BlockSpec((A_CHUNK, A_WIDTH), lambda i: (i, 1)),
                  pl.BlockSpec((1, A_WIDTH), lambda i: (0, 0)),
                  pl.BlockSpec((1, A_WIDTH), lambda i: (0, 0)),
                  pl.BlockSpec((A_HEADS, A_CHUNK, A_CHUNK), lambda i: (0, 0, 0)),
                  pl.BlockSpec((A_CHUNK, A_HEADS), lambda i: (0, 0))],
        out_specs=pl.BlockSpec((A_CHUNK, A_WIDTH), lambda i: (i, 0)),
        out_shape=jax.ShapeDtypeStruct((T_ROWS, A_WIDTH), BF16),
        compiler_params=_cparams(("parallel",)),
        name="gmlp",
    )(p_lo, p_lo, ln_g, ln_b, ws, bs_t)


NQ_LAT = SEQ // ATT_TQ


def _attn_kernel(lam_ref, q_ref, kl_ref, vl_ref, kc_ref, vc_ref, g_ref, o_ref, m_scr, l_scr, acc_scr):
    qi = pl.program_id(2)
    m_scr[...] = jnp.full(m_scr.shape, -1e30, F32)
    l_scr[...] = jnp.zeros(l_scr.shape, F32)
    acc_scr[...] = jnp.zeros(acc_scr.shape, F32)

    def process(k, v):
        for mp in range(2):
            sl = slice(mp * HEAD_DIM, (mp + 1) * HEAD_DIM)
            s = _dot_nt(q_ref[:, sl], k[:, sl])
            cols = [s[:, c * LANE:(c + 1) * LANE] for c in range(k.shape[0] // LANE)]
            m_part = cols[0]
            for col in cols[1:]:
                m_part = jnp.maximum(m_part, col)
            m_old = m_scr[mp]
            m_new = jnp.maximum(m_old, jnp.max(m_part, axis=-1, keepdims=True))
            alpha = jnp.exp2(m_old - m_new)
            l_part = alpha * l_scr[mp]
            ps = []
            for col in cols:
                p = jnp.exp2(col - m_new)
                l_part = l_part + p
                ps.append(p.astype(BF16))
            pv = jnp.dot(jnp.concatenate(ps, axis=1), v, preferred_element_type=F32)
            for half in range(B_V_DIM // LANE):
                hs = slice(half * LANE, (half + 1) * LANE)
                acc_scr[mp, :, hs] = alpha * acc_scr[mp, :, hs] + pv[:, hs]
            l_scr[mp] = l_part
            m_scr[mp] = m_new

    process(kc_ref[...], vc_ref[...])

    @pl.when(qi < NQ_LAT)
    def _():
        for j in range(SEQ // ATT_TK):
            rows = slice(j * ATT_TK, (j + 1) * ATT_TK)
            process(kl_ref[rows, :], vl_ref[rows, :])

    lam = lam_ref[0]
    post = lam_ref[1]
    l0 = jnp.sum(l_scr[0], axis=-1, keepdims=True)
    l1 = jnp.sum(l_scr[1], axis=-1, keepdims=True)
    o = acc_scr[0] / l0 - lam * (acc_scr[1] / l1)
    r = o * lax.rsqrt(jnp.mean(jnp.square(o), axis=-1, keepdims=True) + NORM_EPS)
    o_ref[...] = (r * g_ref[...] * post).astype(o_ref.dtype)


def _diff_attention(p_lo, lam2, subln_g):
    def q_rows(b, h, qi):
        return jnp.where(qi < NQ_LAT, b * NQ_LAT + qi, BATCH * NQ_LAT + b)

    return pl.pallas_call(
        _attn_kernel,
        grid=(BATCH, B_HEADS, NQ_LAT + 1),
        in_specs=[pl.BlockSpec(memory_space=pltpu.SMEM),
                  pl.BlockSpec((ATT_TQ, B_V_DIM), lambda b, h, qi: (q_rows(b, h, qi), 4 + h)),
                  pl.BlockSpec((SEQ, B_V_DIM), lambda b, h, qi: (b, 8 + h)),
                  pl.BlockSpec((SEQ, B_V_DIM), lambda b, h, qi: (b, 12 + h)),
                  pl.BlockSpec((CTX_LEN, B_V_DIM), lambda b, h, qi: (N_LAT // CTX_LEN + b, 8 + h)),
                  pl.BlockSpec((CTX_LEN, B_V_DIM), lambda b, h, qi: (N_LAT // CTX_LEN + b, 12 + h)),
                  pl.BlockSpec((1, B_V_DIM), lambda b, h, qi: (0, 0))],
        out_specs=pl.BlockSpec((ATT_TQ, B_V_DIM), lambda b, h, qi: (q_rows(b, h, qi), h)),
        scratch_shapes=[pltpu.VMEM((2, ATT_TQ, LANE), F32),
                        pltpu.VMEM((2, ATT_TQ, LANE), F32),
                        pltpu.VMEM((2, ATT_TQ, B_V_DIM), F32)],
        out_shape=jax.ShapeDtypeStruct((T_ROWS, B_WIDTH), BF16),
        compiler_params=_cparams(("parallel", "parallel", "arbitrary")),
        name="diff_attention",
    )(lam2, p_lo, p_lo, p_lo, p_lo, p_lo, subln_g)


def _scan_structure():
    c = SCAN_C
    x = np.zeros((2, (2 + SCAN_LEVELS) * c, c), np.float32)
    msk = np.zeros((2, SCAN_LEVELS + 1, c, c), np.float32)
    r = np.arange(c)
    j = np.arange(c)[None, :]
    xf = np.zeros(((2 + SCAN_LEVELS) * c, c), np.float32)
    mf = np.zeros((SCAN_LEVELS + 1, c, c), np.float32)
    xf[0:c] = (j <= r[:, None])
    xf[c:2 * c] = (j > r[:, None])
    for lev in range(SCAN_LEVELS):
        half = (c // 2) >> lev
        start = (r // (2 * half)) * (2 * half)
        mid = start + half - 1
        later = r > mid
        rows = np.where(later[:, None], (j > mid[:, None]) & (j <= r[:, None]),
                        (j > r[:, None]) & (j <= mid[:, None]))
        xf[(2 + lev) * c:(3 + lev) * c] = rows
        same = start[:, None] == start[None, :]
        mf[lev] = same & later[:, None] & (~later)[None, :]
    mf[SCAN_LEVELS] = np.eye(c)
    x[0] = xf
    msk[0] = mf
    x[1] = xf.reshape(2 + SCAN_LEVELS, c, c)[:, ::-1, ::-1].reshape(-1, c)
    msk[1] = mf[:, ::-1, ::-1]
    return x, msk


def _hgrn_kernel(qf_ref, ff_ref, if_ref, qb_ref, fb_ref, ib_ref, lb_ref, x_ref, msk_ref,
                 of_ref, ob_ref, st_scr):
    j = pl.program_id(1)

    @pl.when(j == 0)
    def _():
        st_scr[...] = jnp.zeros(st_scr.shape, F32)

    c = SCAN_C
    dirs = ((qf_ref, ff_ref, if_ref, of_ref, c - 1), (qb_ref, fb_ref, ib_ref, ob_ref, 0))
    for d, (q_ref, f_ref, i_ref, o_ref, end_row) in enumerate(dirs):
        for h in range(C_HEADS):
            sl = slice(h * HEAD_DIM, (h + 1) * HEAD_DIM)
            qraw = q_ref[:, sl]
            q = qraw * _sigmoid(qraw)
            lb = lb_ref[d:d + 1, sl]
            f = lb + (1.0 - lb) * _sigmoid(f_ref[:, sl])
            logf = jnp.log(f)
            k = 1.0 - f
            v = i_ref[:, sl].astype(BF16)
            hi = logf.astype(BF16)
            lo = (logf - hi.astype(F32)).astype(BF16)
            e2 = jnp.dot(x_ref[d], jnp.concatenate([hi, lo], axis=1), preferred_element_type=F32)
            w = jnp.exp(e2[:, 0:c] + e2[:, c:2 * c])
            a = msk_ref[d, SCAN_LEVELS] * _dot_nt(q.astype(BF16), k.astype(BF16))
            for lev in range(SCAN_LEVELS):
                wl = w[(2 + lev) * c:(3 + lev) * c]
                a = a + msk_ref[d, lev] * _dot_nt((q * wl).astype(BF16), (k * wl).astype(BF16))
            st = st_scr[d, h]
            o = jnp.dot(a.astype(BF16), v, preferred_element_type=F32)
            o = o + _dot_nt((q * w[0:c]).astype(BF16), st.astype(BF16))
            o_ref[:, sl] = o
            g_end = w[end_row:end_row + 1, :]
            st_scr[d, h] = st * g_end + _dot_tn(v, (k * w[c:2 * c]).astype(BF16))


def _hgrn_scan(p_hi, lb2, xmat, masks):
    lat_chunks = SEQ // SCAN_C
    ctx_chunks = CTX_LEN // SCAN_C
    steps = ctx_chunks + lat_chunks

    def fwd_rows(b, j):
        return jnp.where(j < ctx_chunks, N_LAT // SCAN_C + b * ctx_chunks + j,
                         b * lat_chunks + j - ctx_chunks)

    def bwd_rows(b, j):
        return jnp.where(j < ctx_chunks, N_LAT // SCAN_C + b * ctx_chunks + (ctx_chunks - 1 - j),
                         b * lat_chunks + (steps - 1 - j))

    def spec(rows, col):
        return pl.BlockSpec((SCAN_C, C_WIDTH), lambda b, j: (rows(b, j), col))

    nx = (2 + SCAN_LEVELS) * SCAN_C
    return pl.pallas_call(
        _hgrn_kernel,
        grid=(BATCH, steps),
        in_specs=[spec(fwd_rows, 0), spec(fwd_rows, 1), spec(fwd_rows, 3),
                  spec(bwd_rows, 0), spec(bwd_rows, 2), spec(bwd_rows, 3),
                  pl.BlockSpec((2, C_WIDTH), lambda b, j: (0, 0)),
                  pl.BlockSpec((2, nx, SCAN_C), lambda b, j: (0, 0, 0)),
                  pl.BlockSpec((2, SCAN_LEVELS + 1, SCAN_C, SCAN_C), lambda b, j: (0, 0, 0, 0))],
        out_specs=[spec(fwd_rows, 0), spec(bwd_rows, 0)],
        out_shape=[jax.ShapeDtypeStruct((T_ROWS, C_WIDTH), F32)] * 2,
        scratch_shapes=[pltpu.VMEM((2, C_HEADS, HEAD_DIM, HEAD_DIM), F32)],
        compiler_params=_cparams(("parallel", "arbitrary")),
        name="hgrn2_scan",
    )(p_hi, p_hi, p_hi, p_hi, p_hi, p_hi, lb2, xmat, masks)


def _layer_norm_rows(z, g, b):
    mu = jnp.mean(z, axis=-1, keepdims=True)
    var = jnp.mean(jnp.square(z - mu), axis=-1, keepdims=True)
    return (z - mu) * lax.rsqrt(var + LN_EPS) * g + b


def _outproj_kernel(a_ref, b_ref, of_ref, ob_ref, cg_ref, hg_ref, x_ref, mod_ref, w_ref,
                    l1g_ref, l1b_ref, rw_ref, rb_ref,
                    x1_ref, hx_ref, hxp_ref, idx_ref, wt_ref, rank_ref, cnt_ref, cnt_scr):
    parts = []
    for h in range(C_HEADS):
        sl = slice(h * HEAD_DIM, (h + 1) * HEAD_DIM)
        o = of_ref[:, sl] + ob_ref[:, sl]
        on = o * lax.rsqrt(jnp.mean(jnp.square(o), axis=-1, keepdims=True) + NORM_EPS) * hg_ref[...]
        g = cg_ref[:, sl]
        parts.append((on * (g * _sigmoid(g))).astype(BF16))
    c_x = jnp.concatenate(parts, axis=1)
    mix = jnp.dot(a_ref[...], w_ref[0:A_WIDTH, :], preferred_element_type=F32)
    mix = mix + jnp.dot(b_ref[...], w_ref[A_WIDTH:A_WIDTH + B_WIDTH, :], preferred_element_type=F32)
    mix = mix + jnp.dot(c_x, w_ref[A_WIDTH + B_WIDTH:, :], preferred_element_type=F32)
    g1 = mod_ref[0, 2:3, :]
    sh2 = mod_ref[0, 3:4, :]
    sc2 = mod_ref[0, 4:5, :]
    x1 = _layer_norm_rows(DEEPNORM_ALPHA * x_ref[...] + g1 * mix, l1g_ref[...], l1b_ref[...])
    x1_ref[...] = x1
    hx = x1 * (1.0 + sc2) + sh2
    hx_bf = hx.astype(BF16)
    hx_ref[...] = hx_bf
    hxp_ref[...] = _pack_pairs(hx)

    logits = jnp.dot(hx.astype(BF16), rw_ref[...], preferred_element_type=F32)
    scores = _sigmoid(logits)
    sel = scores + rb_ref[...]
    tm = sel.shape[0]
    lane = lax.broadcasted_iota(jnp.int32, sel.shape, 1).astype(F32)
    slot = lax.broadcasted_iota(jnp.int32, (tm, LANE), 1)
    idx_acc = jnp.zeros((tm, LANE), F32)
    wt_acc = jnp.zeros((tm, LANE), F32)
    chosen = jnp.zeros(sel.shape, F32)
    hits = []
    for k in range(TOP_K):
        mx = jnp.max(sel, axis=-1, keepdims=True)
        idx = jnp.min(jnp.where(sel == mx, lane, float(N_EXPERTS)), axis=-1, keepdims=True)
        hit = lane == idx
        hits.append(hit)
        chosen = jnp.where(hit, 1.0, chosen)
        w_k = jnp.sum(jnp.where(hit, scores, 0.0), axis=-1, keepdims=True)
        idx_acc = jnp.where(slot == k, idx, idx_acc)
        wt_acc = jnp.where(slot == k, w_k, wt_acc)
        sel = jnp.where(hit, -jnp.inf, sel)
    idx_ref[...] = idx_acc.astype(jnp.int32)
    wt_ref[...] = wt_acc / jnp.sum(wt_acc, axis=-1, keepdims=True) * ROUTED_SCALE

    @pl.when(pl.program_id(0) == 0)
    def _():
        cnt_scr[...] = jnp.zeros(cnt_scr.shape, F32)

    below = (lax.broadcasted_iota(jnp.int32, (tm, tm), 0) > lax.broadcasted_iota(jnp.int32, (tm, tm), 1))
    before = jnp.dot(below.astype(BF16), chosen.astype(BF16), preferred_element_type=F32) + cnt_scr[0:1, :]
    rank_acc = jnp.zeros((tm, LANE), F32)
    for k in range(TOP_K):
        r_k = jnp.sum(jnp.where(hits[k], before, 0.0), axis=-1, keepdims=True)
        rank_acc = jnp.where(slot == k, r_k, rank_acc)
    rank_ref[...] = rank_acc.astype(jnp.int32)
    cnt = cnt_scr[...] + jnp.sum(chosen, axis=0, keepdims=True)
    cnt_scr[...] = cnt
    cnt_ref[...] = cnt


def _outproj(a_x, b_x, o_f, o_b, p_hi, hnorm_g, x, mod3, w_out, ln_g, ln_b, router_w, router_b):
    row = lambda width: pl.BlockSpec((OUT_TM, width), lambda i: (i, 0))
    const = lambda shape: pl.BlockSpec(shape, lambda i: tuple(0 for _ in shape))
    return pl.pallas_call(
        _outproj_kernel,
        grid=(T_ROWS // OUT_TM,),
        in_specs=[row(A_WIDTH), row(B_WIDTH), row(C_WIDTH), row(C_WIDTH),
                  pl.BlockSpec((OUT_TM, C_WIDTH), lambda i: (i, 4)),
                  const((1, HEAD_DIM)),
                  row(D_MODEL),
                  pl.BlockSpec((1, 6, D_MODEL), lambda i: (_mod_row(i, OUT_TM), 0, 0)),
                  const((D_MODEL, D_MODEL)),
                  const((1, D_MODEL)), const((1, D_MODEL)),
                  const((D_MODEL, N_EXPERTS)), const((1, N_EXPERTS))],
        out_specs=[row(D_MODEL), row(D_MODEL), row(D_MODEL // 2), row(LANE), row(LANE), row(LANE),
                   const((8, N_EXPERTS))],
        out_shape=[jax.ShapeDtypeStruct((T_ROWS, D_MODEL), F32),
                   jax.ShapeDtypeStruct((T_ROWS, D_MODEL), BF16),
                   jax.ShapeDtypeStruct((T_ROWS, D_MODEL // 2), jnp.uint32),
                   jax.ShapeDtypeStruct((T_ROWS, LANE), jnp.int32),
                   jax.ShapeDtypeStruct((T_ROWS, LANE), F32),
                   jax.ShapeDtypeStruct((T_ROWS, LANE), jnp.int32),
                   jax.ShapeDtypeStruct((8, N_EXPERTS), F32)],
        scratch_shapes=[pltpu.VMEM((8, N_EXPERTS), F32)],
        compiler_params=_cparams(("arbitrary",)),
        name="outproj_ln_router",
    )(a_x, b_x, o_f, o_b, p_hi, hnorm_g, x, mod3, w_out, ln_g, ln_b, router_w, router_b)


def _experts_kernel(te_ref, nt_ref, xs_ref, wg_ref, wu_ref, wd_ref, ys_ref, wgu_scr, wd_scr, *, packed):
    i = pl.program_id(0)
    live = i < nt_ref[0]
    new_expert = jnp.logical_or(i == 0, te_ref[i] != te_ref[jnp.maximum(i - 1, 0)])

    @pl.when(jnp.logical_and(live, new_expert))
    def _():
        wgu_scr[:, :EXPERT_DIM] = wg_ref[0, 0].astype(BF16)
        wgu_scr[:, EXPERT_DIM:] = wu_ref[0, 0].astype(BF16)
        wd_scr[...] = wd_ref[0, 0].astype(BF16)

    @pl.when(live)
    def _():
        if packed:
            low, high = _unpack_pairs(xs_ref[...])
            x = jnp.concatenate([low.astype(BF16), high.astype(BF16)], axis=1)
        else:
            x = xs_ref[...]
        h = jnp.dot(x, wgu_scr[...], preferred_element_type=F32)
        g = h[:, :EXPERT_DIM]
        u = h[:, EXPERT_DIM:]
        a = (g * _sigmoid(g) * u).astype(BF16)
        y = jnp.dot(a, wd_scr[...], preferred_element_type=F32)
        ys_ref[...] = _pack_pairs(y) if packed else y.astype(ys_ref.dtype)

    @pl.when(jnp.logical_not(live))
    def _():
        ys_ref[...] = jnp.zeros(ys_ref.shape, ys_ref.dtype)


def _experts(tile_expert, n_tiles, xs, w_gate, w_up, w_down, layer):
    tiles = xs.shape[0] // MOE_TM
    last = lambda i, nt: jnp.minimum(i, nt[0] - 1)
    grid_spec = pltpu.PrefetchScalarGridSpec(
        num_scalar_prefetch=2,
        grid=(tiles,),
        in_specs=[pl.BlockSpec((MOE_TM, xs.shape[1]), lambda i, te, nt: (last(i, nt), 0)),
                  pl.BlockSpec((1, 1, D_MODEL, EXPERT_DIM), lambda i, te, nt: (layer, te[i], 0, 0)),
                  pl.BlockSpec((1, 1, D_MODEL, EXPERT_DIM), lambda i, te, nt: (layer, te[i], 0, 0)),
                  pl.BlockSpec((1, 1, EXPERT_DIM, D_MODEL), lambda i, te, nt: (layer, te[i], 0, 0))],
        out_specs=pl.BlockSpec((MOE_TM, xs.shape[1]), lambda i, te, nt: (i, 0)),
        scratch_shapes=[pltpu.VMEM((D_MODEL, 2 * EXPERT_DIM), BF16),
                        pltpu.VMEM((EXPERT_DIM, D_MODEL), BF16)],
    )
    return pl.pallas_call(
        functools.partial(_experts_kernel, packed=xs.dtype == jnp.uint32),
        grid_spec=grid_spec,
        out_shape=jax.ShapeDtypeStruct(xs.shape, xs.dtype),
        compiler_params=_cparams(("arbitrary",)),
        name="experts",
    )(tile_expert, n_tiles, xs, w_gate, w_up, w_down)


def _combine_kernel(yg_ref, wt_ref, ysh_ref, x1_ref, mod_ref, g_ref, b_ref, o_ref):
    tm = COMBINE_TM
    low = jnp.zeros((tm, D_MODEL // 2), F32)
    high = jnp.zeros((tm, D_MODEL // 2), F32)
    for k in range(TOP_K):
        lo_k, hi_k = _unpack_pairs(yg_ref[k])
        low = low + wt_ref[:, k:k + 1] * lo_k
        high = high + wt_ref[:, k:k + 1] * hi_k
    y = ysh_ref[...].astype(F32) + jnp.concatenate([low, high], axis=1)
    g2 = mod_ref[0, 5:6, :]
    o_ref[...] = _layer_norm_rows(DEEPNORM_ALPHA * x1_ref[...] + g2 * y, g_ref[...], b_ref[...])


def _combine(yg, wt, ysh, x1, mod3, ln_g, ln_b):
    tm = COMBINE_TM
    row = lambda width: pl.BlockSpec((tm, width), lambda i: (i, 0))
    return pl.pallas_call(
        _combine_kernel,
        grid=(T_ROWS // tm,),
        in_specs=[pl.BlockSpec((TOP_K, tm, D_MODEL // 2), lambda i: (0, i, 0)),
                  row(LANE), row(D_MODEL), row(D_MODEL),
                  pl.BlockSpec((1, 6, D_MODEL), lambda i: (_mod_row(i, tm), 0, 0)),
                  pl.BlockSpec((1, D_MODEL), lambda i: (0, 0)),
                  pl.BlockSpec((1, D_MODEL), lambda i: (0, 0))],
        out_specs=row(D_MODEL),
        out_shape=jax.ShapeDtypeStruct((T_ROWS, D_MODEL), F32),
        compiler_params=_cparams(("parallel",)),
        name="moe_combine_ln",
    )(yg, wt, ysh, x1, mod3, ln_g, ln_b)


def _layout(idx, rank, counts):
    tiles_per = (counts + MOE_TM - 1) // MOE_TM
    tile_end = jnp.cumsum(tiles_per)
    start = (tile_end - tiles_per) * MOE_TM
    experts = jnp.arange(N_EXPERTS, dtype=jnp.int32)
    pos = rank + jnp.sum(jnp.where(idx[:, :, None] == experts, start, 0), axis=-1)
    n_tiles = tile_end[-1:].astype(jnp.int32)
    tile = jnp.arange(MOE_TILES, dtype=jnp.int32)
    tile_expert = jnp.sum((tile_end[None, :] <= tile[:, None]).astype(jnp.int32), axis=-1)
    tile_expert = jnp.minimum(tile_expert, N_EXPERTS - 1)
    return pos, tile_expert, n_tiles


def _dispatch_kernel(pos_ref, hxw_ref, init_hbm, xs_hbm, sem):
    def issue(t, carry):
        for k in range(TOP_K):
            pltpu.make_async_copy(hxw_ref.at[pl.ds(t, 1), :],
                                  xs_hbm.at[pl.ds(pos_ref[t * TOP_K + k], 1), :], sem).start(priority=k % 2)
        return carry

    lax.fori_loop(0, DISPATCH_TM, issue, 0)
    for _ in range(TOP_K):
        pltpu.make_async_copy(hxw_ref, xs_hbm.at[pl.ds(0, DISPATCH_TM), :], sem).wait()


def _dispatch_rows(pos, hx_words):
    init = jnp.zeros((MOE_ROWS, D_MODEL // 2), jnp.uint32)
    return pl.pallas_call(
        _dispatch_kernel,
        grid=(T_ROWS // DISPATCH_TM,),
        in_specs=[pl.BlockSpec((DISPATCH_TM * TOP_K,), lambda i: (i,), memory_space=pltpu.SMEM),
                  pl.BlockSpec((DISPATCH_TM, D_MODEL // 2), lambda i: (i, 0)),
                  pl.BlockSpec(memory_space=pl.ANY)],
        out_specs=pl.BlockSpec(memory_space=pl.ANY),
        out_shape=jax.ShapeDtypeStruct((MOE_ROWS, D_MODEL // 2), jnp.uint32),
        scratch_shapes=[pltpu.SemaphoreType.DMA(())],
        input_output_aliases={2: 0},
        compiler_params=pltpu.CompilerParams(dimension_semantics=("arbitrary",), has_side_effects=True),
        name="moe_dispatch_rows",
    )(pos.reshape(-1), hx_words, init)


def _rope_tables():
    t = jnp.arange(SEQ)
    row = (t // GRID_W).astype(F32)
    col = (t % GRID_W).astype(F32)
    axis_dim = HEAD_DIM // 2
    inv = ROPE_THETA ** (-jnp.arange(0, axis_dim, 2, dtype=F32) / axis_dim)
    ar = row[:, None] * inv
    ac = col[:, None] * inv
    cos = jnp.concatenate([jnp.cos(ar), jnp.cos(ar), jnp.cos(ac), jnp.cos(ac)], axis=-1)
    sin = jnp.concatenate([jnp.sin(ar), jnp.sin(ar), jnp.sin(ac), jnp.sin(ac)], axis=-1)
    odd = ((jnp.arange(HEAD_DIM) // 32) % 2 == 1)[None, :]
    sa = jnp.where(odd, sin, 0.0)
    sb = jnp.where(odd, 0.0, -sin)
    lat = lambda z: jnp.tile(z, (BATCH, 1))
    cos_all = jnp.concatenate([lat(cos), jnp.ones((N_CTX, HEAD_DIM), F32)], axis=0)
    sa_all = jnp.concatenate([lat(sa), jnp.zeros((N_CTX, HEAD_DIM), F32)], axis=0)
    sb_all = jnp.concatenate([lat(sb), jnp.zeros((N_CTX, HEAD_DIM), F32)], axis=0)
    return cos_all, sa_all, sb_all


def kernel(x, c, ctx, c_ctx, w_mod, b_mod, w_in, w_out, gmlp_ln_g, gmlp_ln_b, gmlp_ws, gmlp_bs,
           diff_lam, diff_subln_g, hgrn_lb, hgrn_norm_g, ln1_g, ln1_b, ln2_g, ln2_b,
           router_w, router_b, exp_w_gate, exp_w_up, exp_w_down, sh_w_gate, sh_w_up, sh_w_down):
    assert x.shape == (BATCH, SEQ, D_MODEL) and ctx.shape == (BATCH, CTX_LEN, D_MODEL)
    xs_all = jnp.concatenate([x.reshape(N_LAT, D_MODEL), ctx.reshape(N_CTX, D_MODEL)], axis=0)

    cond = jnp.zeros((8, D_MODEL), F32).at[0:BATCH].set(c).at[BATCH].set(c_ctx)
    mod_all = _modulation(cond, w_mod, b_mod)[:, :BATCH + 1].reshape(DEPTH, BATCH + 1, 6, D_MODEL)

    sm = jax.nn.softmax(hgrn_lb.astype(F32), axis=0)
    lb_all = jnp.cumsum(sm, axis=0) - sm[0]
    tables = _rope_tables()
    xmat_np, masks_np = _scan_structure()
    xmat = jnp.asarray(xmat_np, BF16)
    masks = jnp.asarray(masks_np, F32)

    for l in range(DEPTH):
        mod3 = mod_all[l]
        w_in_l = w_in[l].astype(BF16)
        p_lo = _inproj(xs_all, mod3, w_in_l[:, :LOW_WIDTH], tables, BF16, True)
        p_hi = _inproj(xs_all, mod3, w_in_l[:, LOW_WIDTH:], tables, F32, False)

        a_x = _gmlp(p_lo, gmlp_ln_g[l][None, :], gmlp_ln_b[l][None, :],
                    gmlp_ws[l].astype(BF16), gmlp_bs[l].T)

        lam_init = 0.8 - 0.6 * math.exp(-0.3 * l)
        dl = diff_lam[l].astype(F32)
        lam = jnp.exp(jnp.sum(dl[0] * dl[1])) - jnp.exp(jnp.sum(dl[2] * dl[3])) + lam_init
        lam2 = jnp.stack([lam, jnp.asarray(1.0 - lam_init, F32)]).astype(F32)
        b_x = _diff_attention(p_lo, lam2, diff_subln_g[l][None, :])

        o_f, o_b = _hgrn_scan(p_hi, lb_all[l], xmat, masks)

        x1, hx, hx_words, idx, wt, rank, cnt = _outproj(
            a_x, b_x, o_f, o_b, p_hi, hgrn_norm_g[l][None, :], xs_all, mod3,
            w_out[l].astype(BF16), ln1_g[l][None, :], ln1_b[l][None, :],
            router_w[l].astype(BF16), router_b[l][None, :])

        pos, tile_expert, n_tiles = _layout(idx[:, :TOP_K], rank[:, :TOP_K], cnt[0].astype(jnp.int32))
        xs_grouped = _dispatch_rows(pos, hx_words)
        ys = _experts(tile_expert, n_tiles, xs_grouped, exp_w_gate, exp_w_up, exp_w_down, l)
        yg = jnp.take(ys, pos.T.reshape(-1), axis=0, mode="clip").reshape(TOP_K, T_ROWS, D_MODEL // 2)

        ysh = _experts(jnp.zeros((T_ROWS // MOE_TM,), jnp.int32),
                       jnp.full((1,), T_ROWS // MOE_TM, jnp.int32),
                       hx, sh_w_gate[:, None], sh_w_up[:, None], sh_w_down[:, None], l)

        xs_all = _combine(yg, wt, ysh, x1, mod3, ln2_g[l][None, :], ln2_b[l][None, :])

    return xs_all[:N_LAT].reshape(BATCH, SEQ, D_MODEL)
```

```python
import functools
import math

import numpy as np
import jax
import jax.numpy as jnp
from jax import lax
from jax.experimental import pallas as pl
from jax.experimental.pallas import tpu as pltpu

F32 = jnp.float32
BF16 = jnp.bfloat16

D_MODEL = 2048
BATCH = 2
SEQ = 4096
DEPTH = 4
CTX_LEN = 256
GRID_W = 64
HEAD_DIM = 128
A_WIDTH = 512
A_HEADS = 4
A_CHUNK = 128
B_WIDTH = 1024
B_HEADS = 4
B_QK_WIDTH = 1024
B_V_DIM = 256
ROPE_THETA = 10000.0
C_WIDTH = 512
C_HEADS = 4
IN_WIDTH = 6656
N_EXPERTS = 64
TOP_K = 8
EXPERT_DIM = 384
ROUTED_SCALE = 2.5
DEEPNORM_ALPHA = (2.0 * DEPTH) ** 0.25
LN_EPS = 1e-5
NORM_EPS = 1e-5

N_LAT = BATCH * SEQ
N_CTX = BATCH * CTX_LEN
T_ROWS = N_LAT + N_CTX
LOW_WIDTH = 4096
HI_WIDTH = IN_WIDTH - LOW_WIDTH
QK_SCALE = HEAD_DIM ** -0.5 * math.log2(math.e)

LANE = 128
PROJ_TM = 512
PROJ_TN = 512
ATT_TQ = 256
ATT_TK = 1024
SCAN_C = 128
SCAN_LEVELS = 7
OUT_TM = 256
MOE_TM = 512
COMBINE_TM = 256
DISPATCH_TM = 256
MOE_ROWS = ((T_ROWS * TOP_K + N_EXPERTS * (MOE_TM - 1)) // MOE_TM) * MOE_TM
MOE_TILES = MOE_ROWS // MOE_TM
VMEM_LIMIT = 48 * 1024 * 1024
EXPERTS_VMEM_LIMIT = 56 * 1024 * 1024


def _cparams(sem, vmem=VMEM_LIMIT):
    return pltpu.CompilerParams(dimension_semantics=sem, vmem_limit_bytes=vmem)


def _sigmoid(x):
    return 1.0 / (1.0 + jnp.exp(-x))


def _gelu_tanh(x):
    return 0.5 * x * (1.0 + jnp.tanh(math.sqrt(2.0 / math.pi) * (x + 0.044715 * (x * x * x))))


def _pack_pairs(x):
    n = x.shape[1] // 2
    bits = pltpu.bitcast(x.astype(BF16).astype(F32), jnp.uint32)
    return bits[:, n:] | (bits[:, :n] >> 16)


def _unpack_pairs(words):
    low = pltpu.bitcast(words << 16, F32)
    high = pltpu.bitcast(words & jnp.uint32(0xFFFF0000), F32)
    return low, high


def _dot_nt(a, b):
    return lax.dot_general(a, b, (((1,), (1,)), ((), ())), preferred_element_type=F32)


def _dot_tn(a, b):
    return lax.dot_general(a, b, (((0,), (0,)), ((), ())), preferred_element_type=F32)


def _mod_kernel(c_ref, w_ref, b_ref, o_ref):
    c = c_ref[...]
    s = (c * _sigmoid(c)).astype(BF16)
    o_ref[0] = jnp.dot(s, w_ref[0].astype(BF16), preferred_element_type=F32) + b_ref[0]


def _modulation(cond, w_mod, b_mod):
    tn = 1024
    return pl.pallas_call(
        _mod_kernel,
        grid=(DEPTH, 6 * D_MODEL // tn),
        in_specs=[pl.BlockSpec((8, D_MODEL), lambda l, n: (0, 0)),
                  pl.BlockSpec((1, D_MODEL, tn), lambda l, n: (l, 0, n)),
                  pl.BlockSpec((1, 1, tn), lambda l, n: (l, 0, n))],
        out_specs=pl.BlockSpec((1, 8, tn), lambda l, n: (l, 0, n)),
        out_shape=jax.ShapeDtypeStruct((DEPTH, 8, 6 * D_MODEL), F32),
        compiler_params=_cparams(("parallel", "parallel")),
        name="modulation",
    )(cond, w_mod, b_mod.reshape(DEPTH, 1, 6 * D_MODEL))


def _mod_row(m, tm):
    return jnp.minimum(m // (SEQ // tm), BATCH)


def _inproj_kernel(x_ref, mod_ref, w_ref, cos_ref, sa_ref, sb_ref, o_ref, h_scr, *, rope):
    n = pl.program_id(1)

    @pl.when(n == 0)
    def _():
        sh = mod_ref[0, 0:1, :]
        sc = mod_ref[0, 1:2, :]
        h_scr[...] = (x_ref[...] * (1.0 + sc) + sh).astype(BF16)

    acc = jnp.dot(h_scr[...], w_ref[...], preferred_element_type=F32)
    if not rope:
        o_ref[...] = acc.astype(o_ref.dtype)
        return

    is_qk = (n >= 2) & (n < 6)

    @pl.when(is_qk)
    def _():
        scale = jnp.where(n < 4, QK_SCALE, 1.0).astype(F32)
        cos = cos_ref[...]
        sa = sa_ref[...]
        sb = sb_ref[...]
        for g in range(PROJ_TN // LANE):
            blk = acc[:, g * LANE:(g + 1) * LANE]
            r = blk * cos + pltpu.roll(blk, 32, 1) * sa + pltpu.roll(blk, 96, 1) * sb
            o_ref[:, g * LANE:(g + 1) * LANE] = (r * scale).astype(o_ref.dtype)

    @pl.when(jnp.logical_not(is_qk))
    def _():
        o_ref[...] = acc.astype(o_ref.dtype)


def _inproj(x, mod3, w, tables, out_dtype, rope):
    width = w.shape[1]
    cos, sa, sb = tables
    tab_spec = pl.BlockSpec((PROJ_TM, LANE), lambda m, n: (m, 0))
    return pl.pallas_call(
        functools.partial(_inproj_kernel, rope=rope),
        grid=(T_ROWS // PROJ_TM, width // PROJ_TN),
        in_specs=[pl.BlockSpec((PROJ_TM, D_MODEL), lambda m, n: (m, 0)),
                  pl.BlockSpec((1, 6, D_MODEL), lambda m, n: (_mod_row(m, PROJ_TM), 0, 0)),
                  pl.BlockSpec((D_MODEL, PROJ_TN), lambda m, n: (0, n)),
                  tab_spec, tab_spec, tab_spec],
        out_specs=pl.BlockSpec((PROJ_TM, PROJ_TN), lambda m, n: (m, n)),
        out_shape=jax.ShapeDtypeStruct((T_ROWS, width), out_dtype),
        scratch_shapes=[pltpu.VMEM((PROJ_TM, D_MODEL), BF16)],
        compiler_params=_cparams(("parallel", "arbitrary")),
        name="inproj_rope" if rope else "inproj",
    )(x, mod3, w, cos, sa, sb)


def _gmlp_kernel(u_ref, v_ref, g_ref, b_ref, ws_ref, bs_ref, o_ref):
    for h in range(A_HEADS):
        sl = slice(h * HEAD_DIM, (h + 1) * HEAD_DIM)
        v = _gelu_tanh(v_ref[:, sl].astype(F32))
        mu = jnp.mean(v, axis=-1, keepdims=True)
        var = jnp.mean(jnp.square(v - mu), axis=-1, keepdims=True)
        vn = (v - mu) * lax.rsqrt(var + LN_EPS) * g_ref[:, sl] + b_ref[:, sl]
        s = jnp.dot(ws_ref[h], vn.astype(BF16), preferred_element_type=F32) + bs_ref[:, h:h + 1]
        u = _gelu_tanh(u_ref[:, sl].astype(F32))
        o_ref[:, sl] = (u * s).astype(o_ref.dtype)


def _gmlp(p_lo, ln_g, ln_b, ws, bs_t):
    return pl.pallas_call(
        _gmlp_kernel,
        grid=(T_ROWS // A_CHUNK,),
        in_specs=[pl.BlockSpec((A_CHUNK, A_WIDTH), lambda i: (i, 0)),
                  pl.BlockSpec((A_CHUNK, A_WIDTH), lambda i: (i, 1)),
                  pl.BlockSpec((1, A_WIDTH), lambda i: (0, 0)),
                  pl.BlockSpec((1, A_WIDTH), lambda i: (0, 0)),
                  pl.BlockSpec((A_HEADS, A_CHUNK, A_CHUNK), lambda i: (0, 0, 0)),
                  pl.BlockSpec((A_CHUNK, A_HEADS), lambda i: (0, 0))],
        out_specs=pl.BlockSpec((A_CHUNK, A_WIDTH), lambda i: (i, 0)),
        out_shape=jax.ShapeDtypeStruct((T_ROWS, A_WIDTH), BF16),
        compiler_params=_cparams(("parallel",)),
        name="gmlp",
    )(p_lo, p_lo, ln_g, ln_b, ws, bs_t)


NQ_LAT = SEQ // ATT_TQ


def _attn_kernel(lam_ref, q_ref, kl_ref, vl_ref, kc_ref, vc_ref, g_ref, o_ref, m_scr, l_scr, acc_scr):
    qi = pl.program_id(2)
    m_scr[...] = jnp.full(m_scr.shape, -1e30, F32)
    l_scr[...] = jnp.zeros(l_scr.shape, F32)
    acc_scr[...] = jnp.zeros(acc_scr.shape, F32)

    def process(k, v):
        for mp in range(2):
            sl = slice(mp * HEAD_DIM, (mp + 1) * HEAD_DIM)
            s = _dot_nt(q_ref[:, sl], k[:, sl])
            cols = [s[:, c * LANE:(c + 1) * LANE] for c in range(k.shape[0] // LANE)]
            m_part = cols[0]
            for col in cols[1:]:
                m_part = jnp.maximum(m_part, col)
            m_old = m_scr[mp]
            m_new = jnp.maximum(m_old, jnp.max(m_part, axis=-1, keepdims=True))
            alpha = jnp.exp2(m_old - m_new)
            l_part = alpha * l_scr[mp]
            ps = []
            for col in cols:
                p = jnp.exp2(col - m_new)
                l_part = l_part + p
                ps.append(p.astype(BF16))
            pv = jnp.dot(jnp.concatenate(ps, axis=1), v, preferred_element_type=F32)
            for half in range(B_V_DIM // LANE):
                hs = slice(half * LANE, (half + 1) * LANE)
                acc_scr[mp, :, hs] = alpha * acc_scr[mp, :, hs] + pv[:, hs]
            l_scr[mp] = l_part
            m_scr[mp] = m_new

    process(kc_ref[...], vc_ref[...])

    @pl.when(qi < NQ_LAT)
    def _():
        for j in range(SEQ // ATT_TK):
            rows = slice(j * ATT_TK, (j + 1) * ATT_TK)
            process(kl_ref[rows, :], vl_ref[rows, :])

    lam = lam_ref[0]
    post = lam_ref[1]
    l0 = jnp.sum(l_scr[0], axis=-1, keepdims=True)
    l1 = jnp.sum(l_scr[1], axis=-1, keepdims=True)
    o = acc_scr[0] / l0 - lam * (acc_scr[1] / l1)
    r = o * lax.rsqrt(jnp.mean(jnp.square(o), axis=-1, keepdims=True) + NORM_EPS)
    o_ref[...] = (r * g_ref[...] * post).astype(o_ref.dtype)


def _diff_attention(p_lo, lam2, subln_g):
    def q_rows(b, h, qi):
        return jnp.where(qi < NQ_LAT, b * NQ_LAT + qi, BATCH * NQ_LAT + b)

    return pl.pallas_call(
        _attn_kernel,
        grid=(BATCH, B_HEADS, NQ_LAT + 1),
        in_specs=[pl.BlockSpec(memory_space=pltpu.SMEM),
                  pl.BlockSpec((ATT_TQ, B_V_DIM), lambda b, h, qi: (q_rows(b, h, qi), 4 + h)),
                  pl.BlockSpec((SEQ, B_V_DIM), lambda b, h, qi: (b, 8 + h)),
                  pl.BlockSpec((SEQ, B_V_DIM), lambda b, h, qi: (b, 12 + h)),
                  pl.BlockSpec((CTX_LEN, B_V_DIM), lambda b, h, qi: (N_LAT // CTX_LEN + b, 8 + h)),
                  pl.BlockSpec((CTX_LEN, B_V_DIM), lambda b, h, qi: (N_LAT // CTX_LEN + b, 12 + h)),
                  pl.BlockSpec((1, B_V_DIM), lambda b, h, qi: (0, 0))],
        out_specs=pl.BlockSpec((ATT_TQ, B_V_DIM), lambda b, h, qi: (q_rows(b, h, qi), h)),
        scratch_shapes=[pltpu.VMEM((2, ATT_TQ, LANE), F32),
                        pltpu.VMEM((2, ATT_TQ, LANE), F32),
                        pltpu.VMEM((2, ATT_TQ, B_V_DIM), F32)],
        out_shape=jax.ShapeDtypeStruct((T_ROWS, B_WIDTH), BF16),
        compiler_params=_cparams(("parallel", "parallel", "arbitrary")),
        name="diff_attention",
    )(lam2, p_lo, p_lo, p_lo, p_lo, p_lo, subln_g)


def _scan_structure():
    c = SCAN_C
    x = np.zeros((2, (2 + SCAN_LEVELS) * c, c), np.float32)
    msk = np.zeros((2, SCAN_LEVELS + 1, c, c), np.float32)
    r = np.arange(c)
    j = np.arange(c)[None, :]
    xf = np.zeros(((2 + SCAN_LEVELS) * c, c), np.float32)
    mf = np.zeros((SCAN_LEVELS + 1, c, c), np.float32)
    xf[0:c] = (j <= r[:, None])
    xf[c:2 * c] = (j > r[:, None])
    for lev in range(SCAN_LEVELS):
        half = (c // 2) >> lev
        start = (r // (2 * half)) * (2 * half)
        mid = start + half - 1
        later = r > mid
        rows = np.where(later[:, None], (j > mid[:, None]) & (j <= r[:, None]),
                        (j > r[:, None]) & (j <= mid[:, None]))
        xf[(2 + lev) * c:(3 + lev) * c] = rows
        same = start[:, None] == start[None, :]
        mf[lev] = same & later[:, None] & (~later)[None, :]
    mf[SCAN_LEVELS] = np.eye(c)
    x[0] = xf
    msk[0] = mf
    x[1] = xf.reshape(2 + SCAN_LEVELS, c, c)[:, ::-1, ::-1].reshape(-1, c)
    msk[1] = mf[:, ::-1, ::-1]
    return x, msk


def _hgrn_kernel(qf_ref, ff_ref, if_ref, qb_ref, fb_ref, ib_ref, lb_ref, x_ref, msk_ref,
                 of_ref, ob_ref, st_scr):
    j = pl.program_id(1)

    @pl.when(j == 0)
    def _():
        st_scr[...] = jnp.zeros(st_scr.shape, F32)

    c = SCAN_C
    dirs = ((qf_ref, ff_ref, if_ref, of_ref, c - 1), (qb_ref, fb_ref, ib_ref, ob_ref, 0))
    for d, (q_ref, f_ref, i_ref, o_ref, end_row) in enumerate(dirs):
        for h in range(C_HEADS):
            sl = slice(h * HEAD_DIM, (h + 1) * HEAD_DIM)
            qraw = q_ref[:, sl]
            q = qraw * _sigmoid(qraw)
            lb = lb_ref[d:d + 1, sl]
            f = lb + (1.0 - lb) * _sigmoid(f_ref[:, sl])
            logf = jnp.log(f)
            k = 1.0 - f
            v = i_ref[:, sl].astype(BF16)
            hi = logf.astype(BF16)
            lo = (logf - hi.astype(F32)).astype(BF16)
            e2 = jnp.dot(x_ref[d], jnp.concatenate([hi, lo], axis=1), preferred_element_type=F32)
            w = jnp.exp(e2[:, 0:c] + e2[:, c:2 * c])
            a = msk_ref[d, SCAN_LEVELS] * _dot_nt(q.astype(BF16), k.astype(BF16))
            for lev in range(SCAN_LEVELS):
                wl = w[(2 + lev) * c:(3 + lev) * c]
                a = a + msk_ref[d, lev] * _dot_nt((q * wl).astype(BF16), (k * wl).astype(BF16))
            st = st_scr[d, h]
            o = jnp.dot(a.astype(BF16), v, preferred_element_type=F32)
            o = o + _dot_nt((q * w[0:c]).astype(BF16), st.astype(BF16))
            o_ref[:, sl] = o
            g_end = w[end_row:end_row + 1, :]
            st_scr[d, h] = st * g_end + _dot_tn(v, (k * w[c:2 * c]).astype(BF16))


def _hgrn_scan(p_hi, lb2, xmat, masks):
    lat_chunks = SEQ // SCAN_C
    ctx_chunks = CTX_LEN // SCAN_C
    steps = ctx_chunks + lat_chunks

    def fwd_rows(b, j):
        return jnp.where(j < ctx_chunks, N_LAT // SCAN_C + b * ctx_chunks + j,
                         b * lat_chunks + j - ctx_chunks)

    def bwd_rows(b, j):
        return jnp.where(j < ctx_chunks, N_LAT // SCAN_C + b * ctx_chunks + (ctx_chunks - 1 - j),
                         b * lat_chunks + (steps - 1 - j))

    def spec(rows, col):
        return pl.BlockSpec((SCAN_C, C_WIDTH), lambda b, j: (rows(b, j), col))

    nx = (2 + SCAN_LEVELS) * SCAN_C
    return pl.pallas_call(
        _hgrn_kernel,
        grid=(BATCH, steps),
        in_specs=[spec(fwd_rows, 0), spec(fwd_rows, 1), spec(fwd_rows, 3),
                  spec(bwd_rows, 0), spec(bwd_rows, 2), spec(bwd_rows, 3),
                  pl.BlockSpec((2, C_WIDTH), lambda b, j: (0, 0)),
                  pl.BlockSpec((2, nx, SCAN_C), lambda b, j: (0, 0, 0)),
                  pl.BlockSpec((2, SCAN_LEVELS + 1, SCAN_C, SCAN_C), lambda b, j: (0, 0, 0, 0))],
        out_specs=[spec(fwd_rows, 0), spec(bwd_rows, 0)],
        out_shape=[jax.ShapeDtypeStruct((T_ROWS, C_WIDTH), F32)] * 2,
        scratch_shapes=[pltpu.VMEM((2, C_HEADS, HEAD_DIM, HEAD_DIM), F32)],
        compiler_params=_cparams(("parallel", "arbitrary")),
        name="hgrn2_scan",
    )(p_hi, p_hi, p_hi, p_hi, p_hi, p_hi, lb2, xmat, masks)


def _layer_norm_rows(z, g, b):
    mu = jnp.mean(z, axis=-1, keepdims=True)
    var = jnp.mean(jnp.square(z - mu), axis=-1, keepdims=True)
    return (z - mu) * lax.rsqrt(var + LN_EPS) * g + b


def _outproj_kernel(a_ref, b_ref, of_ref, ob_ref, cg_ref, hg_ref, x_ref, mod_ref, w_ref,
                    l1g_ref, l1b_ref, rw_ref, rb_ref,
                    x1_ref, hx_ref, hxp_ref, idx_ref, wt_ref, rank_ref, cnt_ref, cnt_scr):
    parts = []
    for h in range(C_HEADS):
        sl = slice(h * HEAD_DIM, (h + 1) * HEAD_DIM)
        o = of_ref[:, sl] + ob_ref[:, sl]
        on = o * lax.rsqrt(jnp.mean(jnp.square(o), axis=-1, keepdims=True) + NORM_EPS) * hg_ref[...]
        g = cg_ref[:, sl]
        parts.append((on * (g * _sigmoid(g))).astype(BF16))
    c_x = jnp.concatenate(parts, axis=1)
    mix = jnp.dot(a_ref[...], w_ref[0:A_WIDTH, :], preferred_element_type=F32)
    mix = mix + jnp.dot(b_ref[...], w_ref[A_WIDTH:A_WIDTH + B_WIDTH, :], preferred_element_type=F32)
    mix = mix + jnp.dot(c_x, w_ref[A_WIDTH + B_WIDTH:, :], preferred_element_type=F32)
    g1 = mod_ref[0, 2:3, :]
    sh2 = mod_ref[0, 3:4, :]
    sc2 = mod_ref[0, 4:5, :]
    x1 = _layer_norm_rows(DEEPNORM_ALPHA * x_ref[...] + g1 * mix, l1g_ref[...], l1b_ref[...])
    x1_ref[...] = x1
    hx = x1 * (1.0 + sc2) + sh2
    hx_bf = hx.astype(BF16)
    hx_ref[...] = hx_bf
    hxp_ref[...] = _pack_pairs(hx)

    logits = jnp.dot(hx.astype(BF16), rw_ref[...], preferred_element_type=F32)
    scores = _sigmoid(logits)
    sel = scores + rb_ref[...]
    tm = sel.shape[0]
    lane = lax.broadcasted_iota(jnp.int32, sel.shape, 1).astype(F32)
    slot = lax.broadcasted_iota(jnp.int32, (tm, LANE), 1)
    idx_acc = jnp.zeros((tm, LANE), F32)
    wt_acc = jnp.zeros((tm, LANE), F32)
    chosen = jnp.zeros(sel.shape, F32)
    hits = []
    for k in range(TOP_K):
        mx = jnp.max(sel, axis=-1, keepdims=True)
        idx = jnp.min(jnp.where(sel == mx, lane, float(N_EXPERTS)), axis=-1, keepdims=True)
        hit = lane == idx
        hits.append(hit)
        chosen = jnp.where(hit, 1.0, chosen)
        w_k = jnp.sum(jnp.where(hit, scores, 0.0), axis=-1, keepdims=True)
        idx_acc = jnp.where(slot == k, idx, idx_acc)
        wt_acc = jnp.where(slot == k, w_k, wt_acc)
        sel = jnp.where(hit, -jnp.inf, sel)
    idx_ref[...] = idx_acc.astype(jnp.int32)
    wt_ref[...] = wt_acc / jnp.sum(wt_acc, axis=-1, keepdims=True) * ROUTED_SCALE

    @pl.when(pl.program_id(0) == 0)
    def _():
        cnt_scr[...] = jnp.zeros(cnt_scr.shape, F32)

    below = (lax.broadcasted_iota(jnp.int32, (tm, tm), 0) > lax.broadcasted_iota(jnp.int32, (tm, tm), 1))
    before = jnp.dot(below.astype(BF16), chosen.astype(BF16), preferred_element_type=F32) + cnt_scr[0:1, :]
    rank_acc = jnp.zeros((tm, LANE), F32)
    for k in range(TOP_K):
        r_k = jnp.sum(jnp.where(hits[k], before, 0.0), axis=-1, keepdims=True)
        rank_acc = jnp.where(slot == k, r_k, rank_acc)
    rank_ref[...] = rank_acc.astype(jnp.int32)
    cnt = cnt_scr[...] + jnp.sum(chosen, axis=0, keepdims=True)
    cnt_scr[...] = cnt
    cnt_ref[...] = cnt


def _outproj(a_x, b_x, o_f, o_b, p_hi, hnorm_g, x, mod3, w_out, ln_g, ln_b, router_w, router_b):
    row = lambda width: pl.BlockSpec((OUT_TM, width), lambda i: (i, 0))
    const = lambda shape: pl.BlockSpec(shape, lambda i: tuple(0 for _ in shape))
    return pl.pallas_call(
        _outproj_kernel,
        grid=(T_ROWS // OUT_TM,),
        in_specs=[row(A_WIDTH), row(B_WIDTH), row(C_WIDTH), row(C_WIDTH),
                  pl.BlockSpec((OUT_TM, C_WIDTH), lambda i: (i, 4)),
                  const((1, HEAD_DIM)),
                  row(D_MODEL),
                  pl.BlockSpec((1, 6, D_MODEL), lambda i: (_mod_row(i, OUT_TM), 0, 0)),
                  const((D_MODEL, D_MODEL)),
                  const((1, D_MODEL)), const((1, D_MODEL)),
                  const((D_MODEL, N_EXPERTS)), const((1, N_EXPERTS))],
        out_specs=[row(D_MODEL), row(D_MODEL), row(D_MODEL // 2), row(LANE), row(LANE), row(LANE),
                   const((8, N_EXPERTS))],
        out_shape=[jax.ShapeDtypeStruct((T_ROWS, D_MODEL), F32),
                   jax.ShapeDtypeStruct((T_ROWS, D_MODEL), BF16),
                   jax.ShapeDtypeStruct((T_ROWS, D_MODEL // 2), jnp.uint32),
                   jax.ShapeDtypeStruct((T_ROWS, LANE), jnp.int32),
                   jax.ShapeDtypeStruct((T_ROWS, LANE), F32),
                   jax.ShapeDtypeStruct((T_ROWS, LANE), jnp.int32),
                   jax.ShapeDtypeStruct((8, N_EXPERTS), F32)],
        scratch_shapes=[pltpu.VMEM((8, N_EXPERTS), F32)],
        compiler_params=_cparams(("arbitrary",)),
        name="outproj_ln_router",
    )(a_x, b_x, o_f, o_b, p_hi, hnorm_g, x, mod3, w_out, ln_g, ln_b, router_w, router_b)


def _experts_kernel(ord_ref, elist_ref, cnt_ref, xs_ref, wg_hbm, wu_hbm, wd_hbm, ys_ref,
                    wg_buf, wu_buf, wd_buf, wgu_scr, wd_scr, sems, *, packed, layer):
    i = pl.program_id(0)
    live = i < cnt_ref[0]
    o = ord_ref[i]
    first_of_expert = jnp.logical_or(i == 0, o != ord_ref[jnp.maximum(i - 1, 0)])

    def weight_copies(which, slot):
        e = elist_ref[which]
        return (pltpu.make_async_copy(wg_hbm.at[layer, e], wg_buf.at[slot], sems.at[slot, 0]),
                pltpu.make_async_copy(wu_hbm.at[layer, e], wu_buf.at[slot], sems.at[slot, 1]),
                pltpu.make_async_copy(wd_hbm.at[layer, e], wd_buf.at[slot], sems.at[slot, 2]))

    @pl.when(i == 0)
    def _():
        for c in weight_copies(0, 0):
            c.start()

    @pl.when(jnp.logical_and(live, first_of_expert))
    def _():
        slot = o % 2
        for c in weight_copies(o, slot):
            c.wait()
        wgu_scr[:, :EXPERT_DIM] = wg_buf[slot].astype(BF16)
        wgu_scr[:, EXPERT_DIM:] = wu_buf[slot].astype(BF16)
        wd_scr[...] = wd_buf[slot].astype(BF16)

        @pl.when(o + 1 < cnt_ref[1])
        def _():
            for c in weight_copies(o + 1, 1 - slot):
                c.start()

    @pl.when(live)
    def _():
        if packed:
            low, high = _unpack_pairs(xs_ref[...])
            x = jnp.concatenate([low.astype(BF16), high.astype(BF16)], axis=1)
        else:
            x = xs_ref[...]
        h = jnp.dot(x, wgu_scr[...], preferred_element_type=F32)
        g = h[:, :EXPERT_DIM]
        u = h[:, EXPERT_DIM:]
        a = (g * _sigmoid(g) * u).astype(BF16)
        y = jnp.dot(a, wd_scr[...], preferred_element_type=F32)
        ys_ref[...] = _pack_pairs(y) if packed else y.astype(ys_ref.dtype)

    @pl.when(jnp.logical_not(live))
    def _():
        ys_ref[...] = jnp.zeros(ys_ref.shape, ys_ref.dtype)


def _experts(tile_ord, expert_list, counts2, xs, w_gate, w_up, w_down, layer):
    tiles = xs.shape[0] // MOE_TM
    last = lambda i, cnt: jnp.minimum(i, cnt[0] - 1)
    grid_spec = pltpu.PrefetchScalarGridSpec(
        num_scalar_prefetch=3,
        grid=(tiles,),
        in_specs=[pl.BlockSpec((MOE_TM, xs.shape[1]), lambda i, od, el, cnt: (last(i, cnt), 0)),
                  pl.BlockSpec(memory_space=pl.ANY),
                  pl.BlockSpec(memory_space=pl.ANY),
                  pl.BlockSpec(memory_space=pl.ANY)],
        out_specs=pl.BlockSpec((MOE_TM, xs.shape[1]), lambda i, od, el, cnt: (i, 0)),
        scratch_shapes=[pltpu.VMEM((2, D_MODEL, EXPERT_DIM), F32),
                        pltpu.VMEM((2, D_MODEL, EXPERT_DIM), F32),
                        pltpu.VMEM((2, EXPERT_DIM, D_MODEL), F32),
                        pltpu.VMEM((D_MODEL, 2 * EXPERT_DIM), BF16),
                        pltpu.VMEM((EXPERT_DIM, D_MODEL), BF16),
                        pltpu.SemaphoreType.DMA((2, 3))],
    )
    return pl.pallas_call(
        functools.partial(_experts_kernel, packed=xs.dtype == jnp.uint32, layer=layer),
        grid_spec=grid_spec,
        out_shape=jax.ShapeDtypeStruct(xs.shape, xs.dtype),
        compiler_params=_cparams(("arbitrary",), vmem=EXPERTS_VMEM_LIMIT),
        name="experts",
    )(tile_ord, expert_list, counts2, xs, w_gate, w_up, w_down)


def _combine_kernel(yg_ref, wt_ref, ysh_ref, x1_ref, mod_ref, g_ref, b_ref, o_ref):
    tm = COMBINE_TM
    low = jnp.zeros((tm, D_MODEL // 2), F32)
    high = jnp.zeros((tm, D_MODEL // 2), F32)
    for k in range(TOP_K):
        lo_k, hi_k = _unpack_pairs(yg_ref[k])
        low = low + wt_ref[:, k:k + 1] * lo_k
        high = high + wt_ref[:, k:k + 1] * hi_k
    y = ysh_ref[...].astype(F32) + jnp.concatenate([low, high], axis=1)
    g2 = mod_ref[0, 5:6, :]
    o_ref[...] = _layer_norm_rows(DEEPNORM_ALPHA * x1_ref[...] + g2 * y, g_ref[...], b_ref[...])


def _combine(yg, wt, ysh, x1, mod3, ln_g, ln_b):
    tm = COMBINE_TM
    row = lambda width: pl.BlockSpec((tm, width), lambda i: (i, 0))
    return pl.pallas_call(
        _combine_kernel,
        grid=(T_ROWS // tm,),
        in_specs=[pl.BlockSpec((TOP_K, tm, D_MODEL // 2), lambda i: (0, i, 0)),
                  row(LANE), row(D_MODEL), row(D_MODEL),
                  pl.BlockSpec((1, 6, D_MODEL), lambda i: (_mod_row(i, tm), 0, 0)),
                  pl.BlockSpec((1, D_MODEL), lambda i: (0, 0)),
                  pl.BlockSpec((1, D_MODEL), lambda i: (0, 0))],
        out_specs=row(D_MODEL),
        out_shape=jax.ShapeDtypeStruct((T_ROWS, D_MODEL), F32),
        compiler_params=_cparams(("parallel",)),
        name="moe_combine_ln",
    )(yg, wt, ysh, x1, mod3, ln_g, ln_b)


def _layout(idx, rank, counts):
    tiles_per = (counts + MOE_TM - 1) // MOE_TM
    tile_end = jnp.cumsum(tiles_per)
    start = (tile_end - tiles_per) * MOE_TM
    experts = jnp.arange(N_EXPERTS, dtype=jnp.int32)
    pos = rank + jnp.sum(jnp.where(idx[:, :, None] == experts, start, 0), axis=-1)
    tile = jnp.arange(MOE_TILES, dtype=jnp.int32)
    tile_expert = jnp.sum((tile_end[None, :] <= tile[:, None]).astype(jnp.int32), axis=-1)
    tile_expert = jnp.minimum(tile_expert, N_EXPERTS - 1)
    used = (counts > 0).astype(jnp.int32)
    used_rank = jnp.cumsum(used) - used
    n_used = jnp.sum(used)
    expert_list = jnp.sum(jnp.where((used_rank[None, :] == experts[:, None]) & (used[None, :] > 0),
                                    experts[None, :], 0), axis=-1)
    tile_ord = jnp.minimum(jnp.take(used_rank, tile_expert), jnp.maximum(n_used - 1, 0))
    counts2 = jnp.stack([tile_end[-1], n_used]).astype(jnp.int32)
    return pos, tile_ord.astype(jnp.int32), expert_list.astype(jnp.int32), counts2


def _dispatch_kernel(pos_ref, hxw_ref, init_hbm, xs_hbm, sem):
    def issue(t, carry):
        for k in range(TOP_K):
            pltpu.make_async_copy(hxw_ref.at[pl.ds(t, 1), :],
                                  xs_hbm.at[pl.ds(pos_ref[t * TOP_K + k], 1), :], sem).start(priority=k % 2)
        return carry

    lax.fori_loop(0, DISPATCH_TM, issue, 0)
    for _ in range(TOP_K):
        pltpu.make_async_copy(hxw_ref, xs_hbm.at[pl.ds(0, DISPATCH_TM), :], sem).wait()


def _dispatch_rows(pos, hx_words):
    init = jnp.zeros((MOE_ROWS, D_MODEL // 2), jnp.uint32)
    return pl.pallas_call(
        _dispatch_kernel,
        grid=(T_ROWS // DISPATCH_TM,),
        in_specs=[pl.BlockSpec((DISPATCH_TM * TOP_K,), lambda i: (i,), memory_space=pltpu.SMEM),
                  pl.BlockSpec((DISPATCH_TM, D_MODEL // 2), lambda i: (i, 0)),
                  pl.BlockSpec(memory_space=pl.ANY)],
        out_specs=pl.BlockSpec(memory_space=pl.ANY),
        out_shape=jax.ShapeDtypeStruct((MOE_ROWS, D_MODEL // 2), jnp.uint32),
        scratch_shapes=[pltpu.SemaphoreType.DMA(())],
        input_output_aliases={2: 0},
        compiler_params=pltpu.CompilerParams(dimension_semantics=("arbitrary",), has_side_effects=True),
        name="moe_dispatch_rows",
    )(pos.reshape(-1), hx_words, init)


def _rope_tables():
    t = jnp.arange(SEQ)
    row = (t // GRID_W).astype(F32)
    col = (t % GRID_W).astype(F32)
    axis_dim = HEAD_DIM // 2
    inv = ROPE_THETA ** (-jnp.arange(0, axis_dim, 2, dtype=F32) / axis_dim)
    ar = row[:, None] * inv
    ac = col[:, None] * inv
    cos = jnp.concatenate([jnp.cos(ar), jnp.cos(ar), jnp.cos(ac), jnp.cos(ac)], axis=-1)
    sin = jnp.concatenate([jnp.sin(ar), jnp.sin(ar), jnp.sin(ac), jnp.sin(ac)], axis=-1)
    odd = ((jnp.arange(HEAD_DIM) // 32) % 2 == 1)[None, :]
    sa = jnp.where(odd, sin, 0.0)
    sb = jnp.where(odd, 0.0, -sin)
    lat = lambda z: jnp.tile(z, (BATCH, 1))
    cos_all = jnp.concatenate([lat(cos), jnp.ones((N_CTX, HEAD_DIM), F32)], axis=0)
    sa_all = jnp.concatenate([lat(sa), jnp.zeros((N_CTX, HEAD_DIM), F32)], axis=0)
    sb_all = jnp.concatenate([lat(sb), jnp.zeros((N_CTX, HEAD_DIM), F32)], axis=0)
    return cos_all, sa_all, sb_all


def kernel(x, c, ctx, c_ctx, w_mod, b_mod, w_in, w_out, gmlp_ln_g, gmlp_ln_b, gmlp_ws, gmlp_bs,
           diff_lam, diff_subln_g, hgrn_lb, hgrn_norm_g, ln1_g, ln1_b, ln2_g, ln2_b,
           router_w, router_b, exp_w_gate, exp_w_up, exp_w_down, sh_w_gate, sh_w_up, sh_w_down):
    assert x.shape == (BATCH, SEQ, D_MODEL) and ctx.shape == (BATCH, CTX_LEN, D_MODEL)
    xs_all = jnp.concatenate([x.reshape(N_LAT, D_MODEL), ctx.reshape(N_CTX, D_MODEL)], axis=0)

    cond = jnp.zeros((8, D_MODEL), F32).at[0:BATCH].set(c).at[BATCH].set(c_ctx)
    mod_all = _modulation(cond, w_mod, b_mod)[:, :BATCH + 1].reshape(DEPTH, BATCH + 1, 6, D_MODEL)

    sm = jax.nn.softmax(hgrn_lb.astype(F32), axis=0)
    lb_all = jnp.cumsum(sm, axis=0) - sm[0]
    tables = _rope_tables()
    xmat_np, masks_np = _scan_structure()
    xmat = jnp.asarray(xmat_np, BF16)
    masks = jnp.asarray(masks_np, F32)

    for l in range(DEPTH):
        mod3 = mod_all[l]
        w_in_l = w_in[l].astype(BF16)
        p_lo = _inproj(xs_all, mod3, w_in_l[:, :LOW_WIDTH], tables, BF16, True)
        p_hi = _inproj(xs_all, mod3, w_in_l[:, LOW_WIDTH:], tables, F32, False)

        a_x = _gmlp(p_lo, gmlp_ln_g[l][None, :], gmlp_ln_b[l][None, :],
                    gmlp_ws[l].astype(BF16), gmlp_bs[l].T)

        lam_init = 0.8 - 0.6 * math.exp(-0.3 * l)
        dl = diff_lam[l].astype(F32)
        lam = jnp.exp(jnp.sum(dl[0] * dl[1])) - jnp.exp(jnp.sum(dl[2] * dl[3])) + lam_init
        lam2 = jnp.stack([lam, jnp.asarray(1.0 - lam_init, F32)]).astype(F32)
        b_x = _diff_attention(p_lo, lam2, diff_subln_g[l][None, :])

        o_f, o_b = _hgrn_scan(p_hi, lb_all[l], xmat, masks)

        x1, hx, hx_words, idx, wt, rank, cnt = _outproj(
            a_x, b_x, o_f, o_b, p_hi, hgrn_norm_g[l][None, :], xs_all, mod3,
            w_out[l].astype(BF16), ln1_g[l][None, :], ln1_b[l][None, :],
            router_w[l].astype(BF16), router_b[l][None, :])

        pos, tile_ord, expert_list, counts2 = _layout(idx[:, :TOP_K], rank[:, :TOP_K], cnt[0].astype(jnp.int32))
        xs_grouped = _dispatch_rows(pos, hx_words)
        ys = _experts(tile_ord, expert_list, counts2, xs_grouped, exp_w_gate, exp_w_up, exp_w_down, l)
        yg = jnp.take(ys, pos.T.reshape(-1), axis=0, mode="clip").reshape(TOP_K, T_ROWS, D_MODEL // 2)

        ysh = _experts(jnp.zeros((T_ROWS // MOE_TM,), jnp.int32), jnp.zeros((1,), jnp.int32),
                       jnp.array([T_ROWS // MOE_TM, 1], jnp.int32),
                       hx, sh_w_gate[:, None], sh_w_up[:, None], sh_w_down[:, None], l)

        xs_all = _combine(yg, wt, ysh, x1, mod3, ln2_g[l][None, :], ln2_b[l][None, :])

    return xs_all[:N_LAT].reshape(BATCH, SEQ, D_MODEL)
```

```python
import functools
import math

import numpy as np
import jax
import jax.numpy as jnp
from jax import lax
from jax.experimental import pallas as pl
from jax.experimental.pallas import tpu as pltpu

F32 = jnp.float32
BF16 = jnp.bfloat16

D_MODEL = 2048
BATCH = 2
SEQ = 4096
DEPTH = 4
CTX_LEN = 256
GRID_W = 64
HEAD_DIM = 128
A_WIDTH = 512
A_HEADS = 4
A_CHUNK = 128
B_WIDTH = 1024
B_HEADS = 4
B_QK_WIDTH = 1024
B_V_DIM = 256
ROPE_THETA = 10000.0
C_WIDTH = 512
C_HEADS = 4
IN_WIDTH = 6656
N_EXPERTS = 64
TOP_K = 8
EXPERT_DIM = 384
ROUTED_SCALE = 2.5
DEEPNORM_ALPHA = (2.0 * DEPTH) ** 0.25
LN_EPS = 1e-5
NORM_EPS = 1e-5

N_LAT = BATCH * SEQ
N_CTX = BATCH * CTX_LEN
T_ROWS = N_LAT + N_CTX
LOW_WIDTH = 4096
HI_WIDTH = IN_WIDTH - LOW_WIDTH
QK_SCALE = HEAD_DIM ** -0.5 * math.log2(math.e)

LANE = 128
PROJ_TM = 512
PROJ_TN = 512
ATT_TQ = 512
ATT_TK = 1024
SCAN_C = 128
SCAN_LEVELS = 7
OUT_TM = 256
MOE_TM = 512
COMBINE_TM = 256
DISPATCH_TM = 256
MOE_ROWS = ((T_ROWS * TOP_K + N_EXPERTS * (MOE_TM - 1)) // MOE_TM) * MOE_TM
MOE_TILES = MOE_ROWS // MOE_TM
VMEM_LIMIT = 48 * 1024 * 1024
EXPERTS_VMEM_LIMIT = 56 * 1024 * 1024


def _cparams(sem, vmem=VMEM_LIMIT):
    return pltpu.CompilerParams(dimension_semantics=sem, vmem_limit_bytes=vmem)


def _sigmoid(x):
    return 1.0 / (1.0 + jnp.exp(-x))


def _gelu_tanh(x):
    return 0.5 * x * (1.0 + jnp.tanh(math.sqrt(2.0 / math.pi) * (x + 0.044715 * (x * x * x))))


def _pack_pairs(x):
    n = x.shape[1] // 2
    bits = pltpu.bitcast(x.astype(BF16).astype(F32), jnp.uint32)
    return bits[:, n:] | (bits[:, :n] >> 16)


def _unpack_pairs(words):
    low = pltpu.bitcast(words << 16, F32)
    high = pltpu.bitcast(words & jnp.uint32(0xFFFF0000), F32)
    return low, high


def _dot_nt(a, b):
    return lax.dot_general(a, b, (((1,), (1,)), ((), ())), preferred_element_type=F32)


def _dot_tn(a, b):
    return lax.dot_general(a, b, (((0,), (0,)), ((), ())), preferred_element_type=F32)


def _mod_kernel(c_ref, w_ref, b_ref, o_ref):
    c = c_ref[...]
    s = (c * _sigmoid(c)).astype(BF16)
    o_ref[0] = jnp.dot(s, w_ref[0].astype(BF16), preferred_element_type=F32) + b_ref[0]


def _modulation(cond, w_mod, b_mod):
    tn = 1024
    return pl.pallas_call(
        _mod_kernel,
        grid=(DEPTH, 6 * D_MODEL // tn),
        in_specs=[pl.BlockSpec((8, D_MODEL), lambda l, n: (0, 0)),
                  pl.BlockSpec((1, D_MODEL, tn), lambda l, n: (l, 0, n)),
                  pl.BlockSpec((1, 1, tn), lambda l, n: (l, 0, n))],
        out_specs=pl.BlockSpec((1, 8, tn), lambda l, n: (l, 0, n)),
        out_shape=jax.ShapeDtypeStruct((DEPTH, 8, 6 * D_MODEL), F32),
        compiler_params=_cparams(("parallel", "parallel")),
        name="modulation",
    )(cond, w_mod, b_mod.reshape(DEPTH, 1, 6 * D_MODEL))


def _mod_row(m, tm):
    return jnp.minimum(m // (SEQ // tm), BATCH)


def _inproj_kernel(x_ref, mod_ref, w_ref, cos_ref, sa_ref, sb_ref, o_ref, h_scr, *, rope):
    n = pl.program_id(1)

    @pl.when(n == 0)
    def _():
        sh = mod_ref[0, 0:1, :]
        sc = mod_ref[0, 1:2, :]
        h_scr[...] = (x_ref[...] * (1.0 + sc) + sh).astype(BF16)

    acc = jnp.dot(h_scr[...], w_ref[...], preferred_element_type=F32)
    if not rope:
        o_ref[...] = acc.astype(o_ref.dtype)
        return

    is_qk = (n >= 2) & (n < 6)

    @pl.when(is_qk)
    def _():
        scale = jnp.where(n < 4, QK_SCALE, 1.0).astype(F32)
        cos = cos_ref[...]
        sa = sa_ref[...]
        sb = sb_ref[...]
        for g in range(PROJ_TN // LANE):
            blk = acc[:, g * LANE:(g + 1) * LANE]
            r = blk * cos + pltpu.roll(blk, 32, 1) * sa + pltpu.roll(blk, 96, 1) * sb
            o_ref[:, g * LANE:(g + 1) * LANE] = (r * scale).astype(o_ref.dtype)

    @pl.when(jnp.logical_not(is_qk))
    def _():
        o_ref[...] = acc.astype(o_ref.dtype)


def _inproj(x, mod3, w, col0, width, tables, out_dtype, rope):
    cos, sa, sb = tables
    tab_spec = pl.BlockSpec((PROJ_TM, LANE), lambda m, n: (m, 0))
    return pl.pallas_call(
        functools.partial(_inproj_kernel, rope=rope),
        grid=(T_ROWS // PROJ_TM, width // PROJ_TN),
        in_specs=[pl.BlockSpec((PROJ_TM, D_MODEL), lambda m, n: (m, 0)),
                  pl.BlockSpec((1, 6, D_MODEL), lambda m, n: (_mod_row(m, PROJ_TM), 0, 0)),
                  pl.BlockSpec((D_MODEL, PROJ_TN), lambda m, n: (0, col0 // PROJ_TN + n)),
                  tab_spec, tab_spec, tab_spec],
        out_specs=pl.BlockSpec((PROJ_TM, PROJ_TN), lambda m, n: (m, n)),
        out_shape=jax.ShapeDtypeStruct((T_ROWS, width), out_dtype),
        scratch_shapes=[pltpu.VMEM((PROJ_TM, D_MODEL), BF16)],
        compiler_params=_cparams(("parallel", "arbitrary")),
        name="inproj_rope" if rope else "inproj",
    )(x, mod3, w, cos, sa, sb)


def _gmlp_kernel(u_ref, v_ref, g_ref, b_ref, ws_ref, bs_ref, o_ref):
    for h in range(A_HEADS):
        sl = slice(h * HEAD_DIM, (h + 1) * HEAD_DIM)
        v = _gelu_tanh(v_ref[:, sl].astype(F32))
        mu = jnp.mean(v, axis=-1, keepdims=True)
        var = jnp.mean(jnp.square(v - mu), axis=-1, keepdims=True)
        vn = (v - mu) * lax.rsqrt(var + LN_EPS) * g_ref[:, sl] + b_ref[:, sl]
        s = jnp.dot(ws_ref[h], vn.astype(BF16), preferred_element_type=F32) + bs_ref[:, h:h + 1]
        u = _gelu_tanh(u_ref[:, sl].astype(F32))
        o_ref[:, sl] = (u * s).astype(o_ref.dtype)


def _gmlp(p_lo, ln_g, ln_b, ws, bs_t):
    return pl.pallas_call(
        _gmlp_kernel,
        grid=(T_ROWS // A_CHUNK,),
        in_specs=[pl.BlockSpec((A_CHUNK, A_WIDTH), lambda i: (i, 0)),
                  pl.BlockSpec((A_CHUNK, A_WIDTH), lambda i: (i, 1)),
                  pl.BlockSpec((1, A_WIDTH), lambda i: (0, 0)),
                  pl.BlockSpec((1, A_WIDTH), lambda i: (0, 0)),
                  pl.BlockSpec((A_HEADS, A_CHUNK, A_CHUNK), lambda i: (0, 0, 0)),
                  pl.BlockSpec((A_CHUNK, A_HEADS), lambda i: (0, 0))],
        out_specs=pl.BlockSpec((A_CHUNK, A_WIDTH), lambda i: (i, 0)),
        out_shape=jax.ShapeDtypeStruct((T_ROWS, A_WIDTH), BF16),
        compiler_params=_cparams(("parallel",)),
        name="gmlp",
    )(p_lo, p_lo, ln_g, ln_b, ws, bs_t)


ATT_LAT_STEPS = N_LAT // ATT_TQ
ATT_STEPS_PER_BATCH = SEQ // ATT_TQ


def _attn_kernel(lam_ref, q_ref, kl_ref, vl_ref, kc_ref, vc_ref, g_ref, o_ref, m_scr, l_scr, acc_scr):
    step = pl.program_id(1)
    m_scr[...] = jnp.full(m_scr.shape, -1e30, F32)
    l_scr[...] = jnp.zeros(l_scr.shape, F32)
    acc_scr[...] = jnp.zeros(acc_scr.shape, F32)

    def process(k, v, visible=None):
        for mp in range(2):
            sl = slice(mp * HEAD_DIM, (mp + 1) * HEAD_DIM)
            s = _dot_nt(q_ref[:, sl], k[:, sl])
            if visible is not None:
                s = jnp.where(visible, s, -1e30)
            cols = [s[:, c * LANE:(c + 1) * LANE] for c in range(k.shape[0] // LANE)]
            m_part = cols[0]
            for col in cols[1:]:
                m_part = jnp.maximum(m_part, col)
            m_old = m_scr[mp]
            m_new = jnp.maximum(m_old, jnp.max(m_part, axis=-1, keepdims=True))
            alpha = jnp.exp2(m_old - m_new)
            l_part = alpha * l_scr[mp]
            ps = []
            for col in cols:
                p = jnp.exp2(col - m_new)
                l_part = l_part + p
                ps.append(p.astype(BF16))
            pv = jnp.dot(jnp.concatenate(ps, axis=1), v, preferred_element_type=F32)
            for half in range(B_V_DIM // LANE):
                hs = slice(half * LANE, (half + 1) * LANE)
                acc_scr[mp, :, hs] = alpha * acc_scr[mp, :, hs] + pv[:, hs]
            l_scr[mp] = l_part
            m_scr[mp] = m_new

    @pl.when(step < ATT_LAT_STEPS)
    def _():
        b = step // ATT_STEPS_PER_BATCH
        ctx_rows = pl.ds(pl.multiple_of(b * CTX_LEN, CTX_LEN), CTX_LEN)
        process(kc_ref[ctx_rows, :], vc_ref[ctx_rows, :])
        for j in range(SEQ // ATT_TK):
            rows = slice(j * ATT_TK, (j + 1) * ATT_TK)
            process(kl_ref[rows, :], vl_ref[rows, :])

    @pl.when(step == ATT_LAT_STEPS)
    def _():
        q_batch = lax.broadcasted_iota(jnp.int32, (ATT_TQ, N_CTX), 0) // CTX_LEN
        k_batch = lax.broadcasted_iota(jnp.int32, (ATT_TQ, N_CTX), 1) // CTX_LEN
        process(kc_ref[...], vc_ref[...], q_batch == k_batch)

    lam = lam_ref[0]
    post = lam_ref[1]
    l0 = jnp.sum(l_scr[0], axis=-1, keepdims=True)
    l1 = jnp.sum(l_scr[1], axis=-1, keepdims=True)
    o = acc_scr[0] / l0 - lam * (acc_scr[1] / l1)
    r = o * lax.rsqrt(jnp.mean(jnp.square(o), axis=-1, keepdims=True) + NORM_EPS)
    o_ref[...] = (r * g_ref[...] * post).astype(o_ref.dtype)


def _diff_attention(p_lo, lam2, subln_g):
    assert ATT_TQ == N_CTX
    lat_batch = lambda s: jnp.minimum(s // ATT_STEPS_PER_BATCH, BATCH - 1)
    return pl.pallas_call(
        _attn_kernel,
        grid=(B_HEADS, ATT_LAT_STEPS + 1),
        in_specs=[pl.BlockSpec(memory_space=pltpu.SMEM),
                  pl.BlockSpec((ATT_TQ, B_V_DIM), lambda h, s: (s, 4 + h)),
                  pl.BlockSpec((SEQ, B_V_DIM), lambda h, s: (lat_batch(s), 8 + h)),
                  pl.BlockSpec((SEQ, B_V_DIM), lambda h, s: (lat_batch(s), 12 + h)),
                  pl.BlockSpec((N_CTX, B_V_DIM), lambda h, s: (N_LAT // N_CTX, 8 + h)),
                  pl.BlockSpec((N_CTX, B_V_DIM), lambda h, s: (N_LAT // N_CTX, 12 + h)),
                  pl.BlockSpec((1, B_V_DIM), lambda h, s: (0, 0))],
        out_specs=pl.BlockSpec((ATT_TQ, B_V_DIM), lambda h, s: (s, h)),
        scratch_shapes=[pltpu.VMEM((2, ATT_TQ, LANE), F32),
                        pltpu.VMEM((2, ATT_TQ, LANE), F32),
                        pltpu.VMEM((2, ATT_TQ, B_V_DIM), F32)],
        out_shape=jax.ShapeDtypeStruct((T_ROWS, B_WIDTH), BF16),
        compiler_params=_cparams(("parallel", "arbitrary")),
        name="diff_attention",
    )(lam2, p_lo, p_lo, p_lo, p_lo, p_lo, subln_g)


def _scan_structure():
    c = SCAN_C
    x = np.zeros((2, (2 + SCAN_LEVELS) * c, c), np.float32)
    msk = np.zeros((2, SCAN_LEVELS + 1, c, c), np.float32)
    r = np.arange(c)
    j = np.arange(c)[None, :]
    xf = np.zeros(((2 + SCAN_LEVELS) * c, c), np.float32)
    mf = np.zeros((SCAN_LEVELS + 1, c, c), np.float32)
    xf[0:c] = (j <= r[:, None])
    xf[c:2 * c] = (j > r[:, None])
    for lev in range(SCAN_LEVELS):
        half = (c // 2) >> lev
        start = (r // (2 * half)) * (2 * half)
        mid = start + half - 1
        later = r > mid
        rows = np.where(later[:, None], (j > mid[:, None]) & (j <= r[:, None]),
                        (j > r[:, None]) & (j <= mid[:, None]))
        xf[(2 + lev) * c:(3 + lev) * c] = rows
        same = start[:, None] == start[None, :]
        mf[lev] = same & later[:, None] & (~later)[None, :]
    mf[SCAN_LEVELS] = np.eye(c)
    x[0] = xf
    msk[0] = mf
    x[1] = xf.reshape(2 + SCAN_LEVELS, c, c)[:, ::-1, ::-1].reshape(-1, c)
    msk[1] = mf[:, ::-1, ::-1]
    return x, msk


def _hgrn_kernel(qf_ref, ff_ref, if_ref, qb_ref, fb_ref, ib_ref, lb_ref, x_ref, msk_ref,
                 of_ref, ob_ref, st_scr):
    j = pl.program_id(1)

    @pl.when(j == 0)
    def _():
        st_scr[...] = jnp.zeros(st_scr.shape, F32)

    c = SCAN_C
    dirs = ((qf_ref, ff_ref, if_ref, of_ref, c - 1), (qb_ref, fb_ref, ib_ref, ob_ref, 0))
    for d, (q_ref, f_ref, i_ref, o_ref, end_row) in enumerate(dirs):
        for h in range(C_HEADS):
            sl = slice(h * HEAD_DIM, (h + 1) * HEAD_DIM)
            qraw = q_ref[:, sl]
            q = qraw * _sigmoid(qraw)
            lb = lb_ref[d:d + 1, sl]
            f = lb + (1.0 - lb) * _sigmoid(f_ref[:, sl])
            logf = jnp.log(f)
            k = 1.0 - f
            v = i_ref[:, sl].astype(BF16)
            hi = logf.astype(BF16)
            lo = (logf - hi.astype(F32)).astype(BF16)
            e2 = jnp.dot(x_ref[d], jnp.concatenate([hi, lo], axis=1), preferred_element_type=F32)
            w = jnp.exp(e2[:, 0:c] + e2[:, c:2 * c])
            a = msk_ref[d, SCAN_LEVELS] * _dot_nt(q.astype(BF16), k.astype(BF16))
            for lev in range(SCAN_LEVELS):
                wl = w[(2 + lev) * c:(3 + lev) * c]
                a = a + msk_ref[d, lev] * _dot_nt((q * wl).astype(BF16), (k * wl).astype(BF16))
            st = st_scr[d, h]
            o = jnp.dot(a.astype(BF16), v, preferred_element_type=F32)
            o = o + _dot_nt((q * w[0:c]).astype(BF16), st.astype(BF16))
            o_ref[:, sl] = o
            g_end = w[end_row:end_row + 1, :]
            st_scr[d, h] = st * g_end + _dot_tn(v, (k * w[c:2 * c]).astype(BF16))


def _hgrn_scan(p_hi, lb2, xmat, masks):
    lat_chunks = SEQ // SCAN_C
    ctx_chunks = CTX_LEN // SCAN_C
    steps = ctx_chunks + lat_chunks

    def fwd_rows(b, j):
        return jnp.where(j < ctx_chunks, N_LAT // SCAN_C + b * ctx_chunks + j,
                         b * lat_chunks + j - ctx_chunks)

    def bwd_rows(b, j):
        return jnp.where(j < ctx_chunks, N_LAT // SCAN_C + b * ctx_chunks + (ctx_chunks - 1 - j),
                         b * lat_chunks + (steps - 1 - j))

    def spec(rows, col):
        return pl.BlockSpec((SCAN_C, C_WIDTH), lambda b, j: (rows(b, j), col))

    nx = (2 + SCAN_LEVELS) * SCAN_C
    return pl.pallas_call(
        _hgrn_kernel,
        grid=(BATCH, steps),
        in_specs=[spec(fwd_rows, 0), spec(fwd_rows, 1), spec(fwd_rows, 3),
                  spec(bwd_rows, 0), spec(bwd_rows, 2), spec(bwd_rows, 3),
                  pl.BlockSpec((2, C_WIDTH), lambda b, j: (0, 0)),
                  pl.BlockSpec((2, nx, SCAN_C), lambda b, j: (0, 0, 0)),
                  pl.BlockSpec((2, SCAN_LEVELS + 1, SCAN_C, SCAN_C), lambda b, j: (0, 0, 0, 0))],
        out_specs=[spec(fwd_rows, 0), spec(bwd_rows, 0)],
        out_shape=[jax.ShapeDtypeStruct((T_ROWS, C_WIDTH), F32)] * 2,
        scratch_shapes=[pltpu.VMEM((2, C_HEADS, HEAD_DIM, HEAD_DIM), F32)],
        compiler_params=_cparams(("parallel", "arbitrary")),
        name="hgrn2_scan",
    )(p_hi, p_hi, p_hi, p_hi, p_hi, p_hi, lb2, xmat, masks)


def _layer_norm_rows(z, g, b):
    mu = jnp.mean(z, axis=-1, keepdims=True)
    var = jnp.mean(jnp.square(z - mu), axis=-1, keepdims=True)
    return (z - mu) * lax.rsqrt(var + LN_EPS) * g + b


def _outproj_kernel(a_ref, b_ref, of_ref, ob_ref, cg_ref, hg_ref, x_ref, mod_ref, w_ref,
                    l1g_ref, l1b_ref, rw_ref, rb_ref,
                    x1_ref, hx_ref, hxp_ref, idx_ref, wt_ref, rank_ref, cnt_ref, cnt_scr):
    parts = []
    for h in range(C_HEADS):
        sl = slice(h * HEAD_DIM, (h + 1) * HEAD_DIM)
        o = of_ref[:, sl] + ob_ref[:, sl]
        on = o * lax.rsqrt(jnp.mean(jnp.square(o), axis=-1, keepdims=True) + NORM_EPS) * hg_ref[...]
        g = cg_ref[:, sl]
        parts.append((on * (g * _sigmoid(g))).astype(BF16))
    c_x = jnp.concatenate(parts, axis=1)
    mix = jnp.dot(a_ref[...], w_ref[0:A_WIDTH, :], preferred_element_type=F32)
    mix = mix + jnp.dot(b_ref[...], w_ref[A_WIDTH:A_WIDTH + B_WIDTH, :], preferred_element_type=F32)
    mix = mix + jnp.dot(c_x, w_ref[A_WIDTH + B_WIDTH:, :], preferred_element_type=F32)
    g1 = mod_ref[0, 2:3, :]
    sh2 = mod_ref[0, 3:4, :]
    sc2 = mod_ref[0, 4:5, :]
    x1 = _layer_norm_rows(DEEPNORM_ALPHA * x_ref[...] + g1 * mix, l1g_ref[...], l1b_ref[...])
    x1_ref[...] = x1
    hx = x1 * (1.0 + sc2) + sh2
    hx_bf = hx.astype(BF16)
    hx_ref[...] = hx_bf
    hxp_ref[...] = _pack_pairs(hx)

    logits = jnp.dot(hx.astype(BF16), rw_ref[...], preferred_element_type=F32)
    scores = _sigmoid(logits)
    sel = scores + rb_ref[...]
    tm = sel.shape[0]
    lane = lax.broadcasted_iota(jnp.int32, sel.shape, 1).astype(F32)
    slot = lax.broadcasted_iota(jnp.int32, (tm, LANE), 1)
    idx_acc = jnp.zeros((tm, LANE), F32)
    wt_acc = jnp.zeros((tm, LANE), F32)
    chosen = jnp.zeros(sel.shape, F32)
    hits = []
    for k in range(TOP_K):
        mx = jnp.max(sel, axis=-1, keepdims=True)
        idx = jnp.min(jnp.where(sel == mx, lane, float(N_EXPERTS)), axis=-1, keepdims=True)
        hit = lane == idx
        hits.append(hit)
        chosen = jnp.where(hit, 1.0, chosen)
        w_k = jnp.sum(jnp.where(hit, scores, 0.0), axis=-1, keepdims=True)
        idx_acc = jnp.where(slot == k, idx, idx_acc)
        wt_acc = jnp.where(slot == k, w_k, wt_acc)
        sel = jnp.where(hit, -jnp.inf, sel)
    idx_ref[...] = idx_acc.astype(jnp.int32)
    wt_ref[...] = wt_acc / jnp.sum(wt_acc, axis=-1, keepdims=True) * ROUTED_SCALE

    @pl.when(pl.program_id(0) == 0)
    def _():
        cnt_scr[...] = jnp.zeros(cnt_scr.shape, F32)

    below = (lax.broadcasted_iota(jnp.int32, (tm, tm), 0) > lax.broadcasted_iota(jnp.int32, (tm, tm), 1))
    before = jnp.dot(below.astype(BF16), chosen.astype(BF16), preferred_element_type=F32) + cnt_scr[0:1, :]
    rank_acc = jnp.zeros((tm, LANE), F32)
    for k in range(TOP_K):
        r_k = jnp.sum(jnp.where(hits[k], before, 0.0), axis=-1, keepdims=True)
        rank_acc = jnp.where(slot == k, r_k, rank_acc)
    rank_ref[...] = rank_acc.astype(jnp.int32)
    cnt = cnt_scr[...] + jnp.sum(chosen, axis=0, keepdims=True)
    cnt_scr[...] = cnt
    cnt_ref[...] = cnt


def _outproj(a_x, b_x, o_f, o_b, p_hi, hnorm_g, x, mod3, w_out, ln_g, ln_b, router_w, router_b):
    row = lambda width: pl.BlockSpec((OUT_TM, width), lambda i: (i, 0))
    const = lambda shape: pl.BlockSpec(shape, lambda i: tuple(0 for _ in shape))
    return pl.pallas_call(
        _outproj_kernel,
        grid=(T_ROWS // OUT_TM,),
        in_specs=[row(A_WIDTH), row(B_WIDTH), row(C_WIDTH), row(C_WIDTH),
                  pl.BlockSpec((OUT_TM, C_WIDTH), lambda i: (i, 4)),
                  const((1, HEAD_DIM)),
                  row(D_MODEL),
                  pl.BlockSpec((1, 6, D_MODEL), lambda i: (_mod_row(i, OUT_TM), 0, 0)),
                  const((D_MODEL, D_MODEL)),
                  const((1, D_MODEL)), const((1, D_MODEL)),
                  const((D_MODEL, N_EXPERTS)), const((1, N_EXPERTS))],
        out_specs=[row(D_MODEL), row(D_MODEL), row(D_MODEL // 2), row(LANE), row(LANE), row(LANE),
                   const((8, N_EXPERTS))],
        out_shape=[jax.ShapeDtypeStruct((T_ROWS, D_MODEL), F32),
                   jax.ShapeDtypeStruct((T_ROWS, D_MODEL), BF16),
                   jax.ShapeDtypeStruct((T_ROWS, D_MODEL // 2), jnp.uint32),
                   jax.ShapeDtypeStruct((T_ROWS, LANE), jnp.int32),
                   jax.ShapeDtypeStruct((T_ROWS, LANE), F32),
                   jax.ShapeDtypeStruct((T_ROWS, LANE), jnp.int32),
                   jax.ShapeDtypeStruct((8, N_EXPERTS), F32)],
        scratch_shapes=[pltpu.VMEM((8, N_EXPERTS), F32)],
        compiler_params=_cparams(("arbitrary",)),
        name="outproj_ln_router",
    )(a_x, b_x, o_f, o_b, p_hi, hnorm_g, x, mod3, w_out, ln_g, ln_b, router_w, router_b)


def _experts_kernel(ord_ref, elist_ref, cnt_ref, xs_ref, wg_hbm, wu_hbm, wd_hbm, ys_ref,
                    wg_buf, wu_buf, wd_buf, wgu_scr, wd_scr, sems, *, packed, layer):
    i = pl.program_id(0)
    live = i < cnt_ref[0]
    o = ord_ref[i]
    first_of_expert = jnp.logical_or(i == 0, o != ord_ref[jnp.maximum(i - 1, 0)])

    def weight_copies(which, slot):
        e = elist_ref[which]
        return (pltpu.make_async_copy(wg_hbm.at[layer, e], wg_buf.at[slot], sems.at[slot, 0]),
                pltpu.make_async_copy(wu_hbm.at[layer, e], wu_buf.at[slot], sems.at[slot, 1]),
                pltpu.make_async_copy(wd_hbm.at[layer, e], wd_buf.at[slot], sems.at[slot, 2]))

    @pl.when(i == 0)
    def _():
        for c in weight_copies(0, 0):
            c.start()

    @pl.when(jnp.logical_and(live, first_of_expert))
    def _():
        slot = o % 2
        for c in weight_copies(o, slot):
            c.wait()
        wgu_scr[:, :EXPERT_DIM] = wg_buf[slot].astype(BF16)
        wgu_scr[:, EXPERT_DIM:] = wu_buf[slot].astype(BF16)
        wd_scr[...] = wd_buf[slot].astype(BF16)

        @pl.when(o + 1 < cnt_ref[1])
        def _():
            for c in weight_copies(o + 1, 1 - slot):
                c.start()

    @pl.when(live)
    def _():
        if packed:
            low, high = _unpack_pairs(xs_ref[...])
            x = jnp.concatenate([low.astype(BF16), high.astype(BF16)], axis=1)
        else:
            x = xs_ref[...]
        h = jnp.dot(x, wgu_scr[...], preferred_element_type=F32)
        g = h[:, :EXPERT_DIM]
        u = h[:, EXPERT_DIM:]
        a = (g * _sigmoid(g) * u).astype(BF16)
        y = jnp.dot(a, wd_scr[...], preferred_element_type=F32)
        ys_ref[...] = _pack_pairs(y) if packed else y.astype(ys_ref.dtype)

    @pl.when(jnp.logical_not(live))
    def _():
        ys_ref[...] = jnp.zeros(ys_ref.shape, ys_ref.dtype)


def _experts(tile_ord, expert_list, counts2, xs, w_gate, w_up, w_down, layer):
    tiles = xs.shape[0] // MOE_TM
    last = lambda i, cnt: jnp.minimum(i, cnt[0] - 1)
    grid_spec = pltpu.PrefetchScalarGridSpec(
        num_scalar_prefetch=3,
        grid=(tiles,),
        in_specs=[pl.BlockSpec((MOE_TM, xs.shape[1]), lambda i, od, el, cnt: (last(i, cnt), 0)),
                  pl.BlockSpec(memory_space=pl.ANY),
                  pl.BlockSpec(memory_space=pl.ANY),
                  pl.BlockSpec(memory_space=pl.ANY)],
        out_specs=pl.BlockSpec((MOE_TM, xs.shape[1]), lambda i, od, el, cnt: (i, 0)),
        scratch_shapes=[pltpu.VMEM((2, D_MODEL, EXPERT_DIM), F32),
                        pltpu.VMEM((2, D_MODEL, EXPERT_DIM), F32),
                        pltpu.VMEM((2, EXPERT_DIM, D_MODEL), F32),
                        pltpu.VMEM((D_MODEL, 2 * EXPERT_DIM), BF16),
                        pltpu.VMEM((EXPERT_DIM, D_MODEL), BF16),
                        pltpu.SemaphoreType.DMA((2, 3))],
    )
    return pl.pallas_call(
        functools.partial(_experts_kernel, packed=xs.dtype == jnp.uint32, layer=layer),
        grid_spec=grid_spec,
        out_shape=jax.ShapeDtypeStruct(xs.shape, xs.dtype),
        compiler_params=_cparams(("arbitrary",), vmem=EXPERTS_VMEM_LIMIT),
        name="experts",
    )(tile_ord, expert_list, counts2, xs, w_gate, w_up, w_down)


def _combine_kernel(yg_ref, wt_ref, ysh_ref, x1_ref, mod_ref, g_ref, b_ref, o_ref):
    tm = COMBINE_TM
    low = jnp.zeros((tm, D_MODEL // 2), F32)
    high = jnp.zeros((tm, D_MODEL // 2), F32)
    for k in range(TOP_K):
        lo_k, hi_k = _unpack_pairs(yg_ref[k])
        low = low + wt_ref[:, k:k + 1] * lo_k
        high = high + wt_ref[:, k:k + 1] * hi_k
    y = ysh_ref[...].astype(F32) + jnp.concatenate([low, high], axis=1)
    g2 = mod_ref[0, 5:6, :]
    o_ref[...] = _layer_norm_rows(DEEPNORM_ALPHA * x1_ref[...] + g2 * y, g_ref[...], b_ref[...])


def _combine(yg, wt, ysh, x1, mod3, ln_g, ln_b):
    tm = COMBINE_TM
    row = lambda width: pl.BlockSpec((tm, width), lambda i: (i, 0))
    return pl.pallas_call(
        _combine_kernel,
        grid=(T_ROWS // tm,),
        in_specs=[pl.BlockSpec((TOP_K, tm, D_MODEL // 2), lambda i: (0, i, 0)),
                  row(LANE), row(D_MODEL), row(D_MODEL),
                  pl.BlockSpec((1, 6, D_MODEL), lambda i: (_mod_row(i, tm), 0, 0)),
                  pl.BlockSpec((1, D_MODEL), lambda i: (0, 0)),
                  pl.BlockSpec((1, D_MODEL), lambda i: (0, 0))],
        out_specs=row(D_MODEL),
        out_shape=jax.ShapeDtypeStruct((T_ROWS, D_MODEL), F32),
        compiler_params=_cparams(("parallel",)),
        name="moe_combine_ln",
    )(yg, wt, ysh, x1, mod3, ln_g, ln_b)


def _layout(idx, rank, counts):
    tiles_per = (counts + MOE_TM - 1) // MOE_TM
    tile_end = jnp.cumsum(tiles_per)
    start = (tile_end - tiles_per) * MOE_TM
    experts = jnp.arange(N_EXPERTS, dtype=jnp.int32)
    pos = rank + jnp.sum(jnp.where(idx[:, :, None] == experts, start, 0), axis=-1)
    tile = jnp.arange(MOE_TILES, dtype=jnp.int32)
    tile_expert = jnp.sum((tile_end[None, :] <= tile[:, None]).astype(jnp.int32), axis=-1)
    tile_expert = jnp.minimum(tile_expert, N_EXPERTS - 1)
    used = (counts > 0).astype(jnp.int32)
    used_rank = jnp.cumsum(used) - used
    n_used = jnp.sum(used)
    expert_list = jnp.sum(jnp.where((used_rank[None, :] == experts[:, None]) & (used[None, :] > 0),
                                    experts[None, :], 0), axis=-1)
    tile_ord = jnp.minimum(jnp.take(used_rank, tile_expert), jnp.maximum(n_used - 1, 0))
    counts2 = jnp.stack([tile_end[-1], n_used]).astype(jnp.int32)
    return pos, tile_ord.astype(jnp.int32), expert_list.astype(jnp.int32), counts2


def _dispatch_kernel(pos_ref, hxw_ref, init_hbm, xs_hbm, sem):
    def issue(t, carry):
        for k in range(TOP_K):
            pltpu.make_async_copy(hxw_ref.at[pl.ds(t, 1), :],
                                  xs_hbm.at[pl.ds(pos_ref[t * TOP_K + k], 1), :], sem).start(priority=k % 2)
        return carry

    lax.fori_loop(0, DISPATCH_TM, issue, 0)
    for _ in range(TOP_K):
        pltpu.make_async_copy(hxw_ref, xs_hbm.at[pl.ds(0, DISPATCH_TM), :], sem).wait()


def _dispatch_rows(pos, hx_words, init):
    return pl.pallas_call(
        _dispatch_kernel,
        grid=(T_ROWS // DISPATCH_TM,),
        in_specs=[pl.BlockSpec((DISPATCH_TM * TOP_K,), lambda i: (i,), memory_space=pltpu.SMEM),
                  pl.BlockSpec((DISPATCH_TM, D_MODEL // 2), lambda i: (i, 0)),
                  pl.BlockSpec(memory_space=pl.ANY)],
        out_specs=pl.BlockSpec(memory_space=pl.ANY),
        out_shape=jax.ShapeDtypeStruct((MOE_ROWS, D_MODEL // 2), jnp.uint32),
        scratch_shapes=[pltpu.SemaphoreType.DMA(())],
        input_output_aliases={2: 0},
        compiler_params=pltpu.CompilerParams(dimension_semantics=("arbitrary",), has_side_effects=True),
        name="moe_dispatch_rows",
    )(pos.reshape(-1), hx_words, init)


def _rope_tables():
    t = jnp.arange(SEQ)
    row = (t // GRID_W).astype(F32)
    col = (t % GRID_W).astype(F32)
    axis_dim = HEAD_DIM // 2
    inv = ROPE_THETA ** (-jnp.arange(0, axis_dim, 2, dtype=F32) / axis_dim)
    ar = row[:, None] * inv
    ac = col[:, None] * inv
    cos = jnp.concatenate([jnp.cos(ar), jnp.cos(ar), jnp.cos(ac), jnp.cos(ac)], axis=-1)
    sin = jnp.concatenate([jnp.sin(ar), jnp.sin(ar), jnp.sin(ac), jnp.sin(ac)], axis=-1)
    odd = ((jnp.arange(HEAD_DIM) // 32) % 2 == 1)[None, :]
    sa = jnp.where(odd, sin, 0.0)
    sb = jnp.where(odd, 0.0, -sin)
    lat = lambda z: jnp.tile(z, (BATCH, 1))
    cos_all = jnp.concatenate([lat(cos), jnp.ones((N_CTX, HEAD_DIM), F32)], axis=0)
    sa_all = jnp.concatenate([lat(sa), jnp.zeros((N_CTX, HEAD_DIM), F32)], axis=0)
    sb_all = jnp.concatenate([lat(sb), jnp.zeros((N_CTX, HEAD_DIM), F32)], axis=0)
    return cos_all, sa_all, sb_all


def kernel(x, c, ctx, c_ctx, w_mod, b_mod, w_in, w_out, gmlp_ln_g, gmlp_ln_b, gmlp_ws, gmlp_bs,
           diff_lam, diff_subln_g, hgrn_lb, hgrn_norm_g, ln1_g, ln1_b, ln2_g, ln2_b,
           router_w, router_b, exp_w_gate, exp_w_up, exp_w_down, sh_w_gate, sh_w_up, sh_w_down):
    assert x.shape == (BATCH, SEQ, D_MODEL) and ctx.shape == (BATCH, CTX_LEN, D_MODEL)
    xs_all = jnp.concatenate([x.reshape(N_LAT, D_MODEL), ctx.reshape(N_CTX, D_MODEL)], axis=0)

    cond = jnp.zeros((8, D_MODEL), F32).at[0:BATCH].set(c).at[BATCH].set(c_ctx)
    mod_all = _modulation(cond, w_mod, b_mod)[:, :BATCH + 1].reshape(DEPTH, BATCH + 1, 6, D_MODEL)

    sm = jax.nn.softmax(hgrn_lb.astype(F32), axis=0)
    lb_all = jnp.cumsum(sm, axis=0) - sm[0]
    tables = _rope_tables()
    xmat_np, masks_np = _scan_structure()
    xmat = jnp.asarray(xmat_np, BF16)
    masks = jnp.asarray(masks_np, F32)

    xs_grouped = jnp.zeros((MOE_ROWS, D_MODEL // 2), jnp.uint32)
    for l in range(DEPTH):
        mod3 = mod_all[l]
        w_in_l = w_in[l].astype(BF16)
        p_lo = _inproj(xs_all, mod3, w_in_l, 0, LOW_WIDTH, tables, BF16, True)
        p_hi = _inproj(xs_all, mod3, w_in_l, LOW_WIDTH, HI_WIDTH, tables, F32, False)

        a_x = _gmlp(p_lo, gmlp_ln_g[l][None, :], gmlp_ln_b[l][None, :],
                    gmlp_ws[l].astype(BF16), gmlp_bs[l].T)

        lam_init = 0.8 - 0.6 * math.exp(-0.3 * l)
        dl = diff_lam[l].astype(F32)
        lam = jnp.exp(jnp.sum(dl[0] * dl[1])) - jnp.exp(jnp.sum(dl[2] * dl[3])) + lam_init
        lam2 = jnp.stack([lam, jnp.asarray(1.0 - lam_init, F32)]).astype(F32)
        b_x = _diff_attention(p_lo, lam2, diff_subln_g[l][None, :])

        o_f, o_b = _hgrn_scan(p_hi, lb_all[l], xmat, masks)

        x1, hx, hx_words, idx, wt, rank, cnt = _outproj(
            a_x, b_x, o_f, o_b, p_hi, hgrn_norm_g[l][None, :], xs_all, mod3,
            w_out[l].astype(BF16), ln1_g[l][None, :], ln1_b[l][None, :],
            router_w[l].astype(BF16), router_b[l][None, :])

        pos, tile_ord, expert_list, counts2 = _layout(idx[:, :TOP_K], rank[:, :TOP_K], cnt[0].astype(jnp.int32))
        xs_grouped = _dispatch_rows(pos, hx_words, xs_grouped)
        ys = _experts(tile_ord, expert_list, counts2, xs_grouped, exp_w_gate, exp_w_up, exp_w_down, l)
        yg = jnp.take(ys, pos.T.reshape(-1), axis=0, mode="clip").reshape(TOP_K, T_ROWS, D_MODEL // 2)

        ysh = _experts(jnp.zeros((T_ROWS // MOE_TM,), jnp.int32), jnp.zeros((1,), jnp.int32),
                       jnp.array([T_ROWS // MOE_TM, 1], jnp.int32),
                       hx, sh_w_gate[:, None], sh_w_up[:, None], sh_w_down[:, None], l)

        xs_all = _combine(yg, wt, ysh, x1, mod3, ln2_g[l][None, :], ln2_b[l][None, :])

    return xs_all[:N_LAT].reshape(BATCH, SEQ, D_MODEL)
```

```python
import functools
import math

import numpy as np
import jax
import jax.numpy as jnp
from jax import lax
from jax.experimental import pallas as pl
from jax.experimental.pallas import tpu as pltpu

F32 = jnp.float32
BF16 = jnp.bfloat16

D_MODEL = 2048
BATCH = 2
SEQ = 4096
DEPTH = 4
CTX_LEN = 256
GRID_W = 64
HEAD_DIM = 128
A_WIDTH = 512
A_HEADS = 4
A_CHUNK = 128
B_WIDTH = 1024
B_HEADS = 4
B_QK_WIDTH = 1024
B_V_DIM = 256
ROPE_THETA = 10000.0
C_WIDTH = 512
C_HEADS = 4
IN_WIDTH = 6656
N_EXPERTS = 64
TOP_K = 8
EXPERT_DIM = 384
ROUTED_SCALE = 2.5
DEEPNORM_ALPHA = (2.0 * DEPTH) ** 0.25
LN_EPS = 1e-5
NORM_EPS = 1e-5

N_LAT = BATCH * SEQ
N_CTX = BATCH * CTX_LEN
T_ROWS = N_LAT + N_CTX
LOW_WIDTH = 4096
HI_WIDTH = IN_WIDTH - LOW_WIDTH
QK_SCALE = HEAD_DIM ** -0.5 * math.log2(math.e)

LANE = 128
PROJ_TM = 1088
MOD_TM = 512
PROJ_TN = 512
ATT_TQ = 512
ATT_TK = 1024
SCAN_C = 128
SCAN_LEVELS = 7
OUT_TM = 256
MOE_TM = 512
COMBINE_TM = 256
DISPATCH_TM = 256
MOE_ROWS = ((T_ROWS * TOP_K + N_EXPERTS * (MOE_TM - 1)) // MOE_TM) * MOE_TM
MOE_TILES = MOE_ROWS // MOE_TM
VMEM_LIMIT = 48 * 1024 * 1024
EXPERTS_VMEM_LIMIT = 56 * 1024 * 1024


def _cparams(sem, vmem=VMEM_LIMIT):
    return pltpu.CompilerParams(dimension_semantics=sem, vmem_limit_bytes=vmem)


def _sigmoid(x):
    return 1.0 / (1.0 + jnp.exp(-x))


def _gelu_tanh(x):
    return 0.5 * x * (1.0 + jnp.tanh(math.sqrt(2.0 / math.pi) * (x + 0.044715 * (x * x * x))))


def _pack_pairs(x):
    n = x.shape[1] // 2
    bits = pltpu.bitcast(x.astype(BF16).astype(F32), jnp.uint32)
    return bits[:, n:] | (bits[:, :n] >> 16)


def _unpack_pairs(words):
    low = pltpu.bitcast(words << 16, F32)
    high = pltpu.bitcast(words & jnp.uint32(0xFFFF0000), F32)
    return low, high


def _dot_nt(a, b):
    return lax.dot_general(a, b, (((1,), (1,)), ((), ())), preferred_element_type=F32)


def _dot_tn(a, b):
    return lax.dot_general(a, b, (((0,), (0,)), ((), ())), preferred_element_type=F32)


def _mod_kernel(c_ref, w_ref, b_ref, o_ref):
    c = c_ref[...]
    s = (c * _sigmoid(c)).astype(BF16)
    o_ref[0] = jnp.dot(s, w_ref[0].astype(BF16), preferred_element_type=F32) + b_ref[0]


def _modulation(cond, w_mod, b_mod):
    tn = 1024
    return pl.pallas_call(
        _mod_kernel,
        grid=(DEPTH, 6 * D_MODEL // tn),
        in_specs=[pl.BlockSpec((8, D_MODEL), lambda l, n: (0, 0)),
                  pl.BlockSpec((1, D_MODEL, tn), lambda l, n: (l, 0, n)),
                  pl.BlockSpec((1, 1, tn), lambda l, n: (l, 0, n))],
        out_specs=pl.BlockSpec((1, 8, tn), lambda l, n: (l, 0, n)),
        out_shape=jax.ShapeDtypeStruct((DEPTH, 8, 6 * D_MODEL), F32),
        compiler_params=_cparams(("parallel", "parallel")),
        name="modulation",
    )(cond, w_mod, b_mod.reshape(DEPTH, 1, 6 * D_MODEL))


def _mod_row(m, tm):
    return jnp.minimum(m // (SEQ // tm), BATCH)


def _modulate_kernel(x_ref, mod_ref, o_ref):
    o_ref[...] = (x_ref[...] * (1.0 + mod_ref[0, 1:2, :]) + mod_ref[0, 0:1, :]).astype(o_ref.dtype)


def _modulate(x, mod3):
    tm = MOD_TM
    return pl.pallas_call(
        _modulate_kernel,
        grid=(T_ROWS // tm,),
        in_specs=[pl.BlockSpec((tm, D_MODEL), lambda i: (i, 0)),
                  pl.BlockSpec((1, 6, D_MODEL), lambda i: (_mod_row(i, tm), 0, 0))],
        out_specs=pl.BlockSpec((tm, D_MODEL), lambda i: (i, 0)),
        out_shape=jax.ShapeDtypeStruct((T_ROWS, D_MODEL), BF16),
        compiler_params=_cparams(("parallel",)),
        name="modulate",
    )(x, mod3)


LOW_BLOCKS = LOW_WIDTH // PROJ_TN


def _inproj_kernel(h_ref, w_ref, cos_ref, sa_ref, sb_ref, lo_ref, hi_ref):
    n = pl.program_id(1)
    acc = jnp.dot(h_ref[...], w_ref[...], preferred_element_type=F32)
    is_qk = (n >= 2) & (n < 6)

    @pl.when(is_qk)
    def _():
        scale = jnp.where(n < 4, QK_SCALE, 1.0).astype(F32)
        cos = cos_ref[...]
        sa = sa_ref[...]
        sb = sb_ref[...]
        for g in range(PROJ_TN // LANE):
            blk = acc[:, g * LANE:(g + 1) * LANE]
            r = blk * cos + pltpu.roll(blk, 32, 1) * sa + pltpu.roll(blk, 96, 1) * sb
            lo_ref[:, g * LANE:(g + 1) * LANE] = (r * scale).astype(lo_ref.dtype)

    @pl.when(jnp.logical_and(jnp.logical_not(is_qk), n < LOW_BLOCKS))
    def _():
        lo_ref[...] = acc.astype(lo_ref.dtype)

    @pl.when(n >= LOW_BLOCKS)
    def _():
        hi_ref[...] = acc


def _inproj(h, w, tables):
    cos, sa, sb = tables
    tab_spec = pl.BlockSpec((PROJ_TM, LANE), lambda m, n: (m, 0))
    return pl.pallas_call(
        _inproj_kernel,
        grid=(T_ROWS // PROJ_TM, IN_WIDTH // PROJ_TN),
        in_specs=[pl.BlockSpec((PROJ_TM, D_MODEL), lambda m, n: (m, 0)),
                  pl.BlockSpec((D_MODEL, PROJ_TN), lambda m, n: (0, n)),
                  tab_spec, tab_spec, tab_spec],
        out_specs=[pl.BlockSpec((PROJ_TM, PROJ_TN), lambda m, n: (m, jnp.minimum(n, LOW_BLOCKS - 1))),
                   pl.BlockSpec((PROJ_TM, PROJ_TN), lambda m, n: (m, jnp.maximum(n - LOW_BLOCKS, 0)))],
        out_shape=[jax.ShapeDtypeStruct((T_ROWS, LOW_WIDTH), BF16),
                   jax.ShapeDtypeStruct((T_ROWS, HI_WIDTH), F32)],
        compiler_params=_cparams(("parallel", "arbitrary")),
        name="inproj",
    )(h, w, cos, sa, sb)


def _gmlp_kernel(u_ref, v_ref, g_ref, b_ref, ws_ref, bs_ref, o_ref):
    for h in range(A_HEADS):
        sl = slice(h * HEAD_DIM, (h + 1) * HEAD_DIM)
        v = _gelu_tanh(v_ref[:, sl].astype(F32))
        mu = jnp.mean(v, axis=-1, keepdims=True)
        var = jnp.mean(jnp.square(v - mu), axis=-1, keepdims=True)
        vn = (v - mu) * lax.rsqrt(var + LN_EPS) * g_ref[:, sl] + b_ref[:, sl]
        s = jnp.dot(ws_ref[h], vn.astype(BF16), preferred_element_type=F32) + bs_ref[:, h:h + 1]
        u = _gelu_tanh(u_ref[:, sl].astype(F32))
        o_ref[:, sl] = (u * s).astype(o_ref.dtype)


def _gmlp(p_lo, ln_g, ln_b, ws, bs_t):
    return pl.pallas_call(
        _gmlp_kernel,
        grid=(T_ROWS // A_CHUNK,),
        in_specs=[pl.BlockSpec((A_CHUNK, A_WIDTH), lambda i: (i, 0)),
                  pl.BlockSpec((A_CHUNK, A_WIDTH), lambda i: (i, 1)),
                  pl.BlockSpec((1, A_WIDTH), lambda i: (0, 0)),
                  pl.BlockSpec((1, A_WIDTH), lambda i: (0, 0)),
                  pl.BlockSpec((A_HEADS, A_CHUNK, A_CHUNK), lambda i: (0, 0, 0)),
                  pl.BlockSpec((A_CHUNK, A_HEADS), lambda i: (0, 0))],
        out_specs=pl.BlockSpec((A_CHUNK, A_WIDTH), lambda i: (i, 0)),
        out_shape=jax.ShapeDtypeStruct((T_ROWS, A_WIDTH), BF16),
        compiler_params=_cparams(("parallel",)),
        name="gmlp",
    )(p_lo, p_lo, ln_g, ln_b, ws, bs_t)


ATT_LAT_STEPS = N_LAT // ATT_TQ
ATT_STEPS_PER_BATCH = SEQ // ATT_TQ


def _attn_kernel(lam_ref, q_ref, kl_ref, vl_ref, kc_ref, vc_ref, g_ref, o_ref, m_scr, l_scr, acc_scr):
    step = pl.program_id(1)
    m_scr[...] = jnp.full(m_scr.shape, -1e30, F32)
    l_scr[...] = jnp.zeros(l_scr.shape, F32)
    acc_scr[...] = jnp.zeros(acc_scr.shape, F32)

    def process(k, v, visible=None):
        for mp in range(2):
            sl = slice(mp * HEAD_DIM, (mp + 1) * HEAD_DIM)
            s = _dot_nt(q_ref[:, sl], k[:, sl])
            if visible is not None:
                s = jnp.where(visible, s, -1e30)
            cols = [s[:, c * LANE:(c + 1) * LANE] for c in range(k.shape[0] // LANE)]
            m_part = cols[0]
            for col in cols[1:]:
                m_part = jnp.maximum(m_part, col)
            m_old = m_scr[mp]
            m_new = jnp.maximum(m_old, jnp.max(m_part, axis=-1, keepdims=True))
            alpha = jnp.exp2(m_old - m_new)
            l_part = alpha * l_scr[mp]
            ps = []
            for col in cols:
                p = jnp.exp2(col - m_new)
                l_part = l_part + p
                ps.append(p.astype(BF16))
            pv = jnp.dot(jnp.concatenate(ps, axis=1), v, preferred_element_type=F32)
            for half in range(B_V_DIM // LANE):
                hs = slice(half * LANE, (half + 1) * LANE)
                acc_scr[mp, :, hs] = alpha * acc_scr[mp, :, hs] + pv[:, hs]
            l_scr[mp] = l_part
            m_scr[mp] = m_new

    @pl.when(step < ATT_LAT_STEPS)
    def _():
        b = step // ATT_STEPS_PER_BATCH
        ctx_rows = pl.ds(pl.multiple_of(b * CTX_LEN, CTX_LEN), CTX_LEN)
        process(kc_ref[ctx_rows, :], vc_ref[ctx_rows, :])
        for j in range(SEQ // ATT_TK):
            rows = slice(j * ATT_TK, (j + 1) * ATT_TK)
            process(kl_ref[rows, :], vl_ref[rows, :])

    @pl.when(step == ATT_LAT_STEPS)
    def _():
        q_batch = lax.broadcasted_iota(jnp.int32, (ATT_TQ, N_CTX), 0) // CTX_LEN
        k_batch = lax.broadcasted_iota(jnp.int32, (ATT_TQ, N_CTX), 1) // CTX_LEN
        process(kc_ref[...], vc_ref[...], q_batch == k_batch)

    lam = lam_ref[0]
    post = lam_ref[1]
    l0 = jnp.sum(l_scr[0], axis=-1, keepdims=True)
    l1 = jnp.sum(l_scr[1], axis=-1, keepdims=True)
    o = acc_scr[0] / l0 - lam * (acc_scr[1] / l1)
    r = o * lax.rsqrt(jnp.mean(jnp.square(o), axis=-1, keepdims=True) + NORM_EPS)
    o_ref[...] = (r * g_ref[...] * post).astype(o_ref.dtype)


def _diff_attention(p_lo, lam2, subln_g):
    assert ATT_TQ == N_CTX
    lat_batch = lambda s: jnp.minimum(s // ATT_STEPS_PER_BATCH, BATCH - 1)
    return pl.pallas_call(
        _attn_kernel,
        grid=(B_HEADS, ATT_LAT_STEPS + 1),
        in_specs=[pl.BlockSpec(memory_space=pltpu.SMEM),
                  pl.BlockSpec((ATT_TQ, B_V_DIM), lambda h, s: (s, 4 + h)),
                  pl.BlockSpec((SEQ, B_V_DIM), lambda h, s: (lat_batch(s), 8 + h)),
                  pl.BlockSpec((SEQ, B_V_DIM), lambda h, s: (lat_batch(s), 12 + h)),
                  pl.BlockSpec((N_CTX, B_V_DIM), lambda h, s: (N_LAT // N_CTX, 8 + h)),
                  pl.BlockSpec((N_CTX, B_V_DIM), lambda h, s: (N_LAT // N_CTX, 12 + h)),
                  pl.BlockSpec((1, B_V_DIM), lambda h, s: (0, 0))],
        out_specs=pl.BlockSpec((ATT_TQ, B_V_DIM), lambda h, s: (s, h)),
        scratch_shapes=[pltpu.VMEM((2, ATT_TQ, LANE), F32),
                        pltpu.VMEM((2, ATT_TQ, LANE), F32),
                        pltpu.VMEM((2, ATT_TQ, B_V_DIM), F32)],
        out_shape=jax.ShapeDtypeStruct((T_ROWS, B_WIDTH), BF16),
        compiler_params=_cparams(("parallel", "arbitrary")),
        name="diff_attention",
    )(lam2, p_lo, p_lo, p_lo, p_lo, p_lo, subln_g)


def _scan_structure():
    c = SCAN_C
    x = np.zeros((2, (2 + SCAN_LEVELS) * c, c), np.float32)
    msk = np.zeros((2, SCAN_LEVELS + 1, c, c), np.float32)
    r = np.arange(c)
    j = np.arange(c)[None, :]
    xf = np.zeros(((2 + SCAN_LEVELS) * c, c), np.float32)
    mf = np.zeros((SCAN_LEVELS + 1, c, c), np.float32)
    xf[0:c] = (j <= r[:, None])
    xf[c:2 * c] = (j > r[:, None])
    for lev in range(SCAN_LEVELS):
        half = (c // 2) >> lev
        start = (r // (2 * half)) * (2 * half)
        mid = start + half - 1
        later = r > mid
        rows = np.where(later[:, None], (j > mid[:, None]) & (j <= r[:, None]),
                        (j > r[:, None]) & (j <= mid[:, None]))
        xf[(2 + lev) * c:(3 + lev) * c] = rows
        same = start[:, None] == start[None, :]
        mf[lev] = same & later[:, None] & (~later)[None, :]
    mf[SCAN_LEVELS] = np.eye(c)
    x[0] = xf
    msk[0] = mf
    x[1] = xf.reshape(2 + SCAN_LEVELS, c, c)[:, ::-1, ::-1].reshape(-1, c)
    msk[1] = mf[:, ::-1, ::-1]
    return x, msk


def _hgrn_kernel(qf_ref, ff_ref, if_ref, qb_ref, fb_ref, ib_ref, lb_ref, x_ref, msk_ref,
                 of_ref, ob_ref, st_scr):
    j = pl.program_id(1)

    @pl.when(j == 0)
    def _():
        st_scr[...] = jnp.zeros(st_scr.shape, F32)

    c = SCAN_C
    dirs = ((qf_ref, ff_ref, if_ref, of_ref, c - 1), (qb_ref, fb_ref, ib_ref, ob_ref, 0))
    for d, (q_ref, f_ref, i_ref, o_ref, end_row) in enumerate(dirs):
        for h in range(C_HEADS):
            sl = slice(h * HEAD_DIM, (h + 1) * HEAD_DIM)
            qraw = q_ref[:, sl]
            q = qraw * _sigmoid(qraw)
            lb = lb_ref[d:d + 1, sl]
            f = lb + (1.0 - lb) * _sigmoid(f_ref[:, sl])
            logf = jnp.log(f)
            k = 1.0 - f
            v = i_ref[:, sl].astype(BF16)
            hi = logf.astype(BF16)
            lo = (logf - hi.astype(F32)).astype(BF16)
            e2 = jnp.dot(x_ref[d], jnp.concatenate([hi, lo], axis=1), preferred_element_type=F32)
            w = jnp.exp(e2[:, 0:c] + e2[:, c:2 * c])
            a = msk_ref[d, SCAN_LEVELS] * _dot_nt(q.astype(BF16), k.astype(BF16))
            for lev in range(SCAN_LEVELS):
                wl = w[(2 + lev) * c:(3 + lev) * c]
                a = a + msk_ref[d, lev] * _dot_nt((q * wl).astype(BF16), (k * wl).astype(BF16))
            st = st_scr[d, h]
            o = jnp.dot(a.astype(BF16), v, preferred_element_type=F32)
            o = o + _dot_nt((q * w[0:c]).astype(BF16), st.astype(BF16))
            o_ref[:, sl] = o
            g_end = w[end_row:end_row + 1, :]
            st_scr[d, h] = st * g_end + _dot_tn(v, (k * w[c:2 * c]).astype(BF16))


def _hgrn_scan(p_hi, lb2, xmat, masks):
    lat_chunks = SEQ // SCAN_C
    ctx_chunks = CTX_LEN // SCAN_C
    steps = ctx_chunks + lat_chunks

    def fwd_rows(b, j):
        return jnp.where(j < ctx_chunks, N_LAT // SCAN_C + b * ctx_chunks + j,
                         b * lat_chunks + j - ctx_chunks)

    def bwd_rows(b, j):
        return jnp.where(j < ctx_chunks, N_LAT // SCAN_C + b * ctx_chunks + (ctx_chunks - 1 - j),
                         b * lat_chunks + (steps - 1 - j))

    def spec(rows, col):
        return pl.BlockSpec((SCAN_C, C_WIDTH), lambda b, j: (rows(b, j), col))

    nx = (2 + SCAN_LEVELS) * SCAN_C
    return pl.pallas_call(
        _hgrn_kernel,
        grid=(BATCH, steps),
        in_specs=[spec(fwd_rows, 0), spec(fwd_rows, 1), spec(fwd_rows, 3),
                  spec(bwd_rows, 0), spec(bwd_rows, 2), spec(bwd_rows, 3),
                  pl.BlockSpec((2, C_WIDTH), lambda b, j: (0, 0)),
                  pl.BlockSpec((2, nx, SCAN_C), lambda b, j: (0, 0, 0)),
                  pl.BlockSpec((2, SCAN_LEVELS + 1, SCAN_C, SCAN_C), lambda b, j: (0, 0, 0, 0))],
        out_specs=[spec(fwd_rows, 0), spec(bwd_rows, 0)],
        out_shape=[jax.ShapeDtypeStruct((T_ROWS, C_WIDTH), F32)] * 2,
        scratch_shapes=[pltpu.VMEM((2, C_HEADS, HEAD_DIM, HEAD_DIM), F32)],
        compiler_params=_cparams(("parallel", "arbitrary")),
        name="hgrn2_scan",
    )(p_hi, p_hi, p_hi, p_hi, p_hi, p_hi, lb2, xmat, masks)


def _layer_norm_rows(z, g, b):
    mu = jnp.mean(z, axis=-1, keepdims=True)
    var = jnp.mean(jnp.square(z - mu), axis=-1, keepdims=True)
    return (z - mu) * lax.rsqrt(var + LN_EPS) * g + b


def _outproj_kernel(a_ref, b_ref, of_ref, ob_ref, cg_ref, hg_ref, x_ref, mod_ref, w_ref,
                    l1g_ref, l1b_ref, rw_ref, rb_ref,
                    x1_ref, hx_ref, hxp_ref, idx_ref, wt_ref, rank_ref, cnt_ref, cnt_scr):
    parts = []
    for h in range(C_HEADS):
        sl = slice(h * HEAD_DIM, (h + 1) * HEAD_DIM)
        o = of_ref[:, sl] + ob_ref[:, sl]
        on = o * lax.rsqrt(jnp.mean(jnp.square(o), axis=-1, keepdims=True) + NORM_EPS) * hg_ref[...]
        g = cg_ref[:, sl]
        parts.append((on * (g * _sigmoid(g))).astype(BF16))
    c_x = jnp.concatenate(parts, axis=1)
    mix = jnp.dot(a_ref[...], w_ref[0:A_WIDTH, :], preferred_element_type=F32)
    mix = mix + jnp.dot(b_ref[...], w_ref[A_WIDTH:A_WIDTH + B_WIDTH, :], preferred_element_type=F32)
    mix = mix + jnp.dot(c_x, w_ref[A_WIDTH + B_WIDTH:, :], preferred_element_type=F32)
    g1 = mod_ref[0, 2:3, :]
    sh2 = mod_ref[0, 3:4, :]
    sc2 = mod_ref[0, 4:5, :]
    x1 = _layer_norm_rows(DEEPNORM_ALPHA * x_ref[...] + g1 * mix, l1g_ref[...], l1b_ref[...])
    x1_ref[...] = x1
    hx = x1 * (1.0 + sc2) + sh2
    hx_bf = hx.astype(BF16)
    hx_ref[...] = hx_bf
    hxp_ref[...] = _pack_pairs(hx)

    logits = jnp.dot(hx.astype(BF16), rw_ref[...], preferred_element_type=F32)
    scores = _sigmoid(logits)
    sel = scores + rb_ref[...]
    tm = sel.shape[0]
    lane = lax.broadcasted_iota(jnp.int32, sel.shape, 1).astype(F32)
    slot = lax.broadcasted_iota(jnp.int32, (tm, LANE), 1)
    idx_acc = jnp.zeros((tm, LANE), F32)
    wt_acc = jnp.zeros((tm, LANE), F32)
    chosen = jnp.zeros(sel.shape, F32)
    hits = []
    for k in range(TOP_K):
        mx = jnp.max(sel, axis=-1, keepdims=True)
        idx = jnp.min(jnp.where(sel == mx, lane, float(N_EXPERTS)), axis=-1, keepdims=True)
        hit = lane == idx
        hits.append(hit)
        chosen = jnp.where(hit, 1.0, chosen)
        w_k = jnp.sum(jnp.where(hit, scores, 0.0), axis=-1, keepdims=True)
        idx_acc = jnp.where(slot == k, idx, idx_acc)
        wt_acc = jnp.where(slot == k, w_k, wt_acc)
        sel = jnp.where(hit, -jnp.inf, sel)
    idx_ref[...] = idx_acc.astype(jnp.int32)
    wt_ref[...] = wt_acc / jnp.sum(wt_acc, axis=-1, keepdims=True) * ROUTED_SCALE

    @pl.when(pl.program_id(0) == 0)
    def _():
        cnt_scr[...] = jnp.zeros(cnt_scr.shape, F32)

    below = (lax.broadcasted_iota(jnp.int32, (tm, tm), 0) > lax.broadcasted_iota(jnp.int32, (tm, tm), 1))
    before = jnp.dot(below.astype(BF16), chosen.astype(BF16), preferred_element_type=F32) + cnt_scr[0:1, :]
    rank_acc = jnp.zeros((tm, LANE), F32)
    for k in range(TOP_K):
        r_k = jnp.sum(jnp.where(hits[k], before, 0.0), axis=-1, keepdims=True)
        rank_acc = jnp.where(slot == k, r_k, rank_acc)
    rank_ref[...] = rank_acc.astype(jnp.int32)
    cnt = cnt_scr[...] + jnp.sum(chosen, axis=0, keepdims=True)
    cnt_scr[...] = cnt
    cnt_ref[...] = cnt


def _outproj(a_x, b_x, o_f, o_b, p_hi, hnorm_g, x, mod3, w_out, ln_g, ln_b, router_w, router_b):
    row = lambda width: pl.BlockSpec((OUT_TM, width), lambda i: (i, 0))
    const = lambda shape: pl.BlockSpec(shape, lambda i: tuple(0 for _ in shape))
    return pl.pallas_call(
        _outproj_kernel,
        grid=(T_ROWS // OUT_TM,),
        in_specs=[row(A_WIDTH), row(B_WIDTH), row(C_WIDTH), row(C_WIDTH),
                  pl.BlockSpec((OUT_TM, C_WIDTH), lambda i: (i, 4)),
                  const((1, HEAD_DIM)),
                  row(D_MODEL),
                  pl.BlockSpec((1, 6, D_MODEL), lambda i: (_mod_row(i, OUT_TM), 0, 0)),
                  const((D_MODEL, D_MODEL)),
                  const((1, D_MODEL)), const((1, D_MODEL)),
                  const((D_MODEL, N_EXPERTS)), const((1, N_EXPERTS))],
        out_specs=[row(D_MODEL), row(D_MODEL), row(D_MODEL // 2), row(LANE), row(LANE), row(LANE),
                   const((8, N_EXPERTS))],
        out_shape=[jax.ShapeDtypeStruct((T_ROWS, D_MODEL), F32),
                   jax.ShapeDtypeStruct((T_ROWS, D_MODEL), BF16),
                   jax.ShapeDtypeStruct((T_ROWS, D_MODEL // 2), jnp.uint32),
                   jax.ShapeDtypeStruct((T_ROWS, LANE), jnp.int32),
                   jax.ShapeDtypeStruct((T_ROWS, LANE), F32),
                   jax.ShapeDtypeStruct((T_ROWS, LANE), jnp.int32),
                   jax.ShapeDtypeStruct((8, N_EXPERTS), F32)],
        scratch_shapes=[pltpu.VMEM((8, N_EXPERTS), F32)],
        compiler_params=_cparams(("arbitrary",)),
        name="outproj_ln_router",
    )(a_x, b_x, o_f, o_b, p_hi, hnorm_g, x, mod3, w_out, ln_g, ln_b, router_w, router_b)


def _experts_kernel(ord_ref, elist_ref, cnt_ref, xs_ref, wg_hbm, wu_hbm, wd_hbm, ys_ref,
                    wg_buf, wu_buf, wd_buf, wgu_scr, wd_scr, sems, *, packed, layer):
    i = pl.program_id(0)
    live = i < cnt_ref[0]
    o = ord_ref[i]
    first_of_expert = jnp.logical_or(i == 0, o != ord_ref[jnp.maximum(i - 1, 0)])

    def weight_copies(which, slot):
        e = elist_ref[which]
        return (pltpu.make_async_copy(wg_hbm.at[layer, e], wg_buf.at[slot], sems.at[slot, 0]),
                pltpu.make_async_copy(wu_hbm.at[layer, e], wu_buf.at[slot], sems.at[slot, 1]),
                pltpu.make_async_copy(wd_hbm.at[layer, e], wd_buf.at[slot], sems.at[slot, 2]))

    @pl.when(i == 0)
    def _():
        for c in weight_copies(0, 0):
            c.start()

    @pl.when(jnp.logical_and(live, first_of_expert))
    def _():
        slot = o % 2
        for c in weight_copies(o, slot):
            c.wait()
        wgu_scr[:, :EXPERT_DIM] = wg_buf[slot].astype(BF16)
        wgu_scr[:, EXPERT_DIM:] = wu_buf[slot].astype(BF16)
        wd_scr[...] = wd_buf[slot].astype(BF16)

        @pl.when(o + 1 < cnt_ref[1])
        def _():
            for c in weight_copies(o + 1, 1 - slot):
                c.start()

    @pl.when(live)
    def _():
        if packed:
            low, high = _unpack_pairs(xs_ref[...])
            x = jnp.concatenate([low.astype(BF16), high.astype(BF16)], axis=1)
        else:
            x = xs_ref[...]
        h = jnp.dot(x, wgu_scr[...], preferred_element_type=F32)
        g = h[:, :EXPERT_DIM]
        u = h[:, EXPERT_DIM:]
        a = (g * _sigmoid(g) * u).astype(BF16)
        y = jnp.dot(a, wd_scr[...], preferred_element_type=F32)
        ys_ref[...] = _pack_pairs(y) if packed else y.astype(ys_ref.dtype)

    @pl.when(jnp.logical_not(live))
    def _():
        ys_ref[...] = jnp.zeros(ys_ref.shape, ys_ref.dtype)


def _experts(tile_ord, expert_list, counts2, xs, w_gate, w_up, w_down, layer):
    tiles = xs.shape[0] // MOE_TM
    last = lambda i, cnt: jnp.minimum(i, cnt[0] - 1)
    grid_spec = pltpu.PrefetchScalarGridSpec(
        num_scalar_prefetch=3,
        grid=(tiles,),
        in_specs=[pl.BlockSpec((MOE_TM, xs.shape[1]), lambda i, od, el, cnt: (last(i, cnt), 0)),
                  pl.BlockSpec(memory_space=pl.ANY),
                  pl.BlockSpec(memory_space=pl.ANY),
                  pl.BlockSpec(memory_space=pl.ANY)],
        out_specs=pl.BlockSpec((MOE_TM, xs.shape[1]), lambda i, od, el, cnt: (i, 0)),
        scratch_shapes=[pltpu.VMEM((2, D_MODEL, EXPERT_DIM), F32),
                        pltpu.VMEM((2, D_MODEL, EXPERT_DIM), F32),
                        pltpu.VMEM((2, EXPERT_DIM, D_MODEL), F32),
                        pltpu.VMEM((D_MODEL, 2 * EXPERT_DIM), BF16),
                        pltpu.VMEM((EXPERT_DIM, D_MODEL), BF16),
                        pltpu.SemaphoreType.DMA((2, 3))],
    )
    return pl.pallas_call(
        functools.partial(_experts_kernel, packed=xs.dtype == jnp.uint32, layer=layer),
        grid_spec=grid_spec,
        out_shape=jax.ShapeDtypeStruct(xs.shape, xs.dtype),
        compiler_params=_cparams(("arbitrary",), vmem=EXPERTS_VMEM_LIMIT),
        name="experts",
    )(tile_ord, expert_list, counts2, xs, w_gate, w_up, w_down)


def _combine_kernel(yg_ref, wt_ref, ysh_ref, x1_ref, mod_ref, g_ref, b_ref, nmod_ref, o_ref, h_ref):
    tm = COMBINE_TM
    low = jnp.zeros((tm, D_MODEL // 2), F32)
    high = jnp.zeros((tm, D_MODEL // 2), F32)
    for k in range(TOP_K):
        lo_k, hi_k = _unpack_pairs(yg_ref[k])
        low = low + wt_ref[:, k:k + 1] * lo_k
        high = high + wt_ref[:, k:k + 1] * hi_k
    y = ysh_ref[...].astype(F32) + jnp.concatenate([low, high], axis=1)
    g2 = mod_ref[0, 5:6, :]
    x2 = _layer_norm_rows(DEEPNORM_ALPHA * x1_ref[...] + g2 * y, g_ref[...], b_ref[...])
    o_ref[...] = x2
    h_ref[...] = (x2 * (1.0 + nmod_ref[0, 1:2, :]) + nmod_ref[0, 0:1, :]).astype(h_ref.dtype)


def _combine(yg, wt, ysh, x1, mod3, ln_g, ln_b, next_mod3):
    tm = COMBINE_TM
    row = lambda width: pl.BlockSpec((tm, width), lambda i: (i, 0))
    mod_spec = pl.BlockSpec((1, 6, D_MODEL), lambda i: (_mod_row(i, tm), 0, 0))
    return pl.pallas_call(
        _combine_kernel,
        grid=(T_ROWS // tm,),
        in_specs=[pl.BlockSpec((TOP_K, tm, D_MODEL // 2), lambda i: (0, i, 0)),
                  row(LANE), row(D_MODEL), row(D_MODEL),
                  mod_spec,
                  pl.BlockSpec((1, D_MODEL), lambda i: (0, 0)),
                  pl.BlockSpec((1, D_MODEL), lambda i: (0, 0)),
                  mod_spec],
        out_specs=[row(D_MODEL), row(D_MODEL)],
        out_shape=[jax.ShapeDtypeStruct((T_ROWS, D_MODEL), F32),
                   jax.ShapeDtypeStruct((T_ROWS, D_MODEL), BF16)],
        compiler_params=_cparams(("parallel",)),
        name="moe_combine_ln",
    )(yg, wt, ysh, x1, mod3, ln_g, ln_b, next_mod3)


def _layout(idx, rank, counts):
    tiles_per = (counts + MOE_TM - 1) // MOE_TM
    tile_end = jnp.cumsum(tiles_per)
    start = (tile_end - tiles_per) * MOE_TM
    experts = jnp.arange(N_EXPERTS, dtype=jnp.int32)
    pos = rank + jnp.sum(jnp.where(idx[:, :, None] == experts, start, 0), axis=-1)
    tile = jnp.arange(MOE_TILES, dtype=jnp.int32)
    tile_expert = jnp.sum((tile_end[None, :] <= tile[:, None]).astype(jnp.int32), axis=-1)
    tile_expert = jnp.minimum(tile_expert, N_EXPERTS - 1)
    used = (counts > 0).astype(jnp.int32)
    used_rank = jnp.cumsum(used) - used
    n_used = jnp.sum(used)
    expert_list = jnp.sum(jnp.where((used_rank[None, :] == experts[:, None]) & (used[None, :] > 0),
                                    experts[None, :], 0), axis=-1)
    tile_ord = jnp.minimum(jnp.take(used_rank, tile_expert), jnp.maximum(n_used - 1, 0))
    counts2 = jnp.stack([tile_end[-1], n_used]).astype(jnp.int32)
    return pos, tile_ord.astype(jnp.int32), expert_list.astype(jnp.int32), counts2


def _dispatch_kernel(pos_ref, hxw_ref, init_hbm, xs_hbm, sem):
    def issue(t, carry):
        for k in range(TOP_K):
            pltpu.make_async_copy(hxw_ref.at[pl.ds(t, 1), :],
                                  xs_hbm.at[pl.ds(pos_ref[t * TOP_K + k], 1), :], sem).start(priority=k % 2)
        return carry

    lax.fori_loop(0, DISPATCH_TM, issue, 0)
    for _ in range(TOP_K):
        pltpu.make_async_copy(hxw_ref, xs_hbm.at[pl.ds(0, DISPATCH_TM), :], sem).wait()


def _dispatch_rows(pos, hx_words, init):
    return pl.pallas_call(
        _dispatch_kernel,
        grid=(T_ROWS // DISPATCH_TM,),
        in_specs=[pl.BlockSpec((DISPATCH_TM * TOP_K,), lambda i: (i,), memory_space=pltpu.SMEM),
                  pl.BlockSpec((DISPATCH_TM, D_MODEL // 2), lambda i: (i, 0)),
                  pl.BlockSpec(memory_space=pl.ANY)],
        out_specs=pl.BlockSpec(memory_space=pl.ANY),
        out_shape=jax.ShapeDtypeStruct((MOE_ROWS, D_MODEL // 2), jnp.uint32),
        scratch_shapes=[pltpu.SemaphoreType.DMA(())],
        input_output_aliases={2: 0},
        compiler_params=pltpu.CompilerParams(dimension_semantics=("arbitrary",), has_side_effects=True),
        name="moe_dispatch_rows",
    )(pos.reshape(-1), hx_words, init)


def _rope_tables():
    t = jnp.arange(SEQ)
    row = (t // GRID_W).astype(F32)
    col = (t % GRID_W).astype(F32)
    axis_dim = HEAD_DIM // 2
    inv = ROPE_THETA ** (-jnp.arange(0, axis_dim, 2, dtype=F32) / axis_dim)
    ar = row[:, None] * inv
    ac = col[:, None] * inv
    cos = jnp.concatenate([jnp.cos(ar), jnp.cos(ar), jnp.cos(ac), jnp.cos(ac)], axis=-1)
    sin = jnp.concatenate([jnp.sin(ar), jnp.sin(ar), jnp.sin(ac), jnp.sin(ac)], axis=-1)
    odd = ((jnp.arange(HEAD_DIM) // 32) % 2 == 1)[None, :]
    sa = jnp.where(odd, sin, 0.0)
    sb = jnp.where(odd, 0.0, -sin)
    lat = lambda z: jnp.tile(z, (BATCH, 1))
    cos_all = jnp.concatenate([lat(cos), jnp.ones((N_CTX, HEAD_DIM), F32)], axis=0)
    sa_all = jnp.concatenate([lat(sa), jnp.zeros((N_CTX, HEAD_DIM), F32)], axis=0)
    sb_all = jnp.concatenate([lat(sb), jnp.zeros((N_CTX, HEAD_DIM), F32)], axis=0)
    return cos_all, sa_all, sb_all


def kernel(x, c, ctx, c_ctx, w_mod, b_mod, w_in, w_out, gmlp_ln_g, gmlp_ln_b, gmlp_ws, gmlp_bs,
           diff_lam, diff_subln_g, hgrn_lb, hgrn_norm_g, ln1_g, ln1_b, ln2_g, ln2_b,
           router_w, router_b, exp_w_gate, exp_w_up, exp_w_down, sh_w_gate, sh_w_up, sh_w_down):
    assert x.shape == (BATCH, SEQ, D_MODEL) and ctx.shape == (BATCH, CTX_LEN, D_MODEL)
    xs_all = jnp.concatenate([x.reshape(N_LAT, D_MODEL), ctx.reshape(N_CTX, D_MODEL)], axis=0)

    cond = jnp.zeros((8, D_MODEL), F32).at[0:BATCH].set(c).at[BATCH].set(c_ctx)
    mod_all = _modulation(cond, w_mod, b_mod)[:, :BATCH + 1].reshape(DEPTH, BATCH + 1, 6, D_MODEL)

    sm = jax.nn.softmax(hgrn_lb.astype(F32), axis=0)
    lb_all = jnp.cumsum(sm, axis=0) - sm[0]
    tables = _rope_tables()
    xmat_np, masks_np = _scan_structure()
    xmat = jnp.asarray(xmat_np, BF16)
    masks = jnp.asarray(masks_np, F32)

    xs_grouped = jnp.zeros((MOE_ROWS, D_MODEL // 2), jnp.uint32)
    h_in = _modulate(xs_all, mod_all[0])
    for l in range(DEPTH):
        mod3 = mod_all[l]
        w_in_l = w_in[l].astype(BF16)
        p_lo, p_hi = _inproj(h_in, w_in_l, tables)

        a_x = _gmlp(p_lo, gmlp_ln_g[l][None, :], gmlp_ln_b[l][None, :],
                    gmlp_ws[l].astype(BF16), gmlp_bs[l].T)

        lam_init = 0.8 - 0.6 * math.exp(-0.3 * l)
        dl = diff_lam[l].astype(F32)
        lam = jnp.exp(jnp.sum(dl[0] * dl[1])) - jnp.exp(jnp.sum(dl[2] * dl[3])) + lam_init
        lam2 = jnp.stack([lam, jnp.asarray(1.0 - lam_init, F32)]).astype(F32)
        b_x = _diff_attention(p_lo, lam2, diff_subln_g[l][None, :])

        o_f, o_b = _hgrn_scan(p_hi, lb_all[l], xmat, masks)

        x1, hx, hx_words, idx, wt, rank, cnt = _outproj(
            a_x, b_x, o_f, o_b, p_hi, hgrn_norm_g[l][None, :], xs_all, mod3,
            w_out[l].astype(BF16), ln1_g[l][None, :], ln1_b[l][None, :],
            router_w[l].astype(BF16), router_b[l][None, :])

        pos, tile_ord, expert_list, counts2 = _layout(idx[:, :TOP_K], rank[:, :TOP_K], cnt[0].astype(jnp.int32))
        xs_grouped = _dispatch_rows(pos, hx_words, xs_grouped)
        ys = _experts(tile_ord, expert_list, counts2, xs_grouped, exp_w_gate, exp_w_up, exp_w_down, l)
        yg = jnp.take(ys, pos.T.reshape(-1), axis=0, mode="clip").reshape(TOP_K, T_ROWS, D_MODEL // 2)

        ysh = _experts(jnp.zeros((T_ROWS // MOE_TM,), jnp.int32), jnp.zeros((1,), jnp.int32),
                       jnp.array([T_ROWS // MOE_TM, 1], jnp.int32),
                       hx, sh_w_gate[:, None], sh_w_up[:, None], sh_w_down[:, None], l)

        xs_all, h_in = _combine(yg, wt, ysh, x1, mod3, ln2_g[l][None, :], ln2_b[l][None, :],
                                mod_all[min(l + 1, DEPTH - 1)])

    return xs_all[:N_LAT].reshape(BATCH, SEQ, D_MODEL)
```

```python
import functools
import math

import numpy as np
import jax
import jax.numpy as jnp
from jax import lax
from jax.experimental import pallas as pl
from jax.experimental.pallas import tpu as pltpu

F32 = jnp.float32
BF16 = jnp.bfloat16

D_MODEL = 2048
BATCH = 2
SEQ = 4096
DEPTH = 4
CTX_LEN = 256
GRID_W = 64
HEAD_DIM = 128
A_WIDTH = 512
A_HEADS = 4
A_CHUNK = 128
B_WIDTH = 1024
B_HEADS = 4
B_QK_WIDTH = 1024
B_V_DIM = 256
ROPE_THETA = 10000.0
C_WIDTH = 512
C_HEADS = 4
IN_WIDTH = 6656
N_EXPERTS = 64
TOP_K = 8
EXPERT_DIM = 384
ROUTED_SCALE = 2.5
DEEPNORM_ALPHA = (2.0 * DEPTH) ** 0.25
LN_EPS = 1e-5
NORM_EPS = 1e-5

N_LAT = BATCH * SEQ
N_CTX = BATCH * CTX_LEN
T_ROWS = N_LAT + N_CTX
LOW_WIDTH = 4096
HI_WIDTH = IN_WIDTH - LOW_WIDTH
QK_SCALE = HEAD_DIM ** -0.5 * math.log2(math.e)

LANE = 128
PROJ_TM = 1088
MOD_TM = 512
PROJ_TN = 512
ATT_TQ = 512
ATT_TK = 1024
SCAN_C = 128
SCAN_LEVELS = 7
OUT_TM = 256
MOE_TM = 512
COMBINE_TM = 128
DISPATCH_TM = 256
MOE_ROWS = ((T_ROWS * TOP_K + N_EXPERTS * (MOE_TM - 1)) // MOE_TM) * MOE_TM
MOE_TILES = MOE_ROWS // MOE_TM
VMEM_LIMIT = 48 * 1024 * 1024
EXPERTS_VMEM_LIMIT = 56 * 1024 * 1024


def _cparams(sem, vmem=VMEM_LIMIT):
    return pltpu.CompilerParams(dimension_semantics=sem, vmem_limit_bytes=vmem)


def _sigmoid(x):
    return 1.0 / (1.0 + jnp.exp(-x))


def _gelu_tanh(x):
    return 0.5 * x * (1.0 + jnp.tanh(math.sqrt(2.0 / math.pi) * (x + 0.044715 * (x * x * x))))


def _pack_pairs(x):
    n = x.shape[1] // 2
    bits = pltpu.bitcast(x.astype(BF16).astype(F32), jnp.uint32)
    return bits[:, n:] | (bits[:, :n] >> 16)


def _unpack_pairs(words):
    low = pltpu.bitcast(words << 16, F32)
    high = pltpu.bitcast(words & jnp.uint32(0xFFFF0000), F32)
    return low, high


def _dot_nt(a, b):
    return lax.dot_general(a, b, (((1,), (1,)), ((), ())), preferred_element_type=F32)


def _dot_tn(a, b):
    return lax.dot_general(a, b, (((0,), (0,)), ((), ())), preferred_element_type=F32)


def _mod_kernel(c_ref, w_ref, b_ref, o_ref):
    c = c_ref[...]
    s = (c * _sigmoid(c)).astype(BF16)
    o_ref[0] = jnp.dot(s, w_ref[0].astype(BF16), preferred_element_type=F32) + b_ref[0]


def _modulation(cond, w_mod, b_mod):
    tn = 1024
    return pl.pallas_call(
        _mod_kernel,
        grid=(DEPTH, 6 * D_MODEL // tn),
        in_specs=[pl.BlockSpec((8, D_MODEL), lambda l, n: (0, 0)),
                  pl.BlockSpec((1, D_MODEL, tn), lambda l, n: (l, 0, n)),
                  pl.BlockSpec((1, 1, tn), lambda l, n: (l, 0, n))],
        out_specs=pl.BlockSpec((1, 8, tn), lambda l, n: (l, 0, n)),
        out_shape=jax.ShapeDtypeStruct((DEPTH, 8, 6 * D_MODEL), F32),
        compiler_params=_cparams(("parallel", "parallel")),
        name="modulation",
    )(cond, w_mod, b_mod.reshape(DEPTH, 1, 6 * D_MODEL))


def _mod_row(m, tm):
    return jnp.minimum(m // (SEQ // tm), BATCH)


def _modulate_kernel(x_ref, mod_ref, o_ref):
    o_ref[...] = (x_ref[...] * (1.0 + mod_ref[0, 1:2, :]) + mod_ref[0, 0:1, :]).astype(o_ref.dtype)


def _modulate(x, mod3):
    tm = MOD_TM
    return pl.pallas_call(
        _modulate_kernel,
        grid=(T_ROWS // tm,),
        in_specs=[pl.BlockSpec((tm, D_MODEL), lambda i: (i, 0)),
                  pl.BlockSpec((1, 6, D_MODEL), lambda i: (_mod_row(i, tm), 0, 0))],
        out_specs=pl.BlockSpec((tm, D_MODEL), lambda i: (i, 0)),
        out_shape=jax.ShapeDtypeStruct((T_ROWS, D_MODEL), BF16),
        compiler_params=_cparams(("parallel",)),
        name="modulate",
    )(x, mod3)


LOW_BLOCKS = LOW_WIDTH // PROJ_TN


def _inproj_kernel(h_ref, w_ref, cos_ref, sa_ref, sb_ref, lo_ref, hi_ref):
    n = pl.program_id(1)
    acc = jnp.dot(h_ref[...], w_ref[...], preferred_element_type=F32)
    is_qk = (n >= 2) & (n < 6)

    @pl.when(is_qk)
    def _():
        scale = jnp.where(n < 4, QK_SCALE, 1.0).astype(F32)
        cos = cos_ref[...]
        sa = sa_ref[...]
        sb = sb_ref[...]
        for g in range(PROJ_TN // LANE):
            blk = acc[:, g * LANE:(g + 1) * LANE]
            r = blk * cos + pltpu.roll(blk, 32, 1) * sa + pltpu.roll(blk, 96, 1) * sb
            lo_ref[:, g * LANE:(g + 1) * LANE] = (r * scale).astype(lo_ref.dtype)

    @pl.when(jnp.logical_and(jnp.logical_not(is_qk), n < LOW_BLOCKS))
    def _():
        lo_ref[...] = acc.astype(lo_ref.dtype)

    @pl.when(n >= LOW_BLOCKS)
    def _():
        hi_ref[...] = acc


def _inproj(h, w, tables):
    cos, sa, sb = tables
    tab_spec = pl.BlockSpec((PROJ_TM, LANE), lambda m, n: (m, 0))
    return pl.pallas_call(
        _inproj_kernel,
        grid=(T_ROWS // PROJ_TM, IN_WIDTH // PROJ_TN),
        in_specs=[pl.BlockSpec((PROJ_TM, D_MODEL), lambda m, n: (m, 0)),
                  pl.BlockSpec((D_MODEL, PROJ_TN), lambda m, n: (0, n)),
                  tab_spec, tab_spec, tab_spec],
        out_specs=[pl.BlockSpec((PROJ_TM, PROJ_TN), lambda m, n: (m, jnp.minimum(n, LOW_BLOCKS - 1))),
                   pl.BlockSpec((PROJ_TM, PROJ_TN), lambda m, n: (m, jnp.maximum(n - LOW_BLOCKS, 0)))],
        out_shape=[jax.ShapeDtypeStruct((T_ROWS, LOW_WIDTH), BF16),
                   jax.ShapeDtypeStruct((T_ROWS, HI_WIDTH), F32)],
        compiler_params=_cparams(("parallel", "arbitrary")),
        name="inproj",
    )(h, w, cos, sa, sb)


def _gmlp_kernel(u_ref, v_ref, g_ref, b_ref, ws_ref, bs_ref, o_ref):
    for h in range(A_HEADS):
        sl = slice(h * HEAD_DIM, (h + 1) * HEAD_DIM)
        v = _gelu_tanh(v_ref[:, sl].astype(F32))
        mu = jnp.mean(v, axis=-1, keepdims=True)
        var = jnp.mean(jnp.square(v - mu), axis=-1, keepdims=True)
        vn = (v - mu) * lax.rsqrt(var + LN_EPS) * g_ref[:, sl] + b_ref[:, sl]
        s = jnp.dot(ws_ref[h], vn.astype(BF16), preferred_element_type=F32) + bs_ref[:, h:h + 1]
        u = _gelu_tanh(u_ref[:, sl].astype(F32))
        o_ref[:, sl] = (u * s).astype(o_ref.dtype)


def _gmlp(p_lo, ln_g, ln_b, ws, bs_t):
    return pl.pallas_call(
        _gmlp_kernel,
        grid=(T_ROWS // A_CHUNK,),
        in_specs=[pl.BlockSpec((A_CHUNK, A_WIDTH), lambda i: (i, 0)),
                  pl.BlockSpec((A_CHUNK, A_WIDTH), lambda i: (i, 1)),
                  pl.BlockSpec((1, A_WIDTH), lambda i: (0, 0)),
                  pl.BlockSpec((1, A_WIDTH), lambda i: (0, 0)),
                  pl.BlockSpec((A_HEADS, A_CHUNK, A_CHUNK), lambda i: (0, 0, 0)),
                  pl.BlockSpec((A_CHUNK, A_HEADS), lambda i: (0, 0))],
        out_specs=pl.BlockSpec((A_CHUNK, A_WIDTH), lambda i: (i, 0)),
        out_shape=jax.ShapeDtypeStruct((T_ROWS, A_WIDTH), BF16),
        compiler_params=_cparams(("parallel",)),
        name="gmlp",
    )(p_lo, p_lo, ln_g, ln_b, ws, bs_t)


ATT_LAT_STEPS = N_LAT // ATT_TQ
ATT_STEPS_PER_BATCH = SEQ // ATT_TQ


def _attn_kernel(lam_ref, q_ref, kl_ref, vl_ref, kc_ref, vc_ref, g_ref, o_ref, m_scr, l_scr, acc_scr):
    step = pl.program_id(1)
    m_scr[...] = jnp.full(m_scr.shape, -1e30, F32)
    l_scr[...] = jnp.zeros(l_scr.shape, F32)
    acc_scr[...] = jnp.zeros(acc_scr.shape, F32)

    def process(k, v, visible=None):
        for mp in range(2):
            sl = slice(mp * HEAD_DIM, (mp + 1) * HEAD_DIM)
            s = _dot_nt(q_ref[:, sl], k[:, sl])
            if visible is not None:
                s = jnp.where(visible, s, -1e30)
            cols = [s[:, c * LANE:(c + 1) * LANE] for c in range(k.shape[0] // LANE)]
            m_part = cols[0]
            for col in cols[1:]:
                m_part = jnp.maximum(m_part, col)
            m_old = m_scr[mp]
            m_new = jnp.maximum(m_old, jnp.max(m_part, axis=-1, keepdims=True))
            alpha = jnp.exp2(m_old - m_new)
            l_part = alpha * l_scr[mp]
            ps = []
            for col in cols:
                p = jnp.exp2(col - m_new)
                l_part = l_part + p
                ps.append(p.astype(BF16))
            pv = jnp.dot(jnp.concatenate(ps, axis=1), v, preferred_element_type=F32)
            for half in range(B_V_DIM // LANE):
                hs = slice(half * LANE, (half + 1) * LANE)
                acc_scr[mp, :, hs] = alpha * acc_scr[mp, :, hs] + pv[:, hs]
            l_scr[mp] = l_part
            m_scr[mp] = m_new

    @pl.when(step < ATT_LAT_STEPS)
    def _():
        b = step // ATT_STEPS_PER_BATCH
        ctx_rows = pl.ds(pl.multiple_of(b * CTX_LEN, CTX_LEN), CTX_LEN)
        process(kc_ref[ctx_rows, :], vc_ref[ctx_rows, :])
        for j in range(SEQ // ATT_TK):
            rows = slice(j * ATT_TK, (j + 1) * ATT_TK)
            process(kl_ref[rows, :], vl_ref[rows, :])

    @pl.when(step == ATT_LAT_STEPS)
    def _():
        q_batch = lax.broadcasted_iota(jnp.int32, (ATT_TQ, N_CTX), 0) // CTX_LEN
        k_batch = lax.broadcasted_iota(jnp.int32, (ATT_TQ, N_CTX), 1) // CTX_LEN
        process(kc_ref[...], vc_ref[...], q_batch == k_batch)

    lam = lam_ref[0]
    post = lam_ref[1]
    l0 = jnp.sum(l_scr[0], axis=-1, keepdims=True)
    l1 = jnp.sum(l_scr[1], axis=-1, keepdims=True)
    o = acc_scr[0] / l0 - lam * (acc_scr[1] / l1)
    r = o * lax.rsqrt(jnp.mean(jnp.square(o), axis=-1, keepdims=True) + NORM_EPS)
    o_ref[...] = (r * g_ref[...] * post).astype(o_ref.dtype)


def _diff_attention(p_lo, lam2, subln_g):
    assert ATT_TQ == N_CTX
    lat_batch = lambda s: jnp.minimum(s // ATT_STEPS_PER_BATCH, BATCH - 1)
    return pl.pallas_call(
        _attn_kernel,
        grid=(B_HEADS, ATT_LAT_STEPS + 1),
        in_specs=[pl.BlockSpec(memory_space=pltpu.SMEM),
                  pl.BlockSpec((ATT_TQ, B_V_DIM), lambda h, s: (s, 4 + h)),
                  pl.BlockSpec((SEQ, B_V_DIM), lambda h, s: (lat_batch(s), 8 + h)),
                  pl.BlockSpec((SEQ, B_V_DIM), lambda h, s: (lat_batch(s), 12 + h)),
                  pl.BlockSpec((N_CTX, B_V_DIM), lambda h, s: (N_LAT // N_CTX, 8 + h)),
                  pl.BlockSpec((N_CTX, B_V_DIM), lambda h, s: (N_LAT // N_CTX, 12 + h)),
                  pl.BlockSpec((1, B_V_DIM), lambda h, s: (0, 0))],
        out_specs=pl.BlockSpec((ATT_TQ, B_V_DIM), lambda h, s: (s, h)),
        scratch_shapes=[pltpu.VMEM((2, ATT_TQ, LANE), F32),
                        pltpu.VMEM((2, ATT_TQ, LANE), F32),
                        pltpu.VMEM((2, ATT_TQ, B_V_DIM), F32)],
        out_shape=jax.ShapeDtypeStruct((T_ROWS, B_WIDTH), BF16),
        compiler_params=_cparams(("parallel", "arbitrary")),
        name="diff_attention",
    )(lam2, p_lo, p_lo, p_lo, p_lo, p_lo, subln_g)


def _scan_structure():
    c = SCAN_C
    x = np.zeros((2, (2 + SCAN_LEVELS) * c, c), np.float32)
    msk = np.zeros((2, SCAN_LEVELS + 1, c, c), np.float32)
    r = np.arange(c)
    j = np.arange(c)[None, :]
    xf = np.zeros(((2 + SCAN_LEVELS) * c, c), np.float32)
    mf = np.zeros((SCAN_LEVELS + 1, c, c), np.float32)
    xf[0:c] = (j <= r[:, None])
    xf[c:2 * c] = (j > r[:, None])
    for lev in range(SCAN_LEVELS):
        half = (c // 2) >> lev
        start = (r // (2 * half)) * (2 * half)
        mid = start + half - 1
        later = r > mid
        rows = np.where(later[:, None], (j > mid[:, None]) & (j <= r[:, None]),
                        (j > r[:, None]) & (j <= mid[:, None]))
        xf[(2 + lev) * c:(3 + lev) * c] = rows
        same = start[:, None] == start[None, :]
        mf[lev] = same & later[:, None] & (~later)[None, :]
    mf[SCAN_LEVELS] = np.eye(c)
    x[0] = xf
    msk[0] = mf
    x[1] = xf.reshape(2 + SCAN_LEVELS, c, c)[:, ::-1, ::-1].reshape(-1, c)
    msk[1] = mf[:, ::-1, ::-1]
    return x, msk


def _hgrn_kernel(qf_ref, ff_ref, if_ref, qb_ref, fb_ref, ib_ref, lb_ref, x_ref, msk_ref,
                 of_ref, ob_ref, st_scr):
    j = pl.program_id(1)

    @pl.when(j == 0)
    def _():
        st_scr[...] = jnp.zeros(st_scr.shape, F32)

    c = SCAN_C
    dirs = ((qf_ref, ff_ref, if_ref, of_ref, c - 1), (qb_ref, fb_ref, ib_ref, ob_ref, 0))
    for d, (q_ref, f_ref, i_ref, o_ref, end_row) in enumerate(dirs):
        for h in range(C_HEADS):
            sl = slice(h * HEAD_DIM, (h + 1) * HEAD_DIM)
            qraw = q_ref[:, sl]
            q = qraw * _sigmoid(qraw)
            lb = lb_ref[d:d + 1, sl]
            f = lb + (1.0 - lb) * _sigmoid(f_ref[:, sl])
            logf = jnp.log(f)
            k = 1.0 - f
            v = i_ref[:, sl].astype(BF16)
            hi = logf.astype(BF16)
            lo = (logf - hi.astype(F32)).astype(BF16)
            e2 = jnp.dot(x_ref[d], jnp.concatenate([hi, lo], axis=1), preferred_element_type=F32)
            w = jnp.exp(e2[:, 0:c] + e2[:, c:2 * c])
            a = msk_ref[d, SCAN_LEVELS] * _dot_nt(q.astype(BF16), k.astype(BF16))
            for lev in range(SCAN_LEVELS):
                wl = w[(2 + lev) * c:(3 + lev) * c]
                a = a + msk_ref[d, lev] * _dot_nt((q * wl).astype(BF16), (k * wl).astype(BF16))
            st = st_scr[d, h]
            o = jnp.dot(a.astype(BF16), v, preferred_element_type=F32)
            o = o + _dot_nt((q * w[0:c]).astype(BF16), st.astype(BF16))
            o_ref[:, sl] = o
            g_end = w[end_row:end_row + 1, :]
            st_scr[d, h] = st * g_end + _dot_tn(v, (k * w[c:2 * c]).astype(BF16))


def _hgrn_scan(p_hi, lb2, xmat, masks):
    lat_chunks = SEQ // SCAN_C
    ctx_chunks = CTX_LEN // SCAN_C
    steps = ctx_chunks + lat_chunks

    def fwd_rows(b, j):
        return jnp.where(j < ctx_chunks, N_LAT // SCAN_C + b * ctx_chunks + j,
                         b * lat_chunks + j - ctx_chunks)

    def bwd_rows(b, j):
        return jnp.where(j < ctx_chunks, N_LAT // SCAN_C + b * ctx_chunks + (ctx_chunks - 1 - j),
                         b * lat_chunks + (steps - 1 - j))

    def spec(rows, col):
        return pl.BlockSpec((SCAN_C, C_WIDTH), lambda b, j: (rows(b, j), col))

    nx = (2 + SCAN_LEVELS) * SCAN_C
    return pl.pallas_call(
        _hgrn_kernel,
        grid=(BATCH, steps),
        in_specs=[spec(fwd_rows, 0), spec(fwd_rows, 1), spec(fwd_rows, 3),
                  spec(bwd_rows, 0), spec(bwd_rows, 2), spec(bwd_rows, 3),
                  pl.BlockSpec((2, C_WIDTH), lambda b, j: (0, 0)),
                  pl.BlockSpec((2, nx, SCAN_C), lambda b, j: (0, 0, 0)),
                  pl.BlockSpec((2, SCAN_LEVELS + 1, SCAN_C, SCAN_C), lambda b, j: (0, 0, 0, 0))],
        out_specs=[spec(fwd_rows, 0), spec(bwd_rows, 0)],
        out_shape=[jax.ShapeDtypeStruct((T_ROWS, C_WIDTH), F32)] * 2,
        scratch_shapes=[pltpu.VMEM((2, C_HEADS, HEAD_DIM, HEAD_DIM), F32)],
        compiler_params=_cparams(("parallel", "arbitrary")),
        name="hgrn2_scan",
    )(p_hi, p_hi, p_hi, p_hi, p_hi, p_hi, lb2, xmat, masks)


def _layer_norm_rows(z, g, b):
    mu = jnp.mean(z, axis=-1, keepdims=True)
    var = jnp.mean(jnp.square(z - mu), axis=-1, keepdims=True)
    return (z - mu) * lax.rsqrt(var + LN_EPS) * g + b


def _outproj_kernel(a_ref, b_ref, of_ref, ob_ref, cg_ref, hg_ref, x_ref, mod_ref, w_ref,
                    l1g_ref, l1b_ref, rw_ref, rb_ref,
                    x1_ref, hx_ref, hxp_ref, idx_ref, wt_ref, rank_ref, cnt_ref, cnt_scr):
    parts = []
    for h in range(C_HEADS):
        sl = slice(h * HEAD_DIM, (h + 1) * HEAD_DIM)
        o = of_ref[:, sl] + ob_ref[:, sl]
        on = o * lax.rsqrt(jnp.mean(jnp.square(o), axis=-1, keepdims=True) + NORM_EPS) * hg_ref[...]
        g = cg_ref[:, sl]
        parts.append((on * (g * _sigmoid(g))).astype(BF16))
    c_x = jnp.concatenate(parts, axis=1)
    mix = jnp.dot(a_ref[...], w_ref[0:A_WIDTH, :], preferred_element_type=F32)
    mix = mix + jnp.dot(b_ref[...], w_ref[A_WIDTH:A_WIDTH + B_WIDTH, :], preferred_element_type=F32)
    mix = mix + jnp.dot(c_x, w_ref[A_WIDTH + B_WIDTH:, :], preferred_element_type=F32)
    g1 = mod_ref[0, 2:3, :]
    sh2 = mod_ref[0, 3:4, :]
    sc2 = mod_ref[0, 4:5, :]
    x1 = _layer_norm_rows(DEEPNORM_ALPHA * x_ref[...] + g1 * mix, l1g_ref[...], l1b_ref[...])
    x1_ref[...] = x1
    hx = x1 * (1.0 + sc2) + sh2
    hx_bf = hx.astype(BF16)
    hx_ref[...] = hx_bf
    hxp_ref[...] = hx

    logits = jnp.dot(hx.astype(BF16), rw_ref[...], preferred_element_type=F32)
    scores = _sigmoid(logits)
    sel = scores + rb_ref[...]
    tm = sel.shape[0]
    lane = lax.broadcasted_iota(jnp.int32, sel.shape, 1).astype(F32)
    slot = lax.broadcasted_iota(jnp.int32, (tm, LANE), 1)
    idx_acc = jnp.zeros((tm, LANE), F32)
    wt_acc = jnp.zeros((tm, LANE), F32)
    chosen = jnp.zeros(sel.shape, F32)
    hits = []
    for k in range(TOP_K):
        mx = jnp.max(sel, axis=-1, keepdims=True)
        idx = jnp.min(jnp.where(sel == mx, lane, float(N_EXPERTS)), axis=-1, keepdims=True)
        hit = lane == idx
        hits.append(hit)
        chosen = jnp.where(hit, 1.0, chosen)
        w_k = jnp.sum(jnp.where(hit, scores, 0.0), axis=-1, keepdims=True)
        idx_acc = jnp.where(slot == k, idx, idx_acc)
        wt_acc = jnp.where(slot == k, w_k, wt_acc)
        sel = jnp.where(hit, -jnp.inf, sel)
    idx_ref[...] = idx_acc.astype(jnp.int32)
    wt_ref[...] = wt_acc / jnp.sum(wt_acc, axis=-1, keepdims=True) * ROUTED_SCALE

    @pl.when(pl.program_id(0) == 0)
    def _():
        cnt_scr[...] = jnp.zeros(cnt_scr.shape, F32)

    below = (lax.broadcasted_iota(jnp.int32, (tm, tm), 0) > lax.broadcasted_iota(jnp.int32, (tm, tm), 1))
    before = jnp.dot(below.astype(BF16), chosen.astype(BF16), preferred_element_type=F32) + cnt_scr[0:1, :]
    rank_acc = jnp.zeros((tm, LANE), F32)
    for k in range(TOP_K):
        r_k = jnp.sum(jnp.where(hits[k], before, 0.0), axis=-1, keepdims=True)
        rank_acc = jnp.where(slot == k, r_k, rank_acc)
    rank_ref[...] = rank_acc.astype(jnp.int32)
    cnt = cnt_scr[...] + jnp.sum(chosen, axis=0, keepdims=True)
    cnt_scr[...] = cnt
    cnt_ref[...] = cnt


def _outproj(a_x, b_x, o_f, o_b, p_hi, hnorm_g, x, mod3, w_out, ln_g, ln_b, router_w, router_b):
    row = lambda width: pl.BlockSpec((OUT_TM, width), lambda i: (i, 0))
    const = lambda shape: pl.BlockSpec(shape, lambda i: tuple(0 for _ in shape))
    return pl.pallas_call(
        _outproj_kernel,
        grid=(T_ROWS // OUT_TM,),
        in_specs=[row(A_WIDTH), row(B_WIDTH), row(C_WIDTH), row(C_WIDTH),
                  pl.BlockSpec((OUT_TM, C_WIDTH), lambda i: (i, 4)),
                  const((1, HEAD_DIM)),
                  row(D_MODEL),
                  pl.BlockSpec((1, 6, D_MODEL), lambda i: (_mod_row(i, OUT_TM), 0, 0)),
                  const((D_MODEL, D_MODEL)),
                  const((1, D_MODEL)), const((1, D_MODEL)),
                  const((D_MODEL, N_EXPERTS)), const((1, N_EXPERTS))],
        out_specs=[row(D_MODEL), row(D_MODEL), row(D_MODEL), row(LANE), row(LANE), row(LANE),
                   const((8, N_EXPERTS))],
        out_shape=[jax.ShapeDtypeStruct((T_ROWS, D_MODEL), F32),
                   jax.ShapeDtypeStruct((T_ROWS, D_MODEL), BF16),
                   jax.ShapeDtypeStruct((T_ROWS, D_MODEL), F32),
                   jax.ShapeDtypeStruct((T_ROWS, LANE), jnp.int32),
                   jax.ShapeDtypeStruct((T_ROWS, LANE), F32),
                   jax.ShapeDtypeStruct((T_ROWS, LANE), jnp.int32),
                   jax.ShapeDtypeStruct((8, N_EXPERTS), F32)],
        scratch_shapes=[pltpu.VMEM((8, N_EXPERTS), F32)],
        compiler_params=_cparams(("arbitrary",)),
        name="outproj_ln_router",
    )(a_x, b_x, o_f, o_b, p_hi, hnorm_g, x, mod3, w_out, ln_g, ln_b, router_w, router_b)


def _experts_kernel(ord_ref, elist_ref, cnt_ref, xs_ref, wg_hbm, wu_hbm, wd_hbm, ys_ref,
                    wg_buf, wu_buf, wd_buf, wgu_scr, wd_scr, sems, *, packed, layer):
    i = pl.program_id(0)
    live = i < cnt_ref[0]
    o = ord_ref[i]
    first_of_expert = jnp.logical_or(i == 0, o != ord_ref[jnp.maximum(i - 1, 0)])

    def weight_copies(which, slot):
        e = elist_ref[which]
        return (pltpu.make_async_copy(wg_hbm.at[layer, e], wg_buf.at[slot], sems.at[slot, 0]),
                pltpu.make_async_copy(wu_hbm.at[layer, e], wu_buf.at[slot], sems.at[slot, 1]),
                pltpu.make_async_copy(wd_hbm.at[layer, e], wd_buf.at[slot], sems.at[slot, 2]))

    @pl.when(i == 0)
    def _():
        for c in weight_copies(0, 0):
            c.start()

    @pl.when(jnp.logical_and(live, first_of_expert))
    def _():
        slot = o % 2
        for c in weight_copies(o, slot):
            c.wait()
        wgu_scr[:, :EXPERT_DIM] = wg_buf[slot].astype(BF16)
        wgu_scr[:, EXPERT_DIM:] = wu_buf[slot].astype(BF16)
        wd_scr[...] = wd_buf[slot].astype(BF16)

        @pl.when(o + 1 < cnt_ref[1])
        def _():
            for c in weight_copies(o + 1, 1 - slot):
                c.start()

    @pl.when(live)
    def _():
        if packed:
            x = xs_ref[...].astype(BF16)
        else:
            x = xs_ref[...]
        h = jnp.dot(x, wgu_scr[...], preferred_element_type=F32)
        g = h[:, :EXPERT_DIM]
        u = h[:, EXPERT_DIM:]
        a = (g * _sigmoid(g) * u).astype(BF16)
        y = jnp.dot(a, wd_scr[...], preferred_element_type=F32)
        ys_ref[...] = y.astype(ys_ref.dtype)

    @pl.when(jnp.logical_not(live))
    def _():
        ys_ref[...] = jnp.zeros(ys_ref.shape, ys_ref.dtype)


def _experts(tile_ord, expert_list, counts2, xs, w_gate, w_up, w_down, layer):
    tiles = xs.shape[0] // MOE_TM
    last = lambda i, cnt: jnp.minimum(i, cnt[0] - 1)
    grid_spec = pltpu.PrefetchScalarGridSpec(
        num_scalar_prefetch=3,
        grid=(tiles,),
        in_specs=[pl.BlockSpec((MOE_TM, xs.shape[1]), lambda i, od, el, cnt: (last(i, cnt), 0)),
                  pl.BlockSpec(memory_space=pl.ANY),
                  pl.BlockSpec(memory_space=pl.ANY),
                  pl.BlockSpec(memory_space=pl.ANY)],
        out_specs=pl.BlockSpec((MOE_TM, xs.shape[1]), lambda i, od, el, cnt: (i, 0)),
        scratch_shapes=[pltpu.VMEM((2, D_MODEL, EXPERT_DIM), F32),
                        pltpu.VMEM((2, D_MODEL, EXPERT_DIM), F32),
                        pltpu.VMEM((2, EXPERT_DIM, D_MODEL), F32),
                        pltpu.VMEM((D_MODEL, 2 * EXPERT_DIM), BF16),
                        pltpu.VMEM((EXPERT_DIM, D_MODEL), BF16),
                        pltpu.SemaphoreType.DMA((2, 3))],
    )
    return pl.pallas_call(
        functools.partial(_experts_kernel, packed=xs.dtype == F32, layer=layer),
        grid_spec=grid_spec,
        out_shape=jax.ShapeDtypeStruct(xs.shape, xs.dtype),
        compiler_params=_cparams(("arbitrary",), vmem=EXPERTS_VMEM_LIMIT),
        name="experts",
    )(tile_ord, expert_list, counts2, xs, w_gate, w_up, w_down)


def _combine_kernel(yg_ref, wt_ref, ysh_ref, x1_ref, mod_ref, g_ref, b_ref, nmod_ref, o_ref, h_ref):
    tm = COMBINE_TM
    y = ysh_ref[...].astype(F32)
    for k in range(TOP_K):
        y = y + wt_ref[:, k:k + 1] * yg_ref[k]
    g2 = mod_ref[0, 5:6, :]
    x2 = _layer_norm_rows(DEEPNORM_ALPHA * x1_ref[...] + g2 * y, g_ref[...], b_ref[...])
    o_ref[...] = x2
    h_ref[...] = (x2 * (1.0 + nmod_ref[0, 1:2, :]) + nmod_ref[0, 0:1, :]).astype(h_ref.dtype)


def _combine(yg, wt, ysh, x1, mod3, ln_g, ln_b, next_mod3):
    tm = COMBINE_TM
    row = lambda width: pl.BlockSpec((tm, width), lambda i: (i, 0))
    mod_spec = pl.BlockSpec((1, 6, D_MODEL), lambda i: (_mod_row(i, tm), 0, 0))
    return pl.pallas_call(
        _combine_kernel,
        grid=(T_ROWS // tm,),
        in_specs=[pl.BlockSpec((TOP_K, tm, D_MODEL), lambda i: (0, i, 0)),
                  row(LANE), row(D_MODEL), row(D_MODEL),
                  mod_spec,
                  pl.BlockSpec((1, D_MODEL), lambda i: (0, 0)),
                  pl.BlockSpec((1, D_MODEL), lambda i: (0, 0)),
                  mod_spec],
        out_specs=[row(D_MODEL), row(D_MODEL)],
        out_shape=[jax.ShapeDtypeStruct((T_ROWS, D_MODEL), F32),
                   jax.ShapeDtypeStruct((T_ROWS, D_MODEL), BF16)],
        compiler_params=_cparams(("parallel",)),
        name="moe_combine_ln",
    )(yg, wt, ysh, x1, mod3, ln_g, ln_b, next_mod3)


def _layout(idx, rank, counts):
    tiles_per = (counts + MOE_TM - 1) // MOE_TM
    tile_end = jnp.cumsum(tiles_per)
    start = (tile_end - tiles_per) * MOE_TM
    experts = jnp.arange(N_EXPERTS, dtype=jnp.int32)
    pos = rank + jnp.sum(jnp.where(idx[:, :, None] == experts, start, 0), axis=-1)
    tile = jnp.arange(MOE_TILES, dtype=jnp.int32)
    tile_expert = jnp.sum((tile_end[None, :] <= tile[:, None]).astype(jnp.int32), axis=-1)
    tile_expert = jnp.minimum(tile_expert, N_EXPERTS - 1)
    used = (counts > 0).astype(jnp.int32)
    used_rank = jnp.cumsum(used) - used
    n_used = jnp.sum(used)
    expert_list = jnp.sum(jnp.where((used_rank[None, :] == experts[:, None]) & (used[None, :] > 0),
                                    experts[None, :], 0), axis=-1)
    tile_ord = jnp.minimum(jnp.take(used_rank, tile_expert), jnp.maximum(n_used - 1, 0))
    counts2 = jnp.stack([tile_end[-1], n_used]).astype(jnp.int32)
    return pos, tile_ord.astype(jnp.int32), expert_list.astype(jnp.int32), counts2


def _dispatch_kernel(pos_ref, hxw_ref, init_hbm, xs_hbm, sem):
    def issue(t, carry):
        for k in range(TOP_K):
            pltpu.make_async_copy(hxw_ref.at[pl.ds(t, 1), :],
                                  xs_hbm.at[pl.ds(pos_ref[t * TOP_K + k], 1), :], sem).start(priority=k % 2)
        return carry

    lax.fori_loop(0, DISPATCH_TM, issue, 0)
    for _ in range(TOP_K):
        pltpu.make_async_copy(hxw_ref, xs_hbm.at[pl.ds(0, DISPATCH_TM), :], sem).wait()


def _dispatch_rows(pos, hx_words, init):
    return pl.pallas_call(
        _dispatch_kernel,
        grid=(T_ROWS // DISPATCH_TM,),
        in_specs=[pl.BlockSpec((DISPATCH_TM * TOP_K,), lambda i: (i,), memory_space=pltpu.SMEM),
                  pl.BlockSpec((DISPATCH_TM, D_MODEL), lambda i: (i, 0)),
                  pl.BlockSpec(memory_space=pl.ANY)],
        out_specs=pl.BlockSpec(memory_space=pl.ANY),
        out_shape=jax.ShapeDtypeStruct((MOE_ROWS, D_MODEL), F32),
        scratch_shapes=[pltpu.SemaphoreType.DMA(())],
        input_output_aliases={2: 0},
        compiler_params=pltpu.CompilerParams(dimension_semantics=("arbitrary",), has_side_effects=True),
        name="moe_dispatch_rows",
    )(pos.reshape(-1), hx_words, init)


def _rope_tables():
    t = jnp.arange(SEQ)
    row = (t // GRID_W).astype(F32)
    col = (t % GRID_W).astype(F32)
    axis_dim = HEAD_DIM // 2
    inv = ROPE_THETA ** (-jnp.arange(0, axis_dim, 2, dtype=F32) / axis_dim)
    ar = row[:, None] * inv
    ac = col[:, None] * inv
    cos = jnp.concatenate([jnp.cos(ar), jnp.cos(ar), jnp.cos(ac), jnp.cos(ac)], axis=-1)
    sin = jnp.concatenate([jnp.sin(ar), jnp.sin(ar), jnp.sin(ac), jnp.sin(ac)], axis=-1)
    odd = ((jnp.arange(HEAD_DIM) // 32) % 2 == 1)[None, :]
    sa = jnp.where(odd, sin, 0.0)
    sb = jnp.where(odd, 0.0, -sin)
    lat = lambda z: jnp.tile(z, (BATCH, 1))
    cos_all = jnp.concatenate([lat(cos), jnp.ones((N_CTX, HEAD_DIM), F32)], axis=0)
    sa_all = jnp.concatenate([lat(sa), jnp.zeros((N_CTX, HEAD_DIM), F32)], axis=0)
    sb_all = jnp.concatenate([lat(sb), jnp.zeros((N_CTX, HEAD_DIM), F32)], axis=0)
    return cos_all, sa_all, sb_all


def kernel(x, c, ctx, c_ctx, w_mod, b_mod, w_in, w_out, gmlp_ln_g, gmlp_ln_b, gmlp_ws, gmlp_bs,
           diff_lam, diff_subln_g, hgrn_lb, hgrn_norm_g, ln1_g, ln1_b, ln2_g, ln2_b,
           router_w, router_b, exp_w_gate, exp_w_up, exp_w_down, sh_w_gate, sh_w_up, sh_w_down):
    assert x.shape == (BATCH, SEQ, D_MODEL) and ctx.shape == (BATCH, CTX_LEN, D_MODEL)
    xs_all = jnp.concatenate([x.reshape(N_LAT, D_MODEL), ctx.reshape(N_CTX, D_MODEL)], axis=0)

    cond = jnp.zeros((8, D_MODEL), F32).at[0:BATCH].set(c).at[BATCH].set(c_ctx)
    mod_all = _modulation(cond, w_mod, b_mod)[:, :BATCH + 1].reshape(DEPTH, BATCH + 1, 6, D_MODEL)

    sm = jax.nn.softmax(hgrn_lb.astype(F32), axis=0)
    lb_all = jnp.cumsum(sm, axis=0) - sm[0]
    tables = _rope_tables()
    xmat_np, masks_np = _scan_structure()
    xmat = jnp.asarray(xmat_np, BF16)
    masks = jnp.asarray(masks_np, F32)

    xs_grouped = jnp.zeros((MOE_ROWS, D_MODEL), F32)
    h_in = _modulate(xs_all, mod_all[0])
    for l in range(DEPTH):
        mod3 = mod_all[l]
        w_in_l = w_in[l].astype(BF16)
        p_lo, p_hi = _inproj(h_in, w_in_l, tables)

        a_x = _gmlp(p_lo, gmlp_ln_g[l][None, :], gmlp_ln_b[l][None, :],
                    gmlp_ws[l].astype(BF16), gmlp_bs[l].T)

        lam_init = 0.8 - 0.6 * math.exp(-0.3 * l)
        dl = diff_lam[l].astype(F32)
        lam = jnp.exp(jnp.sum(dl[0] * dl[1])) - jnp.exp(jnp.sum(dl[2] * dl[3])) + lam_init
        lam2 = jnp.stack([lam, jnp.asarray(1.0 - lam_init, F32)]).astype(F32)
        b_x = _diff_attention(p_lo, lam2, diff_subln_g[l][None, :])

        o_f, o_b = _hgrn_scan(p_hi, lb_all[l], xmat, masks)

        x1, hx, hx_words, idx, wt, rank, cnt = _outproj(
            a_x, b_x, o_f, o_b, p_hi, hgrn_norm_g[l][None, :], xs_all, mod3,
            w_out[l].astype(BF16), ln1_g[l][None, :], ln1_b[l][None, :],
            router_w[l].astype(BF16), router_b[l][None, :])

        pos, tile_ord, expert_list, counts2 = _layout(idx[:, :TOP_K], rank[:, :TOP_K], cnt[0].astype(jnp.int32))
        xs_grouped = _dispatch_rows(pos, hx_words, xs_grouped)
        ys = _experts(tile_ord, expert_list, counts2, xs_grouped, exp_w_gate, exp_w_up, exp_w_down, l)
        yg = jnp.take(ys, pos.T.reshape(-1), axis=0, mode="clip").reshape(TOP_K, T_ROWS, D_MODEL)

        ysh = _experts(jnp.zeros((T_ROWS // MOE_TM,), jnp.int32), jnp.zeros((1,), jnp.int32),
                       jnp.array([T_ROWS // MOE_TM, 1], jnp.int32),
                       hx, sh_w_gate[:, None], sh_w_up[:, None], sh_w_down[:, None], l)

        xs_all, h_in = _combine(yg, wt, ysh, x1, mod3, ln2_g[l][None, :], ln2_b[l][None, :],
                                mod_all[min(l + 1, DEPTH - 1)])

    return xs_all[:N_LAT].reshape(BATCH, SEQ, D_MODEL)
```

```python
import functools
import math

import numpy as np
import jax
import jax.numpy as jnp
from jax import lax
from jax.experimental import pallas as pl
from jax.experimental.pallas import tpu as pltpu

F32 = jnp.float32
BF16 = jnp.bfloat16

D_MODEL = 2048
BATCH = 2
SEQ = 4096
DEPTH = 4
CTX_LEN = 256
GRID_W = 64
HEAD_DIM = 128
A_WIDTH = 512
A_HEADS = 4
A_CHUNK = 128
B_WIDTH = 1024
B_HEADS = 4
B_QK_WIDTH = 1024
B_V_DIM = 256
ROPE_THETA = 10000.0
C_WIDTH = 512
C_HEADS = 4
IN_WIDTH = 6656
N_EXPERTS = 64
TOP_K = 8
EXPERT_DIM = 384
ROUTED_SCALE = 2.5
DEEPNORM_ALPHA = (2.0 * DEPTH) ** 0.25
LN_EPS = 1e-5
NORM_EPS = 1e-5

N_LAT = BATCH * SEQ
N_CTX = BATCH * CTX_LEN
T_ROWS = N_LAT + N_CTX
LOW_WIDTH = 4096
HI_WIDTH = IN_WIDTH - LOW_WIDTH
QK_SCALE = HEAD_DIM ** -0.5 * math.log2(math.e)

LANE = 128
PROJ_TM = 1088
MOD_TM = 512
PROJ_TN = 512
ATT_TQ = 512
ATT_TK = 1024
SCAN_C = 128
SCAN_LEVELS = 7
OUT_TM = 256
MOE_TM = 512
COMBINE_TM = 256
DISPATCH_TM = 256
MOE_ROWS = ((T_ROWS * TOP_K + N_EXPERTS * (MOE_TM - 1)) // MOE_TM) * MOE_TM
MOE_TILES = MOE_ROWS // MOE_TM
VMEM_LIMIT = 48 * 1024 * 1024
EXPERTS_VMEM_LIMIT = 56 * 1024 * 1024


def _cparams(sem, vmem=VMEM_LIMIT):
    return pltpu.CompilerParams(dimension_semantics=sem, vmem_limit_bytes=vmem)


def _sigmoid(x):
    return 1.0 / (1.0 + jnp.exp(-x))


def _gelu_tanh(x):
    return 0.5 * x * (1.0 + jnp.tanh(math.sqrt(2.0 / math.pi) * (x + 0.044715 * (x * x * x))))


def _dot_nt(a, b):
    return lax.dot_general(a, b, (((1,), (1,)), ((), ())), preferred_element_type=F32)


def _dot_tn(a, b):
    return lax.dot_general(a, b, (((0,), (0,)), ((), ())), preferred_element_type=F32)


def _mod_kernel(c_ref, w_ref, b_ref, o_ref):
    c = c_ref[...]
    s = (c * _sigmoid(c)).astype(BF16)
    o_ref[0] = jnp.dot(s, w_ref[0].astype(BF16), preferred_element_type=F32) + b_ref[0]


def _modulation(cond, w_mod, b_mod):
    tn = 1024
    return pl.pallas_call(
        _mod_kernel,
        grid=(DEPTH, 6 * D_MODEL // tn),
        in_specs=[pl.BlockSpec((8, D_MODEL), lambda l, n: (0, 0)),
                  pl.BlockSpec((1, D_MODEL, tn), lambda l, n: (l, 0, n)),
                  pl.BlockSpec((1, 1, tn), lambda l, n: (l, 0, n))],
        out_specs=pl.BlockSpec((1, 8, tn), lambda l, n: (l, 0, n)),
        out_shape=jax.ShapeDtypeStruct((DEPTH, 8, 6 * D_MODEL), F32),
        compiler_params=_cparams(("parallel", "parallel")),
        name="modulation",
    )(cond, w_mod, b_mod.reshape(DEPTH, 1, 6 * D_MODEL))


def _mod_row(m, tm):
    return jnp.minimum(m // (SEQ // tm), BATCH)


def _modulate_kernel(x_ref, mod_ref, o_ref):
    o_ref[...] = (x_ref[...] * (1.0 + mod_ref[0, 1:2, :]) + mod_ref[0, 0:1, :]).astype(o_ref.dtype)


def _modulate(x, mod3):
    tm = MOD_TM
    return pl.pallas_call(
        _modulate_kernel,
        grid=(T_ROWS // tm,),
        in_specs=[pl.BlockSpec((tm, D_MODEL), lambda i: (i, 0)),
                  pl.BlockSpec((1, 6, D_MODEL), lambda i: (_mod_row(i, tm), 0, 0))],
        out_specs=pl.BlockSpec((tm, D_MODEL), lambda i: (i, 0)),
        out_shape=jax.ShapeDtypeStruct((T_ROWS, D_MODEL), BF16),
        compiler_params=_cparams(("parallel",)),
        name="modulate",
    )(x, mod3)


LOW_BLOCKS = LOW_WIDTH // PROJ_TN


def _inproj_kernel(h_ref, w_ref, cos_ref, sa_ref, sb_ref, lo_ref, hi_ref):
    n = pl.program_id(1)
    acc = jnp.dot(h_ref[...], w_ref[...], preferred_element_type=F32)
    is_qk = (n >= 2) & (n < 6)

    @pl.when(is_qk)
    def _():
        scale = jnp.where(n < 4, QK_SCALE, 1.0).astype(F32)
        cos = cos_ref[...]
        sa = sa_ref[...]
        sb = sb_ref[...]
        for g in range(PROJ_TN // LANE):
            blk = acc[:, g * LANE:(g + 1) * LANE]
            r = blk * cos + pltpu.roll(blk, 32, 1) * sa + pltpu.roll(blk, 96, 1) * sb
            lo_ref[:, g * LANE:(g + 1) * LANE] = (r * scale).astype(lo_ref.dtype)

    @pl.when(jnp.logical_and(jnp.logical_not(is_qk), n < LOW_BLOCKS))
    def _():
        lo_ref[...] = acc.astype(lo_ref.dtype)

    @pl.when(n >= LOW_BLOCKS)
    def _():
        hi_ref[...] = acc


def _inproj(h, w, tables):
    cos, sa, sb = tables
    tab_spec = pl.BlockSpec((PROJ_TM, LANE), lambda m, n: (m, 0))
    return pl.pallas_call(
        _inproj_kernel,
        grid=(T_ROWS // PROJ_TM, IN_WIDTH // PROJ_TN),
        in_specs=[pl.BlockSpec((PROJ_TM, D_MODEL), lambda m, n: (m, 0)),
                  pl.BlockSpec((D_MODEL, PROJ_TN), lambda m, n: (0, n)),
                  tab_spec, tab_spec, tab_spec],
        out_specs=[pl.BlockSpec((PROJ_TM, PROJ_TN), lambda m, n: (m, jnp.minimum(n, LOW_BLOCKS - 1))),
                   pl.BlockSpec((PROJ_TM, PROJ_TN), lambda m, n: (m, jnp.maximum(n - LOW_BLOCKS, 0)))],
        out_shape=[jax.ShapeDtypeStruct((T_ROWS, LOW_WIDTH), BF16),
                   jax.ShapeDtypeStruct((T_ROWS, HI_WIDTH), F32)],
        compiler_params=_cparams(("parallel", "arbitrary")),
        name="inproj",
    )(h, w, cos, sa, sb)


def _gmlp_kernel(u_ref, v_ref, g_ref, b_ref, ws_ref, bs_ref, o_ref):
    for h in range(A_HEADS):
        sl = slice(h * HEAD_DIM, (h + 1) * HEAD_DIM)
        v = _gelu_tanh(v_ref[:, sl].astype(F32))
        mu = jnp.mean(v, axis=-1, keepdims=True)
        var = jnp.mean(jnp.square(v - mu), axis=-1, keepdims=True)
        vn = (v - mu) * lax.rsqrt(var + LN_EPS) * g_ref[:, sl] + b_ref[:, sl]
        s = jnp.dot(ws_ref[h], vn.astype(BF16), preferred_element_type=F32) + bs_ref[:, h:h + 1]
        u = _gelu_tanh(u_ref[:, sl].astype(F32))
        o_ref[:, sl] = (u * s).astype(o_ref.dtype)


def _gmlp(p_lo, ln_g, ln_b, ws, bs_t):
    return pl.pallas_call(
        _gmlp_kernel,
        grid=(T_ROWS // A_CHUNK,),
        in_specs=[pl.BlockSpec((A_CHUNK, A_WIDTH), lambda i: (i, 0)),
                  pl.BlockSpec((A_CHUNK, A_WIDTH), lambda i: (i, 1)),
                  pl.BlockSpec((1, A_WIDTH), lambda i: (0, 0)),
                  pl.BlockSpec((1, A_WIDTH), lambda i: (0, 0)),
                  pl.BlockSpec((A_HEADS, A_CHUNK, A_CHUNK), lambda i: (0, 0, 0)),
                  pl.BlockSpec((A_CHUNK, A_HEADS), lambda i: (0, 0))],
        out_specs=pl.BlockSpec((A_CHUNK, A_WIDTH), lambda i: (i, 0)),
        out_shape=jax.ShapeDtypeStruct((T_ROWS, A_WIDTH), BF16),
        compiler_params=_cparams(("parallel",)),
        name="gmlp",
    )(p_lo, p_lo, ln_g, ln_b, ws, bs_t)


ATT_LAT_STEPS = N_LAT // ATT_TQ
ATT_STEPS_PER_BATCH = SEQ // ATT_TQ


def _attn_kernel(lam_ref, q_ref, kl_ref, vl_ref, kc_ref, vc_ref, g_ref, o_ref, m_scr, l_scr, acc_scr):
    step = pl.program_id(1)
    m_scr[...] = jnp.full(m_scr.shape, -1e30, F32)
    l_scr[...] = jnp.zeros(l_scr.shape, F32)
    acc_scr[...] = jnp.zeros(acc_scr.shape, F32)

    def process(k, v, visible=None):
        for mp in range(2):
            sl = slice(mp * HEAD_DIM, (mp + 1) * HEAD_DIM)
            s = _dot_nt(q_ref[:, sl], k[:, sl])
            if visible is not None:
                s = jnp.where(visible, s, -1e30)
            cols = [s[:, c * LANE:(c + 1) * LANE] for c in range(k.shape[0] // LANE)]
            m_part = cols[0]
            for col in cols[1:]:
                m_part = jnp.maximum(m_part, col)
            m_old = m_scr[mp]
            m_new = jnp.maximum(m_old, jnp.max(m_part, axis=-1, keepdims=True))
            alpha = jnp.exp2(m_old - m_new)
            l_part = alpha * l_scr[mp]
            ps = []
            for col in cols:
                p = jnp.exp2(col - m_new)
                l_part = l_part + p
                ps.append(p.astype(BF16))
            pv = jnp.dot(jnp.concatenate(ps, axis=1), v, preferred_element_type=F32)
            for half in range(B_V_DIM // LANE):
                hs = slice(half * LANE, (half + 1) * LANE)
                acc_scr[mp, :, hs] = alpha * acc_scr[mp, :, hs] + pv[:, hs]
            l_scr[mp] = l_part
            m_scr[mp] = m_new

    @pl.when(step < ATT_LAT_STEPS)
    def _():
        b = step // ATT_STEPS_PER_BATCH
        ctx_rows = pl.ds(pl.multiple_of(b * CTX_LEN, CTX_LEN), CTX_LEN)
        process(kc_ref[ctx_rows, :], vc_ref[ctx_rows, :])
        for j in range(SEQ // ATT_TK):
            rows = slice(j * ATT_TK, (j + 1) * ATT_TK)
            process(kl_ref[rows, :], vl_ref[rows, :])

    @pl.when(step == ATT_LAT_STEPS)
    def _():
        q_batch = lax.broadcasted_iota(jnp.int32, (ATT_TQ, N_CTX), 0) // CTX_LEN
        k_batch = lax.broadcasted_iota(jnp.int32, (ATT_TQ, N_CTX), 1) // CTX_LEN
        process(kc_ref[...], vc_ref[...], q_batch == k_batch)

    lam = lam_ref[0]
    post = lam_ref[1]
    l0 = jnp.sum(l_scr[0], axis=-1, keepdims=True)
    l1 = jnp.sum(l_scr[1], axis=-1, keepdims=True)
    o = acc_scr[0] / l0 - lam * (acc_scr[1] / l1)
    r = o * lax.rsqrt(jnp.mean(jnp.square(o), axis=-1, keepdims=True) + NORM_EPS)
    o_ref[...] = (r * g_ref[...] * post).astype(o_ref.dtype)


def _diff_attention(p_lo, lam2, subln_g):
    assert ATT_TQ == N_CTX
    lat_batch = lambda s: jnp.minimum(s // ATT_STEPS_PER_BATCH, BATCH - 1)
    return pl.pallas_call(
        _attn_kernel,
        grid=(B_HEADS, ATT_LAT_STEPS + 1),
        in_specs=[pl.BlockSpec(memory_space=pltpu.SMEM),
                  pl.BlockSpec((ATT_TQ, B_V_DIM), lambda h, s: (s, 4 + h)),
                  pl.BlockSpec((SEQ, B_V_DIM), lambda h, s: (lat_batch(s), 8 + h)),
                  pl.BlockSpec((SEQ, B_V_DIM), lambda h, s: (lat_batch(s), 12 + h)),
                  pl.BlockSpec((N_CTX, B_V_DIM), lambda h, s: (N_LAT // N_CTX, 8 + h)),
                  pl.BlockSpec((N_CTX, B_V_DIM), lambda h, s: (N_LAT // N_CTX, 12 + h)),
                  pl.BlockSpec((1, B_V_DIM), lambda h, s: (0, 0))],
        out_specs=pl.BlockSpec((ATT_TQ, B_V_DIM), lambda h, s: (s, h)),
        scratch_shapes=[pltpu.VMEM((2, ATT_TQ, LANE), F32),
                        pltpu.VMEM((2, ATT_TQ, LANE), F32),
                        pltpu.VMEM((2, ATT_TQ, B_V_DIM), F32)],
        out_shape=jax.ShapeDtypeStruct((T_ROWS, B_WIDTH), BF16),
        compiler_params=_cparams(("parallel", "arbitrary")),
        name="diff_attention",
    )(lam2, p_lo, p_lo, p_lo, p_lo, p_lo, subln_g)


def _scan_structure():
    c = SCAN_C
    x = np.zeros((2, (2 + SCAN_LEVELS) * c, c), np.float32)
    msk = np.zeros((2, SCAN_LEVELS + 1, c, c), np.float32)
    r = np.arange(c)
    j = np.arange(c)[None, :]
    xf = np.zeros(((2 + SCAN_LEVELS) * c, c), np.float32)
    mf = np.zeros((SCAN_LEVELS + 1, c, c), np.float32)
    xf[0:c] = (j <= r[:, None])
    xf[c:2 * c] = (j > r[:, None])
    for lev in range(SCAN_LEVELS):
        half = (c // 2) >> lev
        start = (r // (2 * half)) * (2 * half)
        mid = start + half - 1
        later = r > mid
        rows = np.where(later[:, None], (j > mid[:, None]) & (j <= r[:, None]),
                        (j > r[:, None]) & (j <= mid[:, None]))
        xf[(2 + lev) * c:(3 + lev) * c] = rows
        same = start[:, None] == start[None, :]
        mf[lev] = same & later[:, None] & (~later)[None, :]
    mf[SCAN_LEVELS] = np.eye(c)
    x[0] = xf
    msk[0] = mf
    x[1] = xf.reshape(2 + SCAN_LEVELS, c, c)[:, ::-1, ::-1].reshape(-1, c)
    msk[1] = mf[:, ::-1, ::-1]
    return x, msk


def _hgrn_kernel(qf_ref, ff_ref, if_ref, qb_ref, fb_ref, ib_ref, lb_ref, x_ref, msk_ref,
                 of_ref, ob_ref, st_scr):
    j = pl.program_id(1)

    @pl.when(j == 0)
    def _():
        st_scr[...] = jnp.zeros(st_scr.shape, F32)

    c = SCAN_C
    dirs = ((qf_ref, ff_ref, if_ref, of_ref, c - 1), (qb_ref, fb_ref, ib_ref, ob_ref, 0))
    for d, (q_ref, f_ref, i_ref, o_ref, end_row) in enumerate(dirs):
        for h in range(C_HEADS):
            sl = slice(h * HEAD_DIM, (h + 1) * HEAD_DIM)
            qraw = q_ref[:, sl]
            q = qraw * _sigmoid(qraw)
            lb = lb_ref[d:d + 1, sl]
            f = lb + (1.0 - lb) * _sigmoid(f_ref[:, sl])
            logf = jnp.log(f)
            k = 1.0 - f
            v = i_ref[:, sl].astype(BF16)
            hi = logf.astype(BF16)
            lo = (logf - hi.astype(F32)).astype(BF16)
            e2 = jnp.dot(x_ref[d], jnp.concatenate([hi, lo], axis=1), preferred_element_type=F32)
            w = jnp.exp(e2[:, 0:c] + e2[:, c:2 * c])
            a = msk_ref[d, SCAN_LEVELS] * _dot_nt(q.astype(BF16), k.astype(BF16))
            for lev in range(SCAN_LEVELS):
                wl = w[(2 + lev) * c:(3 + lev) * c]
                a = a + msk_ref[d, lev] * _dot_nt((q * wl).astype(BF16), (k * wl).astype(BF16))
            st = st_scr[d, h]
            o = jnp.dot(a.astype(BF16), v, preferred_element_type=F32)
            o = o + _dot_nt((q * w[0:c]).astype(BF16), st.astype(BF16))
            o_ref[:, sl] = o
            g_end = w[end_row:end_row + 1, :]
            st_scr[d, h] = st * g_end + _dot_tn(v, (k * w[c:2 * c]).astype(BF16))


def _hgrn_scan(p_hi, lb2, xmat, masks):
    lat_chunks = SEQ // SCAN_C
    ctx_chunks = CTX_LEN // SCAN_C
    steps = ctx_chunks + lat_chunks

    def fwd_rows(b, j):
        return jnp.where(j < ctx_chunks, N_LAT // SCAN_C + b * ctx_chunks + j,
                         b * lat_chunks + j - ctx_chunks)

    def bwd_rows(b, j):
        return jnp.where(j < ctx_chunks, N_LAT // SCAN_C + b * ctx_chunks + (ctx_chunks - 1 - j),
                         b * lat_chunks + (steps - 1 - j))

    def spec(rows, col):
        return pl.BlockSpec((SCAN_C, C_WIDTH), lambda b, j: (rows(b, j), col))

    nx = (2 + SCAN_LEVELS) * SCAN_C
    return pl.pallas_call(
        _hgrn_kernel,
        grid=(BATCH, steps),
        in_specs=[spec(fwd_rows, 0), spec(fwd_rows, 1), spec(fwd_rows, 3),
                  spec(bwd_rows, 0), spec(bwd_rows, 2), spec(bwd_rows, 3),
                  pl.BlockSpec((2, C_WIDTH), lambda b, j: (0, 0)),
                  pl.BlockSpec((2, nx, SCAN_C), lambda b, j: (0, 0, 0)),
                  pl.BlockSpec((2, SCAN_LEVELS + 1, SCAN_C, SCAN_C), lambda b, j: (0, 0, 0, 0))],
        out_specs=[spec(fwd_rows, 0), spec(bwd_rows, 0)],
        out_shape=[jax.ShapeDtypeStruct((T_ROWS, C_WIDTH), F32)] * 2,
        scratch_shapes=[pltpu.VMEM((2, C_HEADS, HEAD_DIM, HEAD_DIM), F32)],
        compiler_params=_cparams(("parallel", "arbitrary")),
        name="hgrn2_scan",
    )(p_hi, p_hi, p_hi, p_hi, p_hi, p_hi, lb2, xmat, masks)


def _layer_norm_rows(z, g, b):
    mu = jnp.mean(z, axis=-1, keepdims=True)
    var = jnp.mean(jnp.square(z - mu), axis=-1, keepdims=True)
    return (z - mu) * lax.rsqrt(var + LN_EPS) * g + b


def _outproj_kernel(a_ref, b_ref, of_ref, ob_ref, cg_ref, hg_ref, x_ref, mod_ref, w_ref,
                    l1g_ref, l1b_ref, rw_ref, rb_ref,
                    x1_ref, hx_ref, hxp_ref, idx_ref, wt_ref, rank_ref, cnt_ref, cnt_scr):
    parts = []
    for h in range(C_HEADS):
        sl = slice(h * HEAD_DIM, (h + 1) * HEAD_DIM)
        o = of_ref[:, sl] + ob_ref[:, sl]
        on = o * lax.rsqrt(jnp.mean(jnp.square(o), axis=-1, keepdims=True) + NORM_EPS) * hg_ref[...]
        g = cg_ref[:, sl]
        parts.append((on * (g * _sigmoid(g))).astype(BF16))
    c_x = jnp.concatenate(parts, axis=1)
    mix = jnp.dot(a_ref[...], w_ref[0:A_WIDTH, :], preferred_element_type=F32)
    mix = mix + jnp.dot(b_ref[...], w_ref[A_WIDTH:A_WIDTH + B_WIDTH, :], preferred_element_type=F32)
    mix = mix + jnp.dot(c_x, w_ref[A_WIDTH + B_WIDTH:, :], preferred_element_type=F32)
    g1 = mod_ref[0, 2:3, :]
    sh2 = mod_ref[0, 3:4, :]
    sc2 = mod_ref[0, 4:5, :]
    x1 = _layer_norm_rows(DEEPNORM_ALPHA * x_ref[...] + g1 * mix, l1g_ref[...], l1b_ref[...])
    x1_ref[...] = x1
    hx = x1 * (1.0 + sc2) + sh2
    hx_bf = hx.astype(BF16)
    hx_ref[...] = hx_bf
    hxp_ref[...] = hx

    logits = jnp.dot(hx.astype(BF16), rw_ref[...], preferred_element_type=F32)
    scores = _sigmoid(logits)
    sel = scores + rb_ref[...]
    tm = sel.shape[0]
    lane = lax.broadcasted_iota(jnp.int32, sel.shape, 1).astype(F32)
    slot = lax.broadcasted_iota(jnp.int32, (tm, LANE), 1)
    idx_acc = jnp.zeros((tm, LANE), F32)
    wt_acc = jnp.zeros((tm, LANE), F32)
    chosen = jnp.zeros(sel.shape, F32)
    hits = []
    for k in range(TOP_K):
        mx = jnp.max(sel, axis=-1, keepdims=True)
        idx = jnp.min(jnp.where(sel == mx, lane, float(N_EXPERTS)), axis=-1, keepdims=True)
        hit = lane == idx
        hits.append(hit)
        chosen = jnp.where(hit, 1.0, chosen)
        w_k = jnp.sum(jnp.where(hit, scores, 0.0), axis=-1, keepdims=True)
        idx_acc = jnp.where(slot == k, idx, idx_acc)
        wt_acc = jnp.where(slot == k, w_k, wt_acc)
        sel = jnp.where(hit, -jnp.inf, sel)
    idx_ref[...] = idx_acc.astype(jnp.int32)
    wt_ref[...] = wt_acc / jnp.sum(wt_acc, axis=-1, keepdims=True) * ROUTED_SCALE

    @pl.when(pl.program_id(0) == 0)
    def _():
        cnt_scr[...] = jnp.zeros(cnt_scr.shape, F32)

    below = (lax.broadcasted_iota(jnp.int32, (tm, tm), 0) > lax.broadcasted_iota(jnp.int32, (tm, tm), 1))
    before = jnp.dot(below.astype(BF16), chosen.astype(BF16), preferred_element_type=F32) + cnt_scr[0:1, :]
    rank_acc = jnp.zeros((tm, LANE), F32)
    for k in range(TOP_K):
        r_k = jnp.sum(jnp.where(hits[k], before, 0.0), axis=-1, keepdims=True)
        rank_acc = jnp.where(slot == k, r_k, rank_acc)
    rank_ref[...] = rank_acc.astype(jnp.int32)
    cnt = cnt_scr[...] + jnp.sum(chosen, axis=0, keepdims=True)
    cnt_scr[...] = cnt
    cnt_ref[...] = cnt


def _outproj(a_x, b_x, o_f, o_b, p_hi, hnorm_g, x, mod3, w_out, ln_g, ln_b, router_w, router_b):
    row = lambda width: pl.BlockSpec((OUT_TM, width), lambda i: (i, 0))
    const = lambda shape: pl.BlockSpec(shape, lambda i: tuple(0 for _ in shape))
    return pl.pallas_call(
        _outproj_kernel,
        grid=(T_ROWS // OUT_TM,),
        in_specs=[row(A_WIDTH), row(B_WIDTH), row(C_WIDTH), row(C_WIDTH),
                  pl.BlockSpec((OUT_TM, C_WIDTH), lambda i: (i, 4)),
                  const((1, HEAD_DIM)),
                  row(D_MODEL),
                  pl.BlockSpec((1, 6, D_MODEL), lambda i: (_mod_row(i, OUT_TM), 0, 0)),
                  const((D_MODEL, D_MODEL)),
                  const((1, D_MODEL)), const((1, D_MODEL)),
                  const((D_MODEL, N_EXPERTS)), const((1, N_EXPERTS))],
        out_specs=[row(D_MODEL), row(D_MODEL), row(D_MODEL), row(LANE), row(LANE), row(LANE),
                   const((8, N_EXPERTS))],
        out_shape=[jax.ShapeDtypeStruct((T_ROWS, D_MODEL), F32),
                   jax.ShapeDtypeStruct((T_ROWS, D_MODEL), BF16),
                   jax.ShapeDtypeStruct((T_ROWS, D_MODEL), F32),
                   jax.ShapeDtypeStruct((T_ROWS, LANE), jnp.int32),
                   jax.ShapeDtypeStruct((T_ROWS, LANE), F32),
                   jax.ShapeDtypeStruct((T_ROWS, LANE), jnp.int32),
                   jax.ShapeDtypeStruct((8, N_EXPERTS), F32)],
        scratch_shapes=[pltpu.VMEM((8, N_EXPERTS), F32)],
        compiler_params=_cparams(("arbitrary",)),
        name="outproj_ln_router",
    )(a_x, b_x, o_f, o_b, p_hi, hnorm_g, x, mod3, w_out, ln_g, ln_b, router_w, router_b)


def _experts_kernel(ord_ref, elist_ref, cnt_ref, xs_ref, wg_hbm, wu_hbm, wd_hbm, ys_ref,
                    wg_buf, wu_buf, wd_buf, wgu_scr, wd_scr, sems, *, layer):
    i = pl.program_id(0)
    live = i < cnt_ref[0]
    o = ord_ref[i]
    first_of_expert = jnp.logical_or(i == 0, o != ord_ref[jnp.maximum(i - 1, 0)])

    def weight_copies(which, slot):
        e = elist_ref[which]
        return (pltpu.make_async_copy(wg_hbm.at[layer, e], wg_buf.at[slot], sems.at[slot, 0]),
                pltpu.make_async_copy(wu_hbm.at[layer, e], wu_buf.at[slot], sems.at[slot, 1]),
                pltpu.make_async_copy(wd_hbm.at[layer, e], wd_buf.at[slot], sems.at[slot, 2]))

    @pl.when(i == 0)
    def _():
        for c in weight_copies(0, 0):
            c.start()

    @pl.when(jnp.logical_and(live, first_of_expert))
    def _():
        slot = o % 2
        for c in weight_copies(o, slot):
            c.wait()
        wgu_scr[:, :EXPERT_DIM] = wg_buf[slot].astype(BF16)
        wgu_scr[:, EXPERT_DIM:] = wu_buf[slot].astype(BF16)
        wd_scr[...] = wd_buf[slot].astype(BF16)

        @pl.when(o + 1 < cnt_ref[1])
        def _():
            for c in weight_copies(o + 1, 1 - slot):
                c.start()

    @pl.when(live)
    def _():
        x = xs_ref[...].astype(BF16)
        h = jnp.dot(x, wgu_scr[...], preferred_element_type=F32)
        g = h[:, :EXPERT_DIM]
        u = h[:, EXPERT_DIM:]
        a = (g * _sigmoid(g) * u).astype(BF16)
        y = jnp.dot(a, wd_scr[...], preferred_element_type=F32)
        ys_ref[...] = y.astype(ys_ref.dtype)

    @pl.when(jnp.logical_not(live))
    def _():
        ys_ref[...] = jnp.zeros(ys_ref.shape, ys_ref.dtype)


def _experts(tile_ord, expert_list, counts2, xs, w_gate, w_up, w_down, layer):
    tiles = xs.shape[0] // MOE_TM
    last = lambda i, cnt: jnp.minimum(i, cnt[0] - 1)
    grid_spec = pltpu.PrefetchScalarGridSpec(
        num_scalar_prefetch=3,
        grid=(tiles,),
        in_specs=[pl.BlockSpec((MOE_TM, xs.shape[1]), lambda i, od, el, cnt: (last(i, cnt), 0)),
                  pl.BlockSpec(memory_space=pl.ANY),
                  pl.BlockSpec(memory_space=pl.ANY),
                  pl.BlockSpec(memory_space=pl.ANY)],
        out_specs=pl.BlockSpec((MOE_TM, xs.shape[1]), lambda i, od, el, cnt: (i, 0)),
        scratch_shapes=[pltpu.VMEM((2, D_MODEL, EXPERT_DIM), F32),
                        pltpu.VMEM((2, D_MODEL, EXPERT_DIM), F32),
                        pltpu.VMEM((2, EXPERT_DIM, D_MODEL), F32),
                        pltpu.VMEM((D_MODEL, 2 * EXPERT_DIM), BF16),
                        pltpu.VMEM((EXPERT_DIM, D_MODEL), BF16),
                        pltpu.SemaphoreType.DMA((2, 3))],
    )
    return pl.pallas_call(
        functools.partial(_experts_kernel, layer=layer),
        grid_spec=grid_spec,
        out_shape=jax.ShapeDtypeStruct((xs.shape[0], D_MODEL), BF16),
        compiler_params=_cparams(("arbitrary",), vmem=EXPERTS_VMEM_LIMIT),
        name="experts",
    )(tile_ord, expert_list, counts2, xs, w_gate, w_up, w_down)


def _combine_kernel(yg_ref, wt_ref, ysh_ref, x1_ref, mod_ref, g_ref, b_ref, nmod_ref, o_ref, h_ref):
    tm = COMBINE_TM
    y = ysh_ref[...].astype(F32)
    for k in range(TOP_K):
        y = y + wt_ref[:, k:k + 1] * yg_ref[k].astype(F32)
    g2 = mod_ref[0, 5:6, :]
    x2 = _layer_norm_rows(DEEPNORM_ALPHA * x1_ref[...] + g2 * y, g_ref[...], b_ref[...])
    o_ref[...] = x2
    h_ref[...] = (x2 * (1.0 + nmod_ref[0, 1:2, :]) + nmod_ref[0, 0:1, :]).astype(h_ref.dtype)


def _combine(yg, wt, ysh, x1, mod3, ln_g, ln_b, next_mod3):
    tm = COMBINE_TM
    row = lambda width: pl.BlockSpec((tm, width), lambda i: (i, 0))
    mod_spec = pl.BlockSpec((1, 6, D_MODEL), lambda i: (_mod_row(i, tm), 0, 0))
    return pl.pallas_call(
        _combine_kernel,
        grid=(T_ROWS // tm,),
        in_specs=[pl.BlockSpec((TOP_K, tm, D_MODEL), lambda i: (0, i, 0)),
                  row(LANE), row(D_MODEL), row(D_MODEL),
                  mod_spec,
                  pl.BlockSpec((1, D_MODEL), lambda i: (0, 0)),
                  pl.BlockSpec((1, D_MODEL), lambda i: (0, 0)),
                  mod_spec],
        out_specs=[row(D_MODEL), row(D_MODEL)],
        out_shape=[jax.ShapeDtypeStruct((T_ROWS, D_MODEL), F32),
                   jax.ShapeDtypeStruct((T_ROWS, D_MODEL), BF16)],
        compiler_params=_cparams(("parallel",)),
        name="moe_combine_ln",
    )(yg, wt, ysh, x1, mod3, ln_g, ln_b, next_mod3)


def _layout(idx, rank, counts):
    tiles_per = (counts + MOE_TM - 1) // MOE_TM
    tile_end = jnp.cumsum(tiles_per)
    start = (tile_end - tiles_per) * MOE_TM
    experts = jnp.arange(N_EXPERTS, dtype=jnp.int32)
    pos = rank + jnp.sum(jnp.where(idx[:, :, None] == experts, start, 0), axis=-1)
    tile = jnp.arange(MOE_TILES, dtype=jnp.int32)
    tile_expert = jnp.sum((tile_end[None, :] <= tile[:, None]).astype(jnp.int32), axis=-1)
    tile_expert = jnp.minimum(tile_expert, N_EXPERTS - 1)
    used = (counts > 0).astype(jnp.int32)
    used_rank = jnp.cumsum(used) - used
    n_used = jnp.sum(used)
    expert_list = jnp.sum(jnp.where((used_rank[None, :] == experts[:, None]) & (used[None, :] > 0),
                                    experts[None, :], 0), axis=-1)
    tile_ord = jnp.minimum(jnp.take(used_rank, tile_expert), jnp.maximum(n_used - 1, 0))
    counts2 = jnp.stack([tile_end[-1], n_used]).astype(jnp.int32)
    return pos, tile_ord.astype(jnp.int32), expert_list.astype(jnp.int32), counts2


def _dispatch_kernel(pos_ref, hxw_ref, init_hbm, xs_hbm, sem):
    def issue(t, carry):
        for k in range(TOP_K):
            pltpu.make_async_copy(hxw_ref.at[pl.ds(t, 1), :],
                                  xs_hbm.at[pl.ds(pos_ref[t * TOP_K + k], 1), :], sem).start(priority=k % 2)
        return carry

    lax.fori_loop(0, DISPATCH_TM, issue, 0)
    for _ in range(TOP_K):
        pltpu.make_async_copy(hxw_ref, xs_hbm.at[pl.ds(0, DISPATCH_TM), :], sem).wait()


def _dispatch_rows(pos, hx_words, init):
    return pl.pallas_call(
        _dispatch_kernel,
        grid=(T_ROWS // DISPATCH_TM,),
        in_specs=[pl.BlockSpec((DISPATCH_TM * TOP_K,), lambda i: (i,), memory_space=pltpu.SMEM),
                  pl.BlockSpec((DISPATCH_TM, D_MODEL), lambda i: (i, 0)),
                  pl.BlockSpec(memory_space=pl.ANY)],
        out_specs=pl.BlockSpec(memory_space=pl.ANY),
        out_shape=jax.ShapeDtypeStruct((MOE_ROWS, D_MODEL), F32),
        scratch_shapes=[pltpu.SemaphoreType.DMA(())],
        input_output_aliases={2: 0},
        compiler_params=pltpu.CompilerParams(dimension_semantics=("arbitrary",), has_side_effects=True),
        name="moe_dispatch_rows",
    )(pos.reshape(-1), hx_words, init)


def _rope_tables():
    t = jnp.arange(SEQ)
    row = (t // GRID_W).astype(F32)
    col = (t % GRID_W).astype(F32)
    axis_dim = HEAD_DIM // 2
    inv = ROPE_THETA ** (-jnp.arange(0, axis_dim, 2, dtype=F32) / axis_dim)
    ar = row[:, None] * inv
    ac = col[:, None] * inv
    cos = jnp.concatenate([jnp.cos(ar), jnp.cos(ar), jnp.cos(ac), jnp.cos(ac)], axis=-1)
    sin = jnp.concatenate([jnp.sin(ar), jnp.sin(ar), jnp.sin(ac), jnp.sin(ac)], axis=-1)
    odd = ((jnp.arange(HEAD_DIM) // 32) % 2 == 1)[None, :]
    sa = jnp.where(odd, sin, 0.0)
    sb = jnp.where(odd, 0.0, -sin)
    lat = lambda z: jnp.tile(z, (BATCH, 1))
    cos_all = jnp.concatenate([lat(cos), jnp.ones((N_CTX, HEAD_DIM), F32)], axis=0)
    sa_all = jnp.concatenate([lat(sa), jnp.zeros((N_CTX, HEAD_DIM), F32)], axis=0)
    sb_all = jnp.concatenate([lat(sb), jnp.zeros((N_CTX, HEAD_DIM), F32)], axis=0)
    return cos_all, sa_all, sb_all


def kernel(x, c, ctx, c_ctx, w_mod, b_mod, w_in, w_out, gmlp_ln_g, gmlp_ln_b, gmlp_ws, gmlp_bs,
           diff_lam, diff_subln_g, hgrn_lb, hgrn_norm_g, ln1_g, ln1_b, ln2_g, ln2_b,
           router_w, router_b, exp_w_gate, exp_w_up, exp_w_down, sh_w_gate, sh_w_up, sh_w_down):
    assert x.shape == (BATCH, SEQ, D_MODEL) and ctx.shape == (BATCH, CTX_LEN, D_MODEL)
    xs_all = jnp.concatenate([x.reshape(N_LAT, D_MODEL), ctx.reshape(N_CTX, D_MODEL)], axis=0)

    cond = jnp.zeros((8, D_MODEL), F32).at[0:BATCH].set(c).at[BATCH].set(c_ctx)
    mod_all = _modulation(cond, w_mod, b_mod)[:, :BATCH + 1].reshape(DEPTH, BATCH + 1, 6, D_MODEL)

    sm = jax.nn.softmax(hgrn_lb.astype(F32), axis=0)
    lb_all = jnp.cumsum(sm, axis=0) - sm[0]
    tables = _rope_tables()
    xmat_np, masks_np = _scan_structure()
    xmat = jnp.asarray(xmat_np, BF16)
    masks = jnp.asarray(masks_np, F32)

    xs_grouped = jnp.zeros((MOE_ROWS, D_MODEL), F32)
    h_in = _modulate(xs_all, mod_all[0])
    for l in range(DEPTH):
        mod3 = mod_all[l]
        w_in_l = w_in[l].astype(BF16)
        p_lo, p_hi = _inproj(h_in, w_in_l, tables)

        a_x = _gmlp(p_lo, gmlp_ln_g[l][None, :], gmlp_ln_b[l][None, :],
                    gmlp_ws[l].astype(BF16), gmlp_bs[l].T)

        lam_init = 0.8 - 0.6 * math.exp(-0.3 * l)
        dl = diff_lam[l].astype(F32)
        lam = jnp.exp(jnp.sum(dl[0] * dl[1])) - jnp.exp(jnp.sum(dl[2] * dl[3])) + lam_init
        lam2 = jnp.stack([lam, jnp.asarray(1.0 - lam_init, F32)]).astype(F32)
        b_x = _diff_attention(p_lo, lam2, diff_subln_g[l][None, :])

        o_f, o_b = _hgrn_scan(p_hi, lb_all[l], xmat, masks)

        x1, hx, hx_words, idx, wt, rank, cnt = _outproj(
            a_x, b_x, o_f, o_b, p_hi, hgrn_norm_g[l][None, :], xs_all, mod3,
            w_out[l].astype(BF16), ln1_g[l][None, :], ln1_b[l][None, :],
            router_w[l].astype(BF16), router_b[l][None, :])

        pos, tile_ord, expert_list, counts2 = _layout(idx[:, :TOP_K], rank[:, :TOP_K], cnt[0].astype(jnp.int32))
        xs_grouped = _dispatch_rows(pos, hx_words, xs_grouped)
        ys = _experts(tile_ord, expert_list, counts2, xs_grouped, exp_w_gate, exp_w_up, exp_w_down, l)
        yg = jnp.take(ys, pos.T.reshape(-1), axis=0, mode="clip").reshape(TOP_K, T_ROWS, D_MODEL)

        ysh = _experts(jnp.zeros((T_ROWS // MOE_TM,), jnp.int32), jnp.zeros((1,), jnp.int32),
                       jnp.array([T_ROWS // MOE_TM, 1], jnp.int32),
                       hx, sh_w_gate[:, None], sh_w_up[:, None], sh_w_down[:, None], l)

        xs_all, h_in = _combine(yg, wt, ysh, x1, mod3, ln2_g[l][None, :], ln2_b[l][None, :],
                                mod_all[min(l + 1, DEPTH - 1)])

    return xs_all[:N_LAT].reshape(BATCH, SEQ, D_MODEL)
```
